```python
import jax, jax.numpy as jnp
from jax import lax
import numpy as np

D_MODEL = 2048
BATCH = 1
SEQ = 8192
DEPTH = 2

N_META = 16
D_MIX = D_MODEL
N_ATTN_HEADS = 8
HEAD_DIM = 128
ATTN_WIDTH = N_ATTN_HEADS * HEAD_DIM
CONV_WIDTH = D_MIX - ATTN_WIDTH
CONV_GROUPS = 8
CONV_GROUP_DIM = CONV_WIDTH // CONV_GROUPS
CONV_KERNEL = 31
IN_PROJ_WIDTH = 3 * ATTN_WIDTH + N_ATTN_HEADS + 2 * CONV_WIDTH
Q_BLOCK = 128
D_FF = 5632
N_EXPERTS = 8
TOP_K = 2
D_EXPERT = 7168
EXPERT_BLOCK = 128
N_DENSE = (DEPTH + 1) // 2
N_MOE = DEPTH // 2
RMS_EPS = 1e-6
LN_EPS = 1e-5
MASK_VALUE = -1e30
FORGET_BIAS_INIT = 2.0

kernel_name = "hymba_fox_conformer_moe_trunk"


def rms_norm(x, g):
    xf = x.astype(jnp.float32)
    y = xf * lax.rsqrt(jnp.mean(xf * xf, axis=-1, keepdims=True) + RMS_EPS)
    return (y * g.astype(jnp.float32)).astype(x.dtype)


def layer_norm(x, g, b):
    xf = x.astype(jnp.float32)
    mu = jnp.mean(xf, axis=-1, keepdims=True)
    var = jnp.mean(jnp.square(xf - mu), axis=-1, keepdims=True)
    y = (xf - mu) * lax.rsqrt(var + LN_EPS)
    return (y * g.astype(jnp.float32) + b.astype(jnp.float32)).astype(x.dtype)


def group_rms_norm(x, g, n_groups):
    shp = x.shape
    xg = x.reshape(shp[:-1] + (n_groups, shp[-1] // n_groups))
    y = rms_norm(xg, g.reshape(n_groups, shp[-1] // n_groups))
    return y.reshape(shp)


def forgetting_attention(q, k, v, log_f):
    B, L, H, Dh = q.shape
    c = jnp.cumsum(log_f.astype(jnp.float32), axis=1)
    pad = (-L) % Q_BLOCK
    pw = ((0, 0), (pad, 0), (0, 0), (0, 0))
    q = jnp.pad(q, pw)
    k = jnp.pad(k, pw)
    v = jnp.pad(v, pw)
    cT = jnp.pad(c, ((0, 0), (pad, 0), (0, 0))).transpose(0, 2, 1)
    Lp = L + pad
    key_valid = jnp.arange(Lp) >= pad
    scale = Dh ** -0.5
    outs = []
    for blk in range(Lp // Q_BLOCK):
        q0 = blk * Q_BLOCK
        q1 = q0 + Q_BLOCK
        s = jnp.einsum('bqhd,bkhd->bhqk', q[:, q0:q1], k[:, :q1]).astype(jnp.float32) * scale
        decay = cT[:, :, q0:q1, None] - cT[:, :, None, :q1]
        qpos = jnp.arange(q0, q1)
        kpos = jnp.arange(q1)
        mask = (kpos[None, :] <= qpos[:, None]) & key_valid[None, :q1]
        s = jnp.where(mask, s + decay, MASK_VALUE)
        p = jax.nn.softmax(s, axis=-1).astype(v.dtype)
        outs.append(jnp.einsum('bhqk,bkhd->bqhd', p, v[:, :q1]))
    return jnp.concatenate(outs, axis=1)[:, pad:]


def conformer_conv(u, w_dw, b_dw, ln_g, ln_b, w_pw2):
    a, gate = jnp.split(u, 2, axis=-1)
    h = a * jax.nn.sigmoid(gate)
    h = lax.conv_general_dilated(
        h, w_dw[:, None, :].astype(h.dtype), window_strides=(1,),
        padding=[(CONV_KERNEL - 1, 0)],
        dimension_numbers=('NWC', 'WIO', 'NWC'),
        feature_group_count=CONV_WIDTH) + b_dw
    h = jax.nn.silu(layer_norm(h, ln_g, ln_b))
    return h @ w_pw2


def hybrid_mixer(h, w_in, b_forget, w_dw, b_dw, conv_ln_g, conv_ln_b, w_conv_out,
                 attn_out_g, conv_out_g, w_out):
    B, L, _ = h.shape
    z = h @ w_in
    q, k, v, f_logit, u = jnp.split(
        z, [ATTN_WIDTH, 2 * ATTN_WIDTH, 3 * ATTN_WIDTH, 3 * ATTN_WIDTH + N_ATTN_HEADS], axis=-1)
    hs = (B, L, N_ATTN_HEADS, HEAD_DIM)
    log_f = jax.nn.log_sigmoid(f_logit.astype(jnp.float32) + b_forget.astype(jnp.float32))
    attn = forgetting_attention(q.reshape(hs), k.reshape(hs), v.reshape(hs), log_f)
    attn = group_rms_norm(attn.reshape(B, L, ATTN_WIDTH), attn_out_g, N_ATTN_HEADS)
    conv = conformer_conv(u, w_dw, b_dw, conv_ln_g, conv_ln_b, w_conv_out)
    conv = group_rms_norm(conv, conv_out_g, CONV_GROUPS)
    return jnp.concatenate([attn, conv], axis=-1) @ w_out


def swiglu(x, w_gate, w_up, w_down):
    return (jax.nn.silu(x @ w_gate) * (x @ w_up)) @ w_down


def moe_swiglu(x, w_router, w_gate, w_up, w_down):
    B, L, D = x.shape
    xt = x.reshape(-1, D)
    N = xt.shape[0]
    logits = (xt @ w_router).astype(jnp.float32)
    top_logit, top_idx = lax.top_k(logits, TOP_K)
    gates = jax.nn.softmax(top_logit, axis=-1)
    A = N * TOP_K
    flat_e = top_idx.reshape(-1)
    flat_tok = jnp.repeat(jnp.arange(N, dtype=jnp.int32), TOP_K)
    flat_g = gates.reshape(-1)
    order = jnp.argsort(flat_e)
    sorted_e = flat_e[order]
    counts = jnp.bincount(flat_e, length=N_EXPERTS)
    padded = (counts + EXPERT_BLOCK - 1) // EXPERT_BLOCK * EXPERT_BLOCK
    start = jnp.cumsum(counts) - counts
    pends = jnp.cumsum(padded)
    pstart = pends - padded
    dest = pstart[sorted_e] + jnp.arange(A) - start[sorted_e]
    n_blocks = -(-(A + N_EXPERTS * (EXPERT_BLOCK - 1)) // EXPERT_BLOCK)
    cap = n_blocks * EXPERT_BLOCK
    buf_tok = jnp.zeros((cap,), jnp.int32).at[dest].set(flat_tok[order])
    buf_gate = jnp.zeros((cap,), jnp.float32).at[dest].set(flat_g[order])
    block_e = jnp.minimum(
        jnp.searchsorted(pends, jnp.arange(n_blocks) * EXPERT_BLOCK, side='right'), N_EXPERTS - 1)
    xb = xt[buf_tok].reshape(n_blocks, EXPERT_BLOCK, D)

    def expert_block(args):
        xe, e = args
        return swiglu(xe, w_gate[e], w_up[e], w_down[e])

    yb = lax.map(expert_block, (xb, block_e)).reshape(cap, D)
    out = jnp.zeros_like(xt).at[buf_tok].add(yb * buf_gate[:, None].astype(yb.dtype))
    return out.reshape(B, L, D)


def setup_inputs(seed: int = 0) -> dict:
    key = jax.random.key(seed)
    ks = jax.random.split(key, 22)
    f32 = jnp.float32

    def nrm(k, shape, scale):
        return jax.random.normal(k, shape, f32) * scale

    def gain(k, shape):
        return 1.0 + 0.02 * jax.random.normal(k, shape, f32)

    return {
        "x": nrm(ks[0], (BATCH, SEQ, D_MODEL), 1.0),
        "meta_tokens": nrm(ks[1], (N_META, D_MODEL), 1.0),
        "mix_norm_g": gain(ks[2], (DEPTH, D_MODEL)),
        "ffn_norm_g": gain(ks[3], (DEPTH, D_MODEL)),
        "w_in": nrm(ks[4], (DEPTH, D_MODEL, IN_PROJ_WIDTH), D_MODEL ** -0.5),
        "b_forget": FORGET_BIAS_INIT + 0.1 * jax.random.normal(ks[5], (DEPTH, N_ATTN_HEADS), f32),
        "w_dw": nrm(ks[6], (DEPTH, CONV_KERNEL, CONV_WIDTH), CONV_KERNEL ** -0.5),
        "b_dw": nrm(ks[7], (DEPTH, CONV_WIDTH), 0.02),
        "conv_ln_g": gain(ks[8], (DEPTH, CONV_WIDTH)),
        "conv_ln_b": nrm(ks[9], (DEPTH, CONV_WIDTH), 0.02),
        "w_conv_out": nrm(ks[10], (DEPTH, CONV_WIDTH, CONV_WIDTH), CONV_WIDTH ** -0.5),
        "attn_out_g": gain(ks[11], (DEPTH, ATTN_WIDTH)),
        "conv_out_g": gain(ks[12], (DEPTH, CONV_WIDTH)),
        "w_out": nrm(ks[13], (DEPTH, D_MIX, D_MODEL), D_MIX ** -0.5),
        "dense_w_gate": nrm(ks[14], (N_DENSE, D_MODEL, D_FF), D_MODEL ** -0.5),
        "dense_w_up": nrm(ks[15], (N_DENSE, D_MODEL, D_FF), D_MODEL ** -0.5),
        "dense_w_down": nrm(ks[16], (N_DENSE, D_FF, D_MODEL), D_FF ** -0.5),
        "moe_w_router": nrm(ks[17], (N_MOE, D_MODEL, N_EXPERTS), D_MODEL ** -0.5),
        "moe_w_gate": nrm(ks[18], (N_MOE, N_EXPERTS, D_MODEL, D_EXPERT), D_MODEL ** -0.5),
        "moe_w_up": nrm(ks[19], (N_MOE, N_EXPERTS, D_MODEL, D_EXPERT), D_MODEL ** -0.5),
        "moe_w_down": nrm(ks[20], (N_MOE, N_EXPERTS, D_EXPERT, D_MODEL), D_EXPERT ** -0.5),
        "final_norm_g": gain(ks[21], (D_MODEL,)),
    }


def reference(x, meta_tokens, mix_norm_g, ffn_norm_g, w_in, b_forget, w_dw, b_dw,
              conv_ln_g, conv_ln_b, w_conv_out, attn_out_g, conv_out_g, w_out,
              dense_w_gate, dense_w_up, dense_w_down, moe_w_router, moe_w_gate,
              moe_w_up, moe_w_down, final_norm_g):
    B = x.shape[0]
    meta = jnp.broadcast_to(meta_tokens.astype(x.dtype)[None], (B, N_META, x.shape[-1]))
    h = jnp.concatenate([meta, x], axis=1)
    for layer in range(DEPTH):
        h = h + hybrid_mixer(
            rms_norm(h, mix_norm_g[layer]), w_in[layer], b_forget[layer], w_dw[layer],
            b_dw[layer], conv_ln_g[layer], conv_ln_b[layer], w_conv_out[layer],
            attn_out_g[layer], conv_out_g[layer], w_out[layer])
        hn = rms_norm(h, ffn_norm_g[layer])
        j = layer // 2
        if layer % 2 == 0:
            h = h + swiglu(hn, dense_w_gate[j], dense_w_up[j], dense_w_down[j])
        else:
            h = h + moe_swiglu(hn, moe_w_router[j], moe_w_gate[j], moe_w_up[j], moe_w_down[j])
    h = rms_norm(h, final_norm_g)
    return h[:, N_META:]
```

```python
import functools

import jax
import jax.numpy as jnp
from jax import lax
from jax.experimental import pallas as pl
from jax.experimental.pallas import tpu as pltpu

F32 = jnp.float32
BF16 = jnp.bfloat16

D_MODEL = 2048
SEQ = 8192
N_META = 16
N_HEADS = 8
HEAD_DIM = 128
ATTN_WIDTH = N_HEADS * HEAD_DIM
CONV_WIDTH = 1024
CONV_GROUPS = 8
CONV_KERNEL = 31
D_FF = 5632
N_EXPERTS = 8
D_EXPERT = 7168
RMS_EPS = 1e-6
LN_EPS = 1e-5
MASK_VALUE = -1e30

LANE = 128
BLK = 128
PAD = BLK - N_META
LP = PAD + N_META + SEQ
TM = 640
NT = LP // TM
NB = LP // BLK
HALO = 32
MOE_TM = 256
MOE_TILES = -(-(2 * LP + N_EXPERTS * (MOE_TM - 1)) // MOE_TM)
MOE_CAP = MOE_TILES * MOE_TM
MOE_TN_A = 1024
MOE_TN_B = 512
FFN_TF = 512
OUT_T = 512
VMEM_LIMIT = 56 * 1024 * 1024


def _cparams(sem):
    return pltpu.CompilerParams(dimension_semantics=sem, vmem_limit_bytes=VMEM_LIMIT)


def _rms(x, g):
    return x * lax.rsqrt(jnp.mean(x * x, axis=-1, keepdims=True) + RMS_EPS) * g


def _silu(x):
    return x * jax.nn.sigmoid(x)


def _prep_kernel(meta_ref, x_ref, g_ref, h_ref, hn_ref):
    i = pl.program_id(0)
    v = jnp.where(i == 0, meta_ref[...], x_ref[...])
    h_ref[...] = v
    hn_ref[...] = _rms(v, g_ref[...]).astype(BF16)


def _prep(meta_blk, x2d, g):
    return pl.pallas_call(
        _prep_kernel,
        grid=(NB,),
        in_specs=[
            pl.BlockSpec((BLK, D_MODEL), lambda i: (0, 0)),
            pl.BlockSpec((BLK, D_MODEL), lambda i: (jnp.maximum(i - 1, 0), 0)),
            pl.BlockSpec((1, D_MODEL), lambda i: (0, 0)),
        ],
        out_specs=[
            pl.BlockSpec((BLK, D_MODEL), lambda i: (i, 0)),
            pl.BlockSpec((BLK, D_MODEL), lambda i: (i, 0)),
        ],
        out_shape=[
            jax.ShapeDtypeStruct((LP, D_MODEL), F32),
            jax.ShapeDtypeStruct((LP, D_MODEL), BF16),
        ],
        compiler_params=_cparams(("arbitrary",)),
        name="prep",
    )(meta_blk, x2d, g)


def _qkv_kernel(x_ref, w_ref, o_ref):
    j = pl.program_id(1)
    acc = jnp.dot(x_ref[...], w_ref[...], preferred_element_type=F32)
    scale = jnp.where(j == 0, HEAD_DIM ** -0.5, 1.0).astype(F32)
    o_ref[...] = (acc * scale).astype(BF16)


def _qkv(hn, w_qkv):
    return pl.pallas_call(
        _qkv_kernel,
        grid=(NT, 3),
        in_specs=[
            pl.BlockSpec((TM, D_MODEL), lambda i, j: (i, 0)),
            pl.BlockSpec((D_MODEL, ATTN_WIDTH), lambda i, j: (0, j)),
        ],
        out_specs=pl.BlockSpec((TM, ATTN_WIDTH), lambda i, j: (i, j)),
        out_shape=jax.ShapeDtypeStruct((LP, 3 * ATTN_WIDTH), BF16),
        compiler_params=_cparams(("arbitrary", "arbitrary")),
        name="qkv",
    )(hn, w_qkv)


def _glu_kernel(x_ref, wa_ref, wg_ref, o_ref):
    i = pl.program_id(0)
    x = x_ref[...]
    a = jnp.dot(x, wa_ref[...], preferred_element_type=F32)
    g = jnp.dot(x, wg_ref[...], preferred_element_type=F32)
    row = lax.broadcasted_iota(jnp.int32, a.shape, 0) + i * TM
    o_ref[...] = jnp.where(row >= PAD, a * jax.nn.sigmoid(g), 0.0)


def _glu(hn, w_a, w_g):
    tn = 512
    return pl.pallas_call(
        _glu_kernel,
        grid=(NT, CONV_WIDTH // tn),
        in_specs=[
            pl.BlockSpec((TM, D_MODEL), lambda i, j: (i, 0)),
            pl.BlockSpec((D_MODEL, tn), lambda i, j: (0, j)),
            pl.BlockSpec((D_MODEL, tn), lambda i, j: (0, j)),
        ],
        out_specs=pl.BlockSpec((TM, tn), lambda i, j: (i, j)),
        out_shape=jax.ShapeDtypeStruct((LP, CONV_WIDTH), F32),
        compiler_params=_cparams(("arbitrary", "arbitrary")),
        name="glu",
    )(hn, w_a, w_g)


def _split3(x):
    hi = x.astype(BF16)
    r1 = x - hi.astype(F32)
    mid = r1.astype(BF16)
    lo = (r1 - mid.astype(F32)).astype(BF16)
    return hi, mid, lo


def _decay_kernel(x_ref, w_ref, b_ref, c_ref, carry_ref):
    i = pl.program_id(0)

    @pl.when(i == 0)
    def _():
        carry_ref[...] = jnp.zeros_like(carry_ref)

    f = jnp.dot(x_ref[...], w_ref[...], preferred_element_type=F32) + b_ref[...]
    log_f = jnp.minimum(f, 0.0) - jnp.log1p(jnp.exp(-jnp.abs(f)))
    row = lax.broadcasted_iota(jnp.int32, log_f.shape, 0) + i * TM
    log_f = jnp.where(row >= PAD, log_f, 0.0)
    r = lax.broadcasted_iota(jnp.int32, (TM, TM), 0)
    c = lax.broadcasted_iota(jnp.int32, (TM, TM), 1)
    tri = jnp.where(c <= r, 1.0, 0.0).astype(BF16)
    hi, mid, lo = _split3(log_f)
    cs = (jnp.dot(tri, hi, preferred_element_type=F32)
          + jnp.dot(tri, mid, preferred_element_type=F32)
          + jnp.dot(tri, lo, preferred_element_type=F32)) + carry_ref[0:1, :]
    c_ref[...] = cs
    carry_ref[...] = jnp.broadcast_to(cs[TM - 1:TM, :], carry_ref.shape)


def _decay(hn, w_f, b_f):
    return pl.pallas_call(
        _decay_kernel,
        grid=(NT,),
        in_specs=[
            pl.BlockSpec((TM, D_MODEL), lambda i: (i, 0)),
            pl.BlockSpec((D_MODEL, LANE), lambda i: (0, 0)),
            pl.BlockSpec((1, LANE), lambda i: (0, 0)),
        ],
        out_specs=pl.BlockSpec((TM, LANE), lambda i: (i, 0)),
        out_shape=jax.ShapeDtypeStruct((LP, LANE), F32),
        scratch_shapes=[pltpu.VMEM((8, LANE), F32)],
        compiler_params=_cparams(("arbitrary",)),
        name="decay",
    )(hn, w_f, b_f)


def _attn_kernel(qi_ref, kj_ref, q_ref, k_ref, v_ref, c_ref, g_ref, o_ref, m_ref, l_ref, acc_ref):
    t = pl.program_id(1)
    i = qi_ref[t]
    j = kj_ref[t]

    @pl.when(j == 0)
    def _():
        m_ref[...] = jnp.full_like(m_ref, -1e38)
        l_ref[...] = jnp.zeros_like(l_ref)
        acc_ref[...] = jnp.zeros_like(acc_ref)

    s = lax.dot_general(q_ref[...], k_ref[...], (((1,), (1,)), ((), ())),
                        preferred_element_type=F32)
    s = s - c_ref[0]
    row = lax.broadcasted_iota(jnp.int32, s.shape, 0) + i * TM
    col = lax.broadcasted_iota(jnp.int32, s.shape, 1) + j * TM
    s = jnp.where(col <= row, s, MASK_VALUE)
    m_prev = m_ref[...]
    m_new = jnp.maximum(m_prev, jnp.max(s, axis=-1, keepdims=True))
    alpha = jnp.exp(m_prev - m_new)
    p = jnp.exp(s - m_new)
    l_ref[...] = alpha * l_ref[...] + jnp.sum(p, axis=-1, keepdims=True)
    acc_ref[...] = alpha * acc_ref[...] + jnp.dot(p.astype(BF16), v_ref[...],
                                                  preferred_element_type=F32)
    m_ref[...] = m_new

    @pl.when(j == i)
    def _():
        o = acc_ref[...] / l_ref[...]
        o_ref[...] = _rms(o, g_ref[0]).astype(BF16)


def _attention(qkv, c_t, g_attn):
    qi = jnp.asarray([i for i in range(NT) for _ in range(i + 1)], jnp.int32)
    kj = jnp.asarray([j for i in range(NT) for j in range(i + 1)], jnp.int32)
    grid_spec = pltpu.PrefetchScalarGridSpec(
        num_scalar_prefetch=2,
        grid=(N_HEADS, int(qi.shape[0])),
        in_specs=[
            pl.BlockSpec((TM, HEAD_DIM), lambda h, t, qi, kj: (qi[t], h)),
            pl.BlockSpec((TM, HEAD_DIM), lambda h, t, qi, kj: (kj[t], N_HEADS + h)),
            pl.BlockSpec((TM, HEAD_DIM), lambda h, t, qi, kj: (kj[t], 2 * N_HEADS + h)),
            pl.BlockSpec((1, 1, TM), lambda h, t, qi, kj: (h, 0, kj[t])),
            pl.BlockSpec((1, 1, HEAD_DIM), lambda h, t, qi, kj: (h, 0, 0)),
        ],
        out_specs=pl.BlockSpec((TM, HEAD_DIM), lambda h, t, qi, kj: (qi[t], h)),
        scratch_shapes=[
            pltpu.VMEM((TM, 1), F32),
            pltpu.VMEM((TM, 1), F32),
            pltpu.VMEM((TM, HEAD_DIM), F32),
        ],
    )
    return pl.pallas_call(
        _attn_kernel,
        grid_spec=grid_spec,
        out_shape=jax.ShapeDtypeStruct((LP, ATTN_WIDTH), BF16),
        compiler_params=_cparams(("arbitrary", "arbitrary")),
        name="attention",
    )(qi, kj, qkv, qkv, qkv, c_t, g_attn)


CONV_RC = 64
CONV_CC = 512


def _conv_kernel(x_ref, halo_ref, wdw_ref, bdw_ref, lng_ref, lnb_ref, wpw_ref, og_ref, o_ref,
                 xs_ref, cv_ref):
    i = pl.program_id(0)
    xs_ref[0:HALO, :] = jnp.where(i == 0, 0.0, halo_ref[...])
    xs_ref[HALO:HALO + TM, :] = x_ref[...]
    off = HALO - (CONV_KERNEL - 1)
    for rc in range(TM // CONV_RC):
        for cc in range(CONV_WIDTH // CONV_CC):
            cs = slice(cc * CONV_CC, (cc + 1) * CONV_CC)
            acc = jnp.broadcast_to(bdw_ref[:, cs], (CONV_RC, CONV_CC))
            for j in range(CONV_KERNEL):
                r0 = rc * CONV_RC + off + j
                acc = acc + wdw_ref[j:j + 1, cs] * xs_ref[r0:r0 + CONV_RC, cs]
            cv_ref[rc * CONV_RC:(rc + 1) * CONV_RC, cs] = acc
    y = cv_ref[...]
    mu = jnp.mean(y, axis=-1, keepdims=True)
    yc = y - mu
    var = jnp.mean(yc * yc, axis=-1, keepdims=True)
    a = _silu(yc * lax.rsqrt(var + LN_EPS) * lng_ref[...] + lnb_ref[...])
    z = jnp.dot(a.astype(BF16), wpw_ref[...], preferred_element_type=F32)
    gd = CONV_WIDTH // CONV_GROUPS
    for g in range(CONV_GROUPS):
        gs = slice(g * gd, (g + 1) * gd)
        o_ref[:, gs] = _rms(z[:, gs], og_ref[:, gs]).astype(BF16)


def _conv(hc, w_dw, b_dw, ln_g, ln_b, w_pw, out_g):
    row = lambda i: (0, 0)
    return pl.pallas_call(
        _conv_kernel,
        grid=(NT,),
        in_specs=[
            pl.BlockSpec((TM, CONV_WIDTH), lambda i: (i, 0)),
            pl.BlockSpec((HALO, CONV_WIDTH), lambda i: (jnp.maximum(i * (TM // HALO) - 1, 0), 0)),
            pl.BlockSpec((HALO, CONV_WIDTH), row),
            pl.BlockSpec((1, CONV_WIDTH), row),
            pl.BlockSpec((1, CONV_WIDTH), row),
            pl.BlockSpec((1, CONV_WIDTH), row),
            pl.BlockSpec((CONV_WIDTH, CONV_WIDTH), row),
            pl.BlockSpec((1, CONV_WIDTH), row),
        ],
        out_specs=pl.BlockSpec((TM, CONV_WIDTH), lambda i: (i, 0)),
        out_shape=jax.ShapeDtypeStruct((LP, CONV_WIDTH), BF16),
        scratch_shapes=[
            pltpu.VMEM((HALO + TM, CONV_WIDTH), F32),
            pltpu.VMEM((TM, CONV_WIDTH), F32),
        ],
        compiler_params=_cparams(("arbitrary",)),
        name="conv",
    )(hc, hc, w_dw, b_dw, ln_g, ln_b, w_pw, out_g)


def _outproj_kernel(a_ref, c_ref, w_ref, h_ref, g_ref, ho_ref, hn_ref):
    acc = jnp.dot(a_ref[...], w_ref[0:ATTN_WIDTH, :], preferred_element_type=F32)
    acc = acc + jnp.dot(c_ref[...], w_ref[ATTN_WIDTH:, :], preferred_element_type=F32)
    h_new = h_ref[...] + acc
    ho_ref[...] = h_new
    hn_ref[...] = _rms(h_new, g_ref[...]).astype(hn_ref.dtype)


def _outproj(attn, conv, w_out, h, g_next, hn_dtype):
    return pl.pallas_call(
        _outproj_kernel,
        grid=(NT,),
        in_specs=[
            pl.BlockSpec((TM, ATTN_WIDTH), lambda i: (i, 0)),
            pl.BlockSpec((TM, CONV_WIDTH), lambda i: (i, 0)),
            pl.BlockSpec((D_MODEL, D_MODEL), lambda i: (0, 0)),
            pl.BlockSpec((TM, D_MODEL), lambda i: (i, 0)),
            pl.BlockSpec((1, D_MODEL), lambda i: (0, 0)),
        ],
        out_specs=[
            pl.BlockSpec((TM, D_MODEL), lambda i: (i, 0)),
            pl.BlockSpec((TM, D_MODEL), lambda i: (i, 0)),
        ],
        out_shape=[
            jax.ShapeDtypeStruct((LP, D_MODEL), F32),
            jax.ShapeDtypeStruct((LP, D_MODEL), hn_dtype),
        ],
        compiler_params=_cparams(("arbitrary",)),
        name="outproj",
    )(attn, conv, w_out, h, g_next)


def _ffn_kernel(x_ref, wg_ref, wu_ref, wd_ref, h_ref, g_ref, ho_ref, hn_ref, acc_ref):
    j = pl.program_id(1)

    @pl.when(j == 0)
    def _():
        acc_ref[...] = jnp.zeros_like(acc_ref)

    x = x_ref[...]
    g = jnp.dot(x, wg_ref[...], preferred_element_type=F32)
    u = jnp.dot(x, wu_ref[...], preferred_element_type=F32)
    a = (_silu(g) * u).astype(BF16)
    acc_ref[...] += jnp.dot(a, wd_ref[...], preferred_element_type=F32)

    @pl.when(j == pl.num_programs(1) - 1)
    def _():
        h_new = h_ref[...] + acc_ref[...]
        ho_ref[...] = h_new
        hn_ref[...] = _rms(h_new, g_ref[...]).astype(BF16)


def _ffn(hn, w_gate, w_up, w_down, h, g_next):
    return pl.pallas_call(
        _ffn_kernel,
        grid=(NT, D_FF // FFN_TF),
        in_specs=[
            pl.BlockSpec((TM, D_MODEL), lambda i, j: (i, 0)),
            pl.BlockSpec((D_MODEL, FFN_TF), lambda i, j: (0, j)),
            pl.BlockSpec((D_MODEL, FFN_TF), lambda i, j: (0, j)),
            pl.BlockSpec((FFN_TF, D_MODEL), lambda i, j: (j, 0)),
            pl.BlockSpec((TM, D_MODEL), lambda i, j: (i, 0)),
            pl.BlockSpec((1, D_MODEL), lambda i, j: (0, 0)),
        ],
        out_specs=[
            pl.BlockSpec((TM, D_MODEL), lambda i, j: (i, 0)),
            pl.BlockSpec((TM, D_MODEL), lambda i, j: (i, 0)),
        ],
        out_shape=[
            jax.ShapeDtypeStruct((LP, D_MODEL), F32),
            jax.ShapeDtypeStruct((LP, D_MODEL), BF16),
        ],
        scratch_shapes=[pltpu.VMEM((TM, D_MODEL), F32)],
        compiler_params=_cparams(("arbitrary", "arbitrary")),
        name="ffn",
    )(hn, w_gate, w_up, w_down, h, g_next)


INFO_E1, INFO_E2, INFO_G1, INFO_G2, INFO_R1, INFO_R2 = range(6)


def _router_kernel(x_ref, w_ref, info_ref, cnt_ref, carry_ref):
    i = pl.program_id(0)

    @pl.when(i == 0)
    def _():
        carry_ref[...] = jnp.zeros_like(carry_ref)

    logits = jnp.dot(x_ref[...], w_ref[...], preferred_element_type=F32,
                     precision=lax.Precision.HIGHEST)
    lane = lax.broadcasted_iota(jnp.int32, logits.shape, 1)
    neg = jnp.float32(-3e38)
    l1 = jnp.where(lane < N_EXPERTS, logits, neg)
    m1 = jnp.max(l1, axis=-1, keepdims=True)
    i1 = jnp.min(jnp.where(l1 == m1, lane, LANE), axis=-1, keepdims=True)
    l2 = jnp.where(lane == i1, neg, l1)
    m2 = jnp.max(l2, axis=-1, keepdims=True)
    i2 = jnp.min(jnp.where(l2 == m2, lane, LANE), axis=-1, keepdims=True)
    ex = jnp.exp(m2 - m1)
    g1 = 1.0 / (1.0 + ex)
    g2 = ex / (1.0 + ex)
    sel = jnp.where(lane == i1, 1.0, jnp.where(lane == i2, 1.0, 0.0))
    r = lax.broadcasted_iota(jnp.int32, (BLK, BLK), 0)
    c = lax.broadcasted_iota(jnp.int32, (BLK, BLK), 1)
    tri = jnp.where(c < r, 1.0, 0.0).astype(BF16)
    carry = carry_ref[0:1, :]
    rank = jnp.dot(tri, sel.astype(BF16), preferred_element_type=F32) + carry
    r1 = jnp.sum(jnp.where(lane == i1, rank, 0.0), axis=-1, keepdims=True)
    r2 = jnp.sum(jnp.where(lane == i2, rank, 0.0), axis=-1, keepdims=True)
    info = jnp.zeros_like(logits)
    for k, v in ((INFO_E1, i1.astype(F32)), (INFO_E2, i2.astype(F32)), (INFO_G1, g1),
                 (INFO_G2, g2), (INFO_R1, r1), (INFO_R2, r2)):
        info = jnp.where(lane == k, v, info)
    info_ref[...] = info
    total = carry + jnp.sum(sel, axis=0, keepdims=True)
    carry_ref[...] = jnp.broadcast_to(total, carry_ref.shape)
    cnt_ref[...] = jnp.broadcast_to(total, cnt_ref.shape)


def _router(hn, w_r):
    return pl.pallas_call(
        _router_kernel,
        grid=(NB,),
        in_specs=[
            pl.BlockSpec((BLK, D_MODEL), lambda i: (i, 0)),
            pl.BlockSpec((D_MODEL, LANE), lambda i: (0, 0)),
        ],
        out_specs=[
            pl.BlockSpec((BLK, LANE), lambda i: (i, 0)),
            pl.BlockSpec((8, LANE), lambda i: (0, 0)),
        ],
        out_shape=[
            jax.ShapeDtypeStruct((LP, LANE), F32),
            jax.ShapeDtypeStruct((8, LANE), F32),
        ],
        scratch_shapes=[pltpu.VMEM((8, LANE), F32)],
        compiler_params=_cparams(("arbitrary",)),
        name="router",
    )(hn, w_r)


def _dispatch_kernel(pos_ref, x_ref, xs_in_ref, xs_ref, sem):
    del xs_in_ref
    i = pl.program_id(0)

    def issue(r, carry):
        base = (i * BLK + r) * 2
        for k in range(2):
            pltpu.make_async_copy(x_ref.at[pl.ds(r, 1)],
                                  xs_ref.at[pl.ds(pos_ref[base + k], 1)], sem).start()
        return carry

    lax.fori_loop(0, BLK, issue, 0)
    for _ in range(2):
        pltpu.make_async_copy(x_ref, xs_ref.at[pl.ds(0, BLK)], sem).wait()


def _dispatch(pos, hn):
    grid_spec = pltpu.PrefetchScalarGridSpec(
        num_scalar_prefetch=1,
        grid=(NB,),
        in_specs=[
            pl.BlockSpec((BLK, D_MODEL), lambda i, pos: (i, 0)),
            pl.BlockSpec(memory_space=pl.ANY),
        ],
        out_specs=pl.BlockSpec(memory_space=pl.ANY),
        scratch_shapes=[pltpu.SemaphoreType.DMA(())],
    )
    return pl.pallas_call(
        _dispatch_kernel,
        grid_spec=grid_spec,
        out_shape=jax.ShapeDtypeStruct((MOE_CAP, D_MODEL), F32),
        input_output_aliases={2: 0},
        compiler_params=_cparams(("arbitrary",)),
        name="dispatch",
    )(pos, hn, jnp.zeros((MOE_CAP, D_MODEL), F32))


def _moe_up_kernel(te_ref, nu_ref, x_ref, wg_ref, wu_ref, o_ref):
    i = pl.program_id(1)

    @pl.when(i < nu_ref[0])
    def _():
        x = x_ref[...].astype(BF16)
        g = jnp.dot(x, wg_ref[0], preferred_element_type=F32)
        u = jnp.dot(x, wu_ref[0], preferred_element_type=F32)
        o_ref[...] = (_silu(g) * u).astype(BF16)

    @pl.when(i >= nu_ref[0])
    def _():
        o_ref[...] = jnp.zeros_like(o_ref)


def _moe_up(te, nu, xs, w_gate, w_up):
    grid_spec = pltpu.PrefetchScalarGridSpec(
        num_scalar_prefetch=2,
        grid=(D_EXPERT // MOE_TN_A, MOE_TILES),
        in_specs=[
            pl.BlockSpec((MOE_TM, D_MODEL), lambda j, i, te, nu: (i, 0)),
            pl.BlockSpec((1, D_MODEL, MOE_TN_A), lambda j, i, te, nu: (te[i], 0, j)),
            pl.BlockSpec((1, D_MODEL, MOE_TN_A), lambda j, i, te, nu: (te[i], 0, j)),
        ],
        out_specs=pl.BlockSpec((MOE_TM, MOE_TN_A), lambda j, i, te, nu: (i, j)),
    )
    return pl.pallas_call(
        _moe_up_kernel,
        grid_spec=grid_spec,
        out_shape=jax.ShapeDtypeStruct((MOE_CAP, D_EXPERT), BF16),
        compiler_params=_cparams(("arbitrary", "arbitrary")),
        name="moe_up",
    )(te, nu, xs, w_gate, w_up)


def _moe_down_kernel(te_ref, nu_ref, h_ref, wd_ref, o_ref):
    i = pl.program_id(1)

    @pl.when(i < nu_ref[0])
    def _():
        o_ref[...] = jnp.dot(h_ref[...], wd_ref[0], preferred_element_type=F32)

    @pl.when(i >= nu_ref[0])
    def _():
        o_ref[...] = jnp.zeros_like(o_ref)


def _moe_down(te, nu, h1, w_down):
    grid_spec = pltpu.PrefetchScalarGridSpec(
        num_scalar_prefetch=2,
        grid=(D_MODEL // MOE_TN_B, MOE_TILES),
        in_specs=[
            pl.BlockSpec((MOE_TM, D_EXPERT), lambda j, i, te, nu: (i, 0)),
            pl.BlockSpec((1, D_EXPERT, MOE_TN_B), lambda j, i, te, nu: (te[i], 0, j)),
        ],
        out_specs=pl.BlockSpec((MOE_TM, MOE_TN_B), lambda j, i, te, nu: (i, j)),
    )
    return pl.pallas_call(
        _moe_down_kernel,
        grid_spec=grid_spec,
        out_shape=jax.ShapeDtypeStruct((MOE_CAP, D_MODEL), F32),
        compiler_params=_cparams(("arbitrary", "arbitrary")),
        name="moe_down",
    )(te, nu, h1, w_down)


def _combine_kernel(pos_ref, info_ref, g_ref, h_hbm, y_hbm, o_ref, hbuf, ybuf, sem_h, sem_y):
    i = pl.program_id(0)
    h_copy = pltpu.make_async_copy(h_hbm.at[pl.ds(BLK + i * OUT_T, OUT_T)], hbuf, sem_h)
    h_copy.start()

    def issue(r, carry):
        base = (i * OUT_T + r) * 2
        for k in range(2):
            pltpu.make_async_copy(y_hbm.at[pl.ds(pos_ref[base + k], 1)],
                                  ybuf.at[k, pl.ds(r, 1)], sem_y).start()
        return carry

    lax.fori_loop(0, OUT_T, issue, 0)
    h_copy.wait()
    for k in range(2):
        pltpu.make_async_copy(y_hbm.at[pl.ds(0, OUT_T)], ybuf.at[k], sem_y).wait()
    g1 = info_ref[:, INFO_G1:INFO_G1 + 1]
    g2 = info_ref[:, INFO_G2:INFO_G2 + 1]
    h_new = hbuf[...] + g1 * ybuf[0] + g2 * ybuf[1]
    o_ref[...] = _rms(h_new, g_ref[...])


def _combine(pos_x, info_x, g_final, h, y):
    grid_spec = pltpu.PrefetchScalarGridSpec(
        num_scalar_prefetch=1,
        grid=(SEQ // OUT_T,),
        in_specs=[
            pl.BlockSpec((OUT_T, LANE), lambda i, pos: (i, 0)),
            pl.BlockSpec((1, D_MODEL), lambda i, pos: (0, 0)),
            pl.BlockSpec(memory_space=pl.ANY),
            pl.BlockSpec(memory_space=pl.ANY),
        ],
        out_specs=pl.BlockSpec((OUT_T, D_MODEL), lambda i, pos: (i, 0)),
        scratch_shapes=[
            pltpu.VMEM((OUT_T, D_MODEL), F32),
            pltpu.VMEM((2, OUT_T, D_MODEL), F32),
            pltpu.SemaphoreType.DMA(()),
            pltpu.SemaphoreType.DMA(()),
        ],
    )
    return pl.pallas_call(
        _combine_kernel,
        grid_spec=grid_spec,
        out_shape=jax.ShapeDtypeStruct((SEQ, D_MODEL), F32),
        compiler_params=_cparams(("arbitrary",)),
        name="combine",
    )(pos_x, info_x, g_final, h, y)


def _mixer(h, hn, layer, p, g_next, hn_dtype):
    w_in = p["w_in"][layer]
    a0 = 3 * ATTN_WIDTH
    u0 = a0 + N_HEADS
    w_qkv = w_in[:, :a0].astype(BF16)
    w_f = jnp.pad(w_in[:, a0:u0], ((0, 0), (0, LANE - N_HEADS))).astype(BF16)
    w_a = w_in[:, u0:u0 + CONV_WIDTH].astype(BF16)
    w_g = w_in[:, u0 + CONV_WIDTH:].astype(BF16)
    b_f = jnp.pad(p["b_forget"][layer], (0, LANE - N_HEADS)).reshape(1, LANE)

    qkv = _qkv(hn, w_qkv)
    c = _decay(hn, w_f, b_f)
    key_pad = jnp.arange(LP) < PAD
    c_t = jnp.where(key_pad[None, :], -MASK_VALUE, c[:, :N_HEADS].T).reshape(N_HEADS, 1, LP)
    attn = _attention(qkv, c_t, p["attn_out_g"][layer].reshape(N_HEADS, 1, HEAD_DIM))

    hc = _glu(hn, w_a, w_g)
    w_dw = jnp.pad(p["w_dw"][layer], ((0, HALO - CONV_KERNEL), (0, 0)))
    row = lambda v: v.reshape(1, -1)
    conv = _conv(hc, w_dw, row(p["b_dw"][layer]), row(p["conv_ln_g"][layer]),
                 row(p["conv_ln_b"][layer]), p["w_conv_out"][layer].astype(BF16),
                 row(p["conv_out_g"][layer]))
    return _outproj(attn, conv, p["w_out"][layer].astype(BF16), h, g_next, hn_dtype)


def _moe(h, hn, p, j, g_final):
    w_r = jnp.pad(p["moe_w_router"][j], ((0, 0), (0, LANE - N_EXPERTS)))
    info, cnt = _router(hn, w_r)
    e1 = info[:, INFO_E1].astype(jnp.int32)
    e2 = info[:, INFO_E2].astype(jnp.int32)
    r1 = info[:, INFO_R1].astype(jnp.int32)
    r2 = info[:, INFO_R2].astype(jnp.int32)
    counts = cnt[0, :N_EXPERTS].astype(jnp.int32)
    padded = (counts + MOE_TM - 1) // MOE_TM * MOE_TM
    gend = jnp.cumsum(padded)
    gstart = gend - padded
    pos = jnp.stack([gstart[e1] + r1, gstart[e2] + r2], axis=-1).reshape(-1)
    n_used = gend[N_EXPERTS - 1] // MOE_TM
    tiles = jnp.minimum(jnp.arange(MOE_TILES, dtype=jnp.int32), n_used - 1)
    te = jnp.minimum(jnp.searchsorted(gend, tiles * MOE_TM, side="right"),
                     N_EXPERTS - 1).astype(jnp.int32)
    nu = n_used.reshape(1).astype(jnp.int32)

    xs = _dispatch(pos, hn)
    h1 = _moe_up(te, nu, xs, p["moe_w_gate"][j].astype(BF16), p["moe_w_up"][j].astype(BF16))
    y = _moe_down(te, nu, h1, p["moe_w_down"][j].astype(BF16))
    return _combine(pos[2 * BLK:], info[BLK:], g_final, h, y)


def kernel(x, meta_tokens, mix_norm_g, ffn_norm_g, w_in, b_forget, w_dw, b_dw, conv_ln_g, conv_ln_b, w_conv_out, attn_out_g, conv_out_g, w_out, dense_w_gate, dense_w_up, dense_w_down, moe_w_router, moe_w_gate, moe_w_up, moe_w_down, final_norm_g):
    assert x.shape == (1, SEQ, D_MODEL) and meta_tokens.shape == (N_META, D_MODEL)
    p = dict(w_in=w_in, b_forget=b_forget, w_dw=w_dw, b_dw=b_dw, conv_ln_g=conv_ln_g,
             conv_ln_b=conv_ln_b, w_conv_out=w_conv_out, attn_out_g=attn_out_g,
             conv_out_g=conv_out_g, w_out=w_out, moe_w_router=moe_w_router,
             moe_w_gate=moe_w_gate, moe_w_up=moe_w_up, moe_w_down=moe_w_down)
    row = lambda v: v.reshape(1, D_MODEL)
    meta_blk = jnp.pad(meta_tokens.astype(F32), ((PAD, 0), (0, 0)))
    h, hn = _prep(meta_blk, x.reshape(SEQ, D_MODEL), row(mix_norm_g[0]))

    h, hn = _mixer(h, hn, 0, p, row(ffn_norm_g[0]), BF16)
    h, hn = _ffn(hn, dense_w_gate[0].astype(BF16), dense_w_up[0].astype(BF16),
                 dense_w_down[0].astype(BF16), h, row(mix_norm_g[1]))
    h, hn = _mixer(h, hn, 1, p, row(ffn_norm_g[1]), F32)
    out = _moe(h, hn, p, 0, row(final_norm_g))
    return out.reshape(1, SEQ, D_MODEL)
```

```python
import functools

import jax
import jax.numpy as jnp
from jax import lax
from jax.experimental import pallas as pl
from jax.experimental.pallas import tpu as pltpu

F32 = jnp.float32
BF16 = jnp.bfloat16

D_MODEL = 2048
SEQ = 8192
N_META = 16
N_HEADS = 8
HEAD_DIM = 128
ATTN_WIDTH = N_HEADS * HEAD_DIM
CONV_WIDTH = 1024
CONV_GROUPS = 8
CONV_KERNEL = 31
D_FF = 5632
N_EXPERTS = 8
D_EXPERT = 7168
RMS_EPS = 1e-6
LN_EPS = 1e-5
MASK_VALUE = -1e30
LOG2E = 1.4426950408889634

LANE = 128
SUBLANES = 8
BLK = 128
PAD = BLK - N_META
LP = PAD + N_META + SEQ
TM = 640
NT = LP // TM
NB = LP // BLK
HALO = 32
MOE_TM = 256
MOE_TILES = -(-(2 * LP + N_EXPERTS * (MOE_TM - 1)) // MOE_TM)
MOE_CAP = MOE_TILES * MOE_TM
MOE_TN_A = 1024
MOE_TN_B = 512
FFN_TF = 512
OUT_T = 512
VMEM_LIMIT = 56 * 1024 * 1024


def _cparams(sem):
    return pltpu.CompilerParams(dimension_semantics=sem, vmem_limit_bytes=VMEM_LIMIT)


def _rms(x, g):
    return x * lax.rsqrt(jnp.mean(x * x, axis=-1, keepdims=True) + RMS_EPS) * g


def _silu(x):
    return x * jax.nn.sigmoid(x)


def _prep_kernel(meta_ref, x_ref, g_ref, h_ref, hn_ref):
    i = pl.program_id(0)
    v = jnp.where(i == 0, meta_ref[...], x_ref[...])
    h_ref[...] = v
    hn_ref[...] = _rms(v, g_ref[...]).astype(BF16)


def _prep(meta_blk, x2d, g):
    return pl.pallas_call(
        _prep_kernel,
        grid=(NB,),
        in_specs=[
            pl.BlockSpec((BLK, D_MODEL), lambda i: (0, 0)),
            pl.BlockSpec((BLK, D_MODEL), lambda i: (jnp.maximum(i - 1, 0), 0)),
            pl.BlockSpec((1, D_MODEL), lambda i: (0, 0)),
        ],
        out_specs=[
            pl.BlockSpec((BLK, D_MODEL), lambda i: (i, 0)),
            pl.BlockSpec((BLK, D_MODEL), lambda i: (i, 0)),
        ],
        out_shape=[
            jax.ShapeDtypeStruct((LP, D_MODEL), F32),
            jax.ShapeDtypeStruct((LP, D_MODEL), BF16),
        ],
        compiler_params=_cparams(("arbitrary",)),
        name="prep",
    )(meta_blk, x2d, g)


def _qkv_kernel(x_ref, w_ref, o_ref):
    j = pl.program_id(1)
    acc = jnp.dot(x_ref[...], w_ref[...], preferred_element_type=F32)
    scale = jnp.where(j == 0, HEAD_DIM ** -0.5 * LOG2E, 1.0).astype(F32)
    o_ref[...] = (acc * scale).astype(BF16)


def _qkv(hn, w_qkv):
    return pl.pallas_call(
        _qkv_kernel,
        grid=(NT, 3),
        in_specs=[
            pl.BlockSpec((TM, D_MODEL), lambda i, j: (i, 0)),
            pl.BlockSpec((D_MODEL, ATTN_WIDTH), lambda i, j: (0, j)),
        ],
        out_specs=pl.BlockSpec((TM, ATTN_WIDTH), lambda i, j: (i, j)),
        out_shape=jax.ShapeDtypeStruct((LP, 3 * ATTN_WIDTH), BF16),
        compiler_params=_cparams(("arbitrary", "arbitrary")),
        name="qkv",
    )(hn, w_qkv)


def _glu_kernel(x_ref, wa_ref, wg_ref, o_ref):
    i = pl.program_id(0)
    x = x_ref[...]
    a = jnp.dot(x, wa_ref[...], preferred_element_type=F32)
    g = jnp.dot(x, wg_ref[...], preferred_element_type=F32)
    row = lax.broadcasted_iota(jnp.int32, a.shape, 0) + i * TM
    o_ref[...] = jnp.where(row >= PAD, a * jax.nn.sigmoid(g), 0.0)


def _glu(hn, w_a, w_g):
    tn = 512
    return pl.pallas_call(
        _glu_kernel,
        grid=(NT, CONV_WIDTH // tn),
        in_specs=[
            pl.BlockSpec((TM, D_MODEL), lambda i, j: (i, 0)),
            pl.BlockSpec((D_MODEL, tn), lambda i, j: (0, j)),
            pl.BlockSpec((D_MODEL, tn), lambda i, j: (0, j)),
        ],
        out_specs=pl.BlockSpec((TM, tn), lambda i, j: (i, j)),
        out_shape=jax.ShapeDtypeStruct((LP, CONV_WIDTH), F32),
        compiler_params=_cparams(("arbitrary", "arbitrary")),
        name="glu",
    )(hn, w_a, w_g)


def _split3(x):
    hi = x.astype(BF16)
    r1 = x - hi.astype(F32)
    mid = r1.astype(BF16)
    lo = (r1 - mid.astype(F32)).astype(BF16)
    return hi, mid, lo


def _decay_kernel(x_ref, w_ref, b_ref, kb_ref, carry_ref):
    i = pl.program_id(0)

    @pl.when(i == 0)
    def _():
        carry_ref[...] = jnp.zeros_like(carry_ref)

    f = jnp.dot(x_ref[...], w_ref[...], preferred_element_type=F32) + b_ref[...]
    log_f = jnp.minimum(f, 0.0) - jnp.log1p(jnp.exp(-jnp.abs(f)))
    row = lax.broadcasted_iota(jnp.int32, log_f.shape, 0) + i * TM
    log_f = jnp.where(row >= PAD, log_f, 0.0)
    r = lax.broadcasted_iota(jnp.int32, (TM, TM), 0)
    c = lax.broadcasted_iota(jnp.int32, (TM, TM), 1)
    tri = jnp.where(c <= r, 1.0, 0.0).astype(BF16)
    hi, mid, lo = _split3(log_f)
    cs = (jnp.dot(tri, hi, preferred_element_type=F32)
          + jnp.dot(tri, mid, preferred_element_type=F32)
          + jnp.dot(tri, lo, preferred_element_type=F32)) + carry_ref[0:1, :]
    carry_ref[...] = jnp.broadcast_to(cs[TM - 1:TM, :], carry_ref.shape)
    lane = lax.broadcasted_iota(jnp.int32, (TM, LANE), 1)
    for h in range(N_HEADS):
        bias = jnp.where(row[:, h:h + 1] >= PAD, -LOG2E * cs[:, h:h + 1], MASK_VALUE)
        hi, mid, lo = (v.astype(F32) for v in _split3(jnp.broadcast_to(bias, (TM, LANE))))
        pieces = jnp.where(lane == 0, hi, jnp.where(lane == 1, mid, jnp.where(lane == 2, lo, 0.0)))
        kb_ref[h] = pieces.astype(BF16)


def _decay(hn, w_f, b_f):
    return pl.pallas_call(
        _decay_kernel,
        grid=(NT,),
        in_specs=[
            pl.BlockSpec((TM, D_MODEL), lambda i: (i, 0)),
            pl.BlockSpec((D_MODEL, LANE), lambda i: (0, 0)),
            pl.BlockSpec((1, LANE), lambda i: (0, 0)),
        ],
        out_specs=pl.BlockSpec((N_HEADS, TM, LANE), lambda i: (0, i, 0)),
        out_shape=jax.ShapeDtypeStruct((N_HEADS, LP, LANE), BF16),
        scratch_shapes=[pltpu.VMEM((8, LANE), F32)],
        compiler_params=_cparams(("arbitrary",)),
        name="decay",
    )(hn, w_f, b_f)


HEADS_PER_STEP = 2
ATTN_STRIP = 16


def _attn_kernel(q_ref, k_ref, v_ref, kb_ref, g_ref, o_ref, *state_refs):
    i = pl.program_id(1)
    n_state = len(state_refs) // HEADS_PER_STEP
    heads = [state_refs[n_state * h:n_state * (h + 1)] for h in range(HEADS_PER_STEP)]
    for m_ref, l_ref, acc_ref, _, _ in heads:
        m_ref[...] = jnp.full_like(m_ref, -1e38)
        l_ref[...] = jnp.zeros_like(l_ref)
        acc_ref[...] = jnp.zeros_like(acc_ref)

    lane = lax.broadcasted_iota(jnp.int32, (TM, HEAD_DIM), 1)
    q_ones = jnp.where(lane < 3, 1.0, 0.0).astype(BF16)

    def block(j, diagonal):
        r0 = pl.multiple_of(j * TM, TM)
        for h, (_, _, _, s_ref, _) in enumerate(heads):
            hs = slice(h * HEAD_DIM, (h + 1) * HEAD_DIM)
            q_aug = jnp.concatenate([q_ref[:, hs], q_ones], axis=1)
            k_aug = jnp.concatenate([k_ref[pl.ds(r0, TM), hs], kb_ref[h, pl.ds(r0, TM), :]], axis=1)
            s_ref[...] = lax.dot_general(q_aug, k_aug, (((1,), (1,)), ((), ())),
                                         preferred_element_type=F32)
        for h, (m_ref, l_ref, acc_ref, s_ref, p_ref) in enumerate(heads):
            hs = slice(h * HEAD_DIM, (h + 1) * HEAD_DIM)
            for r in range(TM // ATTN_STRIP):
                rs = slice(r * ATTN_STRIP, (r + 1) * ATTN_STRIP)
                nc = min(TM, -(-((r + 1) * ATTN_STRIP) // LANE) * LANE) if diagonal else TM

                def logits():
                    s = s_ref[rs, :nc]
                    if diagonal:
                        row = lax.broadcasted_iota(jnp.int32, s.shape, 0) + r * ATTN_STRIP
                        col = lax.broadcasted_iota(jnp.int32, s.shape, 1)
                        s = jnp.where(col <= row, s, MASK_VALUE)
                    return s

                m_prev = m_ref[rs, :]
                m_new = jnp.maximum(m_prev, jnp.max(logits(), axis=-1, keepdims=True))
                p = jnp.exp2(logits() - jnp.concatenate([m_new] * (nc // LANE), axis=1))
                p_ref[rs, :nc] = p.astype(BF16)
                if nc < TM:
                    p_ref[rs, nc:] = jnp.zeros((ATTN_STRIP, TM - nc), BF16)
                alpha = jnp.exp2(m_prev - m_new)
                l_ref[rs, :] = alpha * l_ref[rs, :] + jnp.sum(p, axis=-1, keepdims=True)
                m_ref[rs, :] = m_new
                acc_ref[rs, :] = alpha * acc_ref[rs, :]
            acc_ref[...] += jnp.dot(p_ref[...], v_ref[pl.ds(r0, TM), hs],
                                    preferred_element_type=F32)

    def full_block(j, carry):
        block(j, False)
        return carry

    lax.fori_loop(0, i, full_block, 0)
    block(i, True)
    for h, (_, l_ref, acc_ref, _, _) in enumerate(heads):
        hs = slice(h * HEAD_DIM, (h + 1) * HEAD_DIM)
        o = acc_ref[...] / l_ref[...]
        o_ref[:, hs] = _rms(o, g_ref[h]).astype(BF16)


def _attention(qkv, key_bias, g_attn):
    hw = HEADS_PER_STEP * HEAD_DIM
    nhp = N_HEADS // HEADS_PER_STEP
    return pl.pallas_call(
        _attn_kernel,
        grid=(nhp, NT),
        in_specs=[
            pl.BlockSpec((TM, hw), lambda hp, i: (i, hp)),
            pl.BlockSpec((LP, hw), lambda hp, i: (0, nhp + hp)),
            pl.BlockSpec((LP, hw), lambda hp, i: (0, 2 * nhp + hp)),
            pl.BlockSpec((HEADS_PER_STEP, LP, LANE), lambda hp, i: (hp, 0, 0)),
            pl.BlockSpec((HEADS_PER_STEP, 1, HEAD_DIM), lambda hp, i: (hp, 0, 0)),
        ],
        out_specs=pl.BlockSpec((TM, hw), lambda hp, i: (i, hp)),
        out_shape=jax.ShapeDtypeStruct((LP, ATTN_WIDTH), BF16),
        scratch_shapes=[
            pltpu.VMEM((TM, HEAD_DIM), F32),
            pltpu.VMEM((TM, HEAD_DIM), F32),
            pltpu.VMEM((TM, HEAD_DIM), F32),
            pltpu.VMEM((TM, TM), F32),
            pltpu.VMEM((TM, TM), BF16),
        ] * HEADS_PER_STEP,
        compiler_params=_cparams(("arbitrary", "arbitrary")),
        name="attention",
    )(qkv, qkv, qkv, key_bias, g_attn)


CONV_RC = 64
CONV_CC = 512
CONV_ROWS = HALO + TM - SUBLANES


def _conv_kernel(x_ref, halo_ref, wdw_ref, bdw_ref, lng_ref, lnb_ref, wpw_ref, og_ref, o_ref,
                 xs_ref, cv_ref):
    i = pl.program_id(0)
    xs_ref[0, 0:HALO, :] = jnp.where(i == 0, 0.0, halo_ref[...])
    xs_ref[0, HALO:HALO + TM, :] = x_ref[...]
    for s in range(1, SUBLANES):
        xs_ref[s, 0:CONV_ROWS, :] = xs_ref[0, s:s + CONV_ROWS, :]
    off = HALO - (CONV_KERNEL - 1)
    for rc in range(TM // CONV_RC):
        for cc in range(CONV_WIDTH // CONV_CC):
            cs = slice(cc * CONV_CC, (cc + 1) * CONV_CC)
            acc = jnp.broadcast_to(bdw_ref[:, cs], (CONV_RC, CONV_CC))
            for j in range(CONV_KERNEL):
                s, a = (off + j) % SUBLANES, (off + j) // SUBLANES * SUBLANES
                r0 = rc * CONV_RC + a
                acc = acc + wdw_ref[j:j + 1, cs] * xs_ref[s, r0:r0 + CONV_RC, cs]
            cv_ref[rc * CONV_RC:(rc + 1) * CONV_RC, cs] = acc
    y = cv_ref[...]
    mu = jnp.mean(y, axis=-1, keepdims=True)
    yc = y - mu
    var = jnp.mean(yc * yc, axis=-1, keepdims=True)
    a = _silu(yc * lax.rsqrt(var + LN_EPS) * lng_ref[...] + lnb_ref[...])
    z = jnp.dot(a.astype(BF16), wpw_ref[...], preferred_element_type=F32)
    gd = CONV_WIDTH // CONV_GROUPS
    for g in range(CONV_GROUPS):
        gs = slice(g * gd, (g + 1) * gd)
        o_ref[:, gs] = _rms(z[:, gs], og_ref[:, gs]).astype(BF16)


def _conv(hc, w_dw, b_dw, ln_g, ln_b, w_pw, out_g):
    row = lambda i: (0, 0)
    return pl.pallas_call(
        _conv_kernel,
        grid=(NT,),
        in_specs=[
            pl.BlockSpec((TM, CONV_WIDTH), lambda i: (i, 0)),
            pl.BlockSpec((HALO, CONV_WIDTH), lambda i: (jnp.maximum(i * (TM // HALO) - 1, 0), 0)),
            pl.BlockSpec((HALO, CONV_WIDTH), row),
            pl.BlockSpec((1, CONV_WIDTH), row),
            pl.BlockSpec((1, CONV_WIDTH), row),
            pl.BlockSpec((1, CONV_WIDTH), row),
            pl.BlockSpec((CONV_WIDTH, CONV_WIDTH), row),
            pl.BlockSpec((1, CONV_WIDTH), row),
        ],
        out_specs=pl.BlockSpec((TM, CONV_WIDTH), lambda i: (i, 0)),
        out_shape=jax.ShapeDtypeStruct((LP, CONV_WIDTH), BF16),
        scratch_shapes=[
            pltpu.VMEM((SUBLANES, HALO + TM, CONV_WIDTH), F32),
            pltpu.VMEM((TM, CONV_WIDTH), F32),
        ],
        compiler_params=_cparams(("arbitrary",)),
        name="conv",
    )(hc, hc, w_dw, b_dw, ln_g, ln_b, w_pw, out_g)


def _outproj_kernel(a_ref, c_ref, w_ref, h_ref, g_ref, ho_ref, hn_ref):
    acc = jnp.dot(a_ref[...], w_ref[0:ATTN_WIDTH, :], preferred_element_type=F32)
    acc = acc + jnp.dot(c_ref[...], w_ref[ATTN_WIDTH:, :], preferred_element_type=F32)
    h_new = h_ref[...] + acc
    ho_ref[...] = h_new
    hn_ref[...] = _rms(h_new, g_ref[...]).astype(hn_ref.dtype)


def _outproj(attn, conv, w_out, h, g_next, hn_dtype):
    return pl.pallas_call(
        _outproj_kernel,
        grid=(NT,),
        in_specs=[
            pl.BlockSpec((TM, ATTN_WIDTH), lambda i: (i, 0)),
            pl.BlockSpec((TM, CONV_WIDTH), lambda i: (i, 0)),
            pl.BlockSpec((D_MODEL, D_MODEL), lambda i: (0, 0)),
            pl.BlockSpec((TM, D_MODEL), lambda i: (i, 0)),
            pl.BlockSpec((1, D_MODEL), lambda i: (0, 0)),
        ],
        out_specs=[
            pl.BlockSpec((TM, D_MODEL), lambda i: (i, 0)),
            pl.BlockSpec((TM, D_MODEL), lambda i: (i, 0)),
        ],
        out_shape=[
            jax.ShapeDtypeStruct((LP, D_MODEL), F32),
            jax.ShapeDtypeStruct((LP, D_MODEL), hn_dtype),
        ],
        compiler_params=_cparams(("arbitrary",)),
        name="outproj",
    )(attn, conv, w_out, h, g_next)


def _ffn_kernel(x_ref, wg_ref, wu_ref, wd_ref, h_ref, g_ref, ho_ref, hn_ref, acc_ref):
    j = pl.program_id(1)

    @pl.when(j == 0)
    def _():
        acc_ref[...] = jnp.zeros_like(acc_ref)

    x = x_ref[...]
    g = jnp.dot(x, wg_ref[...], preferred_element_type=F32)
    u = jnp.dot(x, wu_ref[...], preferred_element_type=F32)
    a = (_silu(g) * u).astype(BF16)
    acc_ref[...] += jnp.dot(a, wd_ref[...], preferred_element_type=F32)

    @pl.when(j == pl.num_programs(1) - 1)
    def _():
        h_new = h_ref[...] + acc_ref[...]
        ho_ref[...] = h_new
        hn_ref[...] = _rms(h_new, g_ref[...]).astype(BF16)


def _ffn(hn, w_gate, w_up, w_down, h, g_next):
    return pl.pallas_call(
        _ffn_kernel,
        grid=(NT, D_FF // FFN_TF),
        in_specs=[
            pl.BlockSpec((TM, D_MODEL), lambda i, j: (i, 0)),
            pl.BlockSpec((D_MODEL, FFN_TF), lambda i, j: (0, j)),
            pl.BlockSpec((D_MODEL, FFN_TF), lambda i, j: (0, j)),
            pl.BlockSpec((FFN_TF, D_MODEL), lambda i, j: (j, 0)),
            pl.BlockSpec((TM, D_MODEL), lambda i, j: (i, 0)),
            pl.BlockSpec((1, D_MODEL), lambda i, j: (0, 0)),
        ],
        out_specs=[
            pl.BlockSpec((TM, D_MODEL), lambda i, j: (i, 0)),
            pl.BlockSpec((TM, D_MODEL), lambda i, j: (i, 0)),
        ],
        out_shape=[
            jax.ShapeDtypeStruct((LP, D_MODEL), F32),
            jax.ShapeDtypeStruct((LP, D_MODEL), BF16),
        ],
        scratch_shapes=[pltpu.VMEM((TM, D_MODEL), F32)],
        compiler_params=_cparams(("arbitrary", "arbitrary")),
        name="ffn",
    )(hn, w_gate, w_up, w_down, h, g_next)


INFO_E1, INFO_E2, INFO_G1, INFO_G2, INFO_R1, INFO_R2 = range(6)


def _router_kernel(x_ref, w_ref, info_ref, cnt_ref, carry_ref):
    i = pl.program_id(0)

    @pl.when(i == 0)
    def _():
        carry_ref[...] = jnp.zeros_like(carry_ref)

    logits = jnp.dot(x_ref[...], w_ref[...], preferred_element_type=F32,
                     precision=lax.Precision.HIGHEST)
    lane = lax.broadcasted_iota(jnp.int32, logits.shape, 1)
    neg = jnp.float32(-3e38)
    l1 = jnp.where(lane < N_EXPERTS, logits, neg)
    m1 = jnp.max(l1, axis=-1, keepdims=True)
    i1 = jnp.min(jnp.where(l1 == m1, lane, LANE), axis=-1, keepdims=True)
    l2 = jnp.where(lane == i1, neg, l1)
    m2 = jnp.max(l2, axis=-1, keepdims=True)
    i2 = jnp.min(jnp.where(l2 == m2, lane, LANE), axis=-1, keepdims=True)
    ex = jnp.exp(m2 - m1)
    g1 = 1.0 / (1.0 + ex)
    g2 = ex / (1.0 + ex)
    sel = jnp.where(lane == i1, 1.0, jnp.where(lane == i2, 1.0, 0.0))
    r = lax.broadcasted_iota(jnp.int32, (TM, TM), 0)
    c = lax.broadcasted_iota(jnp.int32, (TM, TM), 1)
    tri = jnp.where(c < r, 1.0, 0.0).astype(BF16)
    carry = carry_ref[0:1, :]
    rank = jnp.dot(tri, sel.astype(BF16), preferred_element_type=F32) + carry
    r1 = jnp.sum(jnp.where(lane == i1, rank, 0.0), axis=-1, keepdims=True)
    r2 = jnp.sum(jnp.where(lane == i2, rank, 0.0), axis=-1, keepdims=True)
    info = jnp.zeros_like(logits)
    for k, v in ((INFO_E1, i1.astype(F32)), (INFO_E2, i2.astype(F32)), (INFO_G1, g1),
                 (INFO_G2, g2), (INFO_R1, r1), (INFO_R2, r2)):
        info = jnp.where(lane == k, v, info)
    info_ref[...] = info
    total = carry + jnp.sum(sel, axis=0, keepdims=True)
    carry_ref[...] = jnp.broadcast_to(total, carry_ref.shape)
    cnt_ref[...] = jnp.broadcast_to(total, cnt_ref.shape)


def _router(hn, w_r):
    return pl.pallas_call(
        _router_kernel,
        grid=(NT,),
        in_specs=[
            pl.BlockSpec((TM, D_MODEL), lambda i: (i, 0)),
            pl.BlockSpec((D_MODEL, LANE), lambda i: (0, 0)),
        ],
        out_specs=[
            pl.BlockSpec((TM, LANE), lambda i: (i, 0)),
            pl.BlockSpec((8, LANE), lambda i: (0, 0)),
        ],
        out_shape=[
            jax.ShapeDtypeStruct((LP, LANE), F32),
            jax.ShapeDtypeStruct((8, LANE), F32),
        ],
        scratch_shapes=[pltpu.VMEM((8, LANE), F32)],
        compiler_params=_cparams(("arbitrary",)),
        name="router",
    )(hn, w_r)


def _dispatch_kernel(pos_ref, x_ref, xs_in_ref, xs_ref, sem):
    del xs_in_ref
    i = pl.program_id(0)

    def issue(r, carry):
        base = (i * BLK + r) * 2
        for k in range(2):
            pltpu.make_async_copy(x_ref.at[pl.ds(r, 1)],
                                  xs_ref.at[pl.ds(pos_ref[base + k], 1)], sem).start()
        return carry

    lax.fori_loop(0, BLK, issue, 0)
    for _ in range(2):
        pltpu.make_async_copy(x_ref, xs_ref.at[pl.ds(0, BLK)], sem).wait()


def _dispatch(pos, hn):
    grid_spec = pltpu.PrefetchScalarGridSpec(
        num_scalar_prefetch=1,
        grid=(NB,),
        in_specs=[
            pl.BlockSpec((BLK, D_MODEL), lambda i, pos: (i, 0)),
            pl.BlockSpec(memory_space=pl.ANY),
        ],
        out_specs=pl.BlockSpec(memory_space=pl.ANY),
        scratch_shapes=[pltpu.SemaphoreType.DMA(())],
    )
    return pl.pallas_call(
        _dispatch_kernel,
        grid_spec=grid_spec,
        out_shape=jax.ShapeDtypeStruct((MOE_CAP, D_MODEL), F32),
        input_output_aliases={2: 0},
        compiler_params=_cparams(("arbitrary",)),
        name="dispatch",
    )(pos, hn, jnp.zeros((MOE_CAP, D_MODEL), F32))


def _new_expert(te_ref, i):
    return jnp.logical_or(i == 0, te_ref[i] != te_ref[jnp.maximum(i - 1, 0)])


def _moe_up_kernel(te_ref, nu_ref, x_ref, wg_ref, wu_ref, o_ref, wgb_ref, wub_ref):
    i = pl.program_id(1)

    @pl.when(_new_expert(te_ref, i))
    def _():
        wgb_ref[...] = wg_ref[0].astype(BF16)
        wub_ref[...] = wu_ref[0].astype(BF16)

    @pl.when(i < nu_ref[0])
    def _():
        x = x_ref[...].astype(BF16)
        g = jnp.dot(x, wgb_ref[...], preferred_element_type=F32)
        u = jnp.dot(x, wub_ref[...], preferred_element_type=F32)
        o_ref[...] = (_silu(g) * u).astype(BF16)

    @pl.when(i >= nu_ref[0])
    def _():
        o_ref[...] = jnp.zeros_like(o_ref)


def _moe_up(te, nu, xs, w_gate, w_up):
    grid_spec = pltpu.PrefetchScalarGridSpec(
        num_scalar_prefetch=2,
        grid=(D_EXPERT // MOE_TN_A, MOE_TILES),
        in_specs=[
            pl.BlockSpec((MOE_TM, D_MODEL), lambda j, i, te, nu: (i, 0)),
            pl.BlockSpec((1, D_MODEL, MOE_TN_A), lambda j, i, te, nu: (te[i], 0, j)),
            pl.BlockSpec((1, D_MODEL, MOE_TN_A), lambda j, i, te, nu: (te[i], 0, j)),
        ],
        out_specs=pl.BlockSpec((MOE_TM, MOE_TN_A), lambda j, i, te, nu: (i, j)),
        scratch_shapes=[
            pltpu.VMEM((D_MODEL, MOE_TN_A), BF16),
            pltpu.VMEM((D_MODEL, MOE_TN_A), BF16),
        ],
    )
    return pl.pallas_call(
        _moe_up_kernel,
        grid_spec=grid_spec,
        out_shape=jax.ShapeDtypeStruct((MOE_CAP, D_EXPERT), BF16),
        compiler_params=_cparams(("arbitrary", "arbitrary")),
        name="moe_up",
    )(te, nu, xs, w_gate, w_up)


def _moe_down_kernel(te_ref, nu_ref, h_ref, wd_ref, o_ref, wdb_ref):
    i = pl.program_id(1)

    @pl.when(_new_expert(te_ref, i))
    def _():
        wdb_ref[...] = wd_ref[0].astype(BF16)

    @pl.when(i < nu_ref[0])
    def _():
        o_ref[...] = jnp.dot(h_ref[...], wdb_ref[...], preferred_element_type=F32)

    @pl.when(i >= nu_ref[0])
    def _():
        o_ref[...] = jnp.zeros_like(o_ref)


def _moe_down(te, nu, h1, w_down):
    grid_spec = pltpu.PrefetchScalarGridSpec(
        num_scalar_prefetch=2,
        grid=(D_MODEL // MOE_TN_B, MOE_TILES),
        in_specs=[
            pl.BlockSpec((MOE_TM, D_EXPERT), lambda j, i, te, nu: (i, 0)),
            pl.BlockSpec((1, D_EXPERT, MOE_TN_B), lambda j, i, te, nu: (te[i], 0, j)),
        ],
        out_specs=pl.BlockSpec((MOE_TM, MOE_TN_B), lambda j, i, te, nu: (i, j)),
        scratch_shapes=[pltpu.VMEM((D_EXPERT, MOE_TN_B), BF16)],
    )
    return pl.pallas_call(
        _moe_down_kernel,
        grid_spec=grid_spec,
        out_shape=jax.ShapeDtypeStruct((MOE_CAP, D_MODEL), F32),
        compiler_params=_cparams(("arbitrary", "arbitrary")),
        name="moe_down",
    )(te, nu, h1, w_down)


def _combine_kernel(pos_ref, info_ref, g_ref, h_hbm, y_hbm, o_ref, hbuf, ybuf, sem_h, sem_y):
    i = pl.program_id(0)
    h_copy = pltpu.make_async_copy(h_hbm.at[pl.ds(BLK + i * OUT_T, OUT_T)], hbuf, sem_h)
    h_copy.start()

    def issue(r, carry):
        base = (i * OUT_T + r) * 2
        for k in range(2):
            pltpu.make_async_copy(y_hbm.at[pl.ds(pos_ref[base + k], 1)],
                                  ybuf.at[k, pl.ds(r, 1)], sem_y).start()
        return carry

    lax.fori_loop(0, OUT_T, issue, 0)
    h_copy.wait()
    for k in range(2):
        pltpu.make_async_copy(y_hbm.at[pl.ds(0, OUT_T)], ybuf.at[k], sem_y).wait()
    g1 = info_ref[:, INFO_G1:INFO_G1 + 1]
    g2 = info_ref[:, INFO_G2:INFO_G2 + 1]
    h_new = hbuf[...] + g1 * ybuf[0] + g2 * ybuf[1]
    o_ref[...] = _rms(h_new, g_ref[...])


def _combine(pos_x, info_x, g_final, h, y):
    grid_spec = pltpu.PrefetchScalarGridSpec(
        num_scalar_prefetch=1,
        grid=(SEQ // OUT_T,),
        in_specs=[
            pl.BlockSpec((OUT_T, LANE), lambda i, pos: (i, 0)),
            pl.BlockSpec((1, D_MODEL), lambda i, pos: (0, 0)),
            pl.BlockSpec(memory_space=pl.ANY),
            pl.BlockSpec(memory_space=pl.ANY),
        ],
        out_specs=pl.BlockSpec((OUT_T, D_MODEL), lambda i, pos: (i, 0)),
        scratch_shapes=[
            pltpu.VMEM((OUT_T, D_MODEL), F32),
            pltpu.VMEM((2, OUT_T, D_MODEL), F32),
            pltpu.SemaphoreType.DMA(()),
            pltpu.SemaphoreType.DMA(()),
        ],
    )
    return pl.pallas_call(
        _combine_kernel,
        grid_spec=grid_spec,
        out_shape=jax.ShapeDtypeStruct((SEQ, D_MODEL), F32),
        compiler_params=_cparams(("arbitrary",)),
        name="combine",
    )(pos_x, info_x, g_final, h, y)


def _mixer(h, hn, layer, p, g_next, hn_dtype):
    w_in = p["w_in"][layer]
    a0 = 3 * ATTN_WIDTH
    u0 = a0 + N_HEADS
    w_qkv = w_in[:, :a0].astype(BF16)
    w_f = jnp.pad(w_in[:, a0:u0], ((0, 0), (0, LANE - N_HEADS))).astype(BF16)
    w_a = w_in[:, u0:u0 + CONV_WIDTH].astype(BF16)
    w_g = w_in[:, u0 + CONV_WIDTH:].astype(BF16)
    b_f = jnp.pad(p["b_forget"][layer], (0, LANE - N_HEADS)).reshape(1, LANE)

    qkv = _qkv(hn, w_qkv)
    key_bias = _decay(hn, w_f, b_f)
    attn = _attention(qkv, key_bias, p["attn_out_g"][layer].reshape(N_HEADS, 1, HEAD_DIM))

    hc = _glu(hn, w_a, w_g)
    w_dw = jnp.pad(p["w_dw"][layer], ((0, HALO - CONV_KERNEL), (0, 0)))
    row = lambda v: v.reshape(1, -1)
    conv = _conv(hc, w_dw, row(p["b_dw"][layer]), row(p["conv_ln_g"][layer]),
                 row(p["conv_ln_b"][layer]), p["w_conv_out"][layer].astype(BF16),
                 row(p["conv_out_g"][layer]))
    return _outproj(attn, conv, p["w_out"][layer].astype(BF16), h, g_next, hn_dtype)


def _moe(h, hn, p, j, g_final):
    w_r = jnp.pad(p["moe_w_router"][j], ((0, 0), (0, LANE - N_EXPERTS)))
    info, cnt = _router(hn, w_r)
    e1 = info[:, INFO_E1].astype(jnp.int32)
    e2 = info[:, INFO_E2].astype(jnp.int32)
    r1 = info[:, INFO_R1].astype(jnp.int32)
    r2 = info[:, INFO_R2].astype(jnp.int32)
    counts = cnt[0, :N_EXPERTS].astype(jnp.int32)
    padded = (counts + MOE_TM - 1) // MOE_TM * MOE_TM
    gend = jnp.cumsum(padded)
    gstart = gend - padded
    pos = jnp.stack([gstart[e1] + r1, gstart[e2] + r2], axis=-1).reshape(-1)
    n_used = gend[N_EXPERTS - 1] // MOE_TM
    tiles = jnp.minimum(jnp.arange(MOE_TILES, dtype=jnp.int32), n_used - 1)
    te = jnp.sum((gend[None, :] <= (tiles * MOE_TM)[:, None]).astype(jnp.int32), axis=1)
    te = jnp.minimum(te, N_EXPERTS - 1)
    nu = n_used.reshape(1).astype(jnp.int32)

    xs = _dispatch(pos, hn)
    h1 = _moe_up(te, nu, xs, p["moe_w_gate"][j], p["moe_w_up"][j])
    y = _moe_down(te, nu, h1, p["moe_w_down"][j])
    return _combine(pos[2 * BLK:], info[BLK:], g_final, h, y)


def kernel(x, meta_tokens, mix_norm_g, ffn_norm_g, w_in, b_forget, w_dw, b_dw, conv_ln_g, conv_ln_b, w_conv_out, attn_out_g, conv_out_g, w_out, dense_w_gate, dense_w_up, dense_w_down, moe_w_router, moe_w_gate, moe_w_up, moe_w_down, final_norm_g):
    assert x.shape == (1, SEQ, D_MODEL) and meta_tokens.shape == (N_META, D_MODEL)
    p = dict(w_in=w_in, b_forget=b_forget, w_dw=w_dw, b_dw=b_dw, conv_ln_g=conv_ln_g,
             conv_ln_b=conv_ln_b, w_conv_out=w_conv_out, attn_out_g=attn_out_g,
             conv_out_g=conv_out_g, w_out=w_out, moe_w_router=moe_w_router,
             moe_w_gate=moe_w_gate, moe_w_up=moe_w_up, moe_w_down=moe_w_down)
    row = lambda v: v.reshape(1, D_MODEL)
    meta_blk = jnp.pad(meta_tokens.astype(F32), ((PAD, 0), (0, 0)))
    h, hn = _prep(meta_blk, x.reshape(SEQ, D_MODEL), row(mix_norm_g[0]))

    h, hn = _mixer(h, hn, 0, p, row(ffn_norm_g[0]), BF16)
    h, hn = _ffn(hn, dense_w_gate[0].astype(BF16), dense_w_up[0].astype(BF16),
                 dense_w_down[0].astype(BF16), h, row(mix_norm_g[1]))
    h, hn = _mixer(h, hn, 1, p, row(ffn_norm_g[1]), F32)
    out = _moe(h, hn, p, 0, row(final_norm_g))
    return out.reshape(1, SEQ, D_MODEL)
```

```python
import functools

import jax
import jax.numpy as jnp
from jax import lax
from jax.experimental import pallas as pl
from jax.experimental.pallas import tpu as pltpu

F32 = jnp.float32
BF16 = jnp.bfloat16

D_MODEL = 2048
SEQ = 8192
N_META = 16
N_HEADS = 8
HEAD_DIM = 128
ATTN_WIDTH = N_HEADS * HEAD_DIM
CONV_WIDTH = 1024
CONV_GROUPS = 8
CONV_KERNEL = 31
D_FF = 5632
N_EXPERTS = 8
D_EXPERT = 7168
RMS_EPS = 1e-6
LN_EPS = 1e-5
MASK_VALUE = -1e30
LOG2E = 1.4426950408889634

LANE = 128
SUBLANES = 8
BLK = 128
PAD = BLK - N_META
LP = PAD + N_META + SEQ
TM = 640
NT = LP // TM
NB = LP // BLK
HALO = 32
MOE_TM = 256
MOE_TILES = -(-(2 * LP + N_EXPERTS * (MOE_TM - 1)) // MOE_TM)
MOE_CAP = MOE_TILES * MOE_TM
MOE_TN_A = 1024
MOE_TN_B = 512
FFN_TF = 512
OUT_T = 512
VMEM_LIMIT = 56 * 1024 * 1024


def _cparams(sem):
    return pltpu.CompilerParams(dimension_semantics=sem, vmem_limit_bytes=VMEM_LIMIT)


def _rms(x, g):
    return x * lax.rsqrt(jnp.mean(x * x, axis=-1, keepdims=True) + RMS_EPS) * g


def _silu(x):
    return x * jax.nn.sigmoid(x)


def _prep_kernel(meta_ref, x_ref, g_ref, h_ref, hn_ref):
    i = pl.program_id(0)
    v = jnp.where(i == 0, meta_ref[...], x_ref[...])
    h_ref[...] = v
    hn_ref[...] = _rms(v, g_ref[...]).astype(BF16)


def _prep(meta_blk, x2d, g):
    return pl.pallas_call(
        _prep_kernel,
        grid=(NB,),
        in_specs=[
            pl.BlockSpec((BLK, D_MODEL), lambda i: (0, 0)),
            pl.BlockSpec((BLK, D_MODEL), lambda i: (jnp.maximum(i - 1, 0), 0)),
            pl.BlockSpec((1, D_MODEL), lambda i: (0, 0)),
        ],
        out_specs=[
            pl.BlockSpec((BLK, D_MODEL), lambda i: (i, 0)),
            pl.BlockSpec((BLK, D_MODEL), lambda i: (i, 0)),
        ],
        out_shape=[
            jax.ShapeDtypeStruct((LP, D_MODEL), F32),
            jax.ShapeDtypeStruct((LP, D_MODEL), BF16),
        ],
        compiler_params=_cparams(("arbitrary",)),
        name="prep",
    )(meta_blk, x2d, g)


def _qkv_kernel(x_ref, w_ref, o_ref):
    j = pl.program_id(1)
    acc = jnp.dot(x_ref[...], w_ref[...], preferred_element_type=F32)
    scale = jnp.where(j == 0, HEAD_DIM ** -0.5 * LOG2E, 1.0).astype(F32)
    o_ref[...] = (acc * scale).astype(BF16)


def _qkv(hn, w_qkv):
    return pl.pallas_call(
        _qkv_kernel,
        grid=(NT, 3),
        in_specs=[
            pl.BlockSpec((TM, D_MODEL), lambda i, j: (i, 0)),
            pl.BlockSpec((D_MODEL, ATTN_WIDTH), lambda i, j: (0, j)),
        ],
        out_specs=pl.BlockSpec((TM, ATTN_WIDTH), lambda i, j: (i, j)),
        out_shape=jax.ShapeDtypeStruct((LP, 3 * ATTN_WIDTH), BF16),
        compiler_params=_cparams(("arbitrary", "arbitrary")),
        name="qkv",
    )(hn, w_qkv)


def _glu_kernel(x_ref, wa_ref, wg_ref, o_ref):
    i = pl.program_id(0)
    x = x_ref[...]
    a = jnp.dot(x, wa_ref[...], preferred_element_type=F32)
    g = jnp.dot(x, wg_ref[...], preferred_element_type=F32)
    row = lax.broadcasted_iota(jnp.int32, a.shape, 0) + i * TM
    o_ref[...] = jnp.where(row >= PAD, a * jax.nn.sigmoid(g), 0.0)


def _glu(hn, w_a, w_g):
    tn = 512
    return pl.pallas_call(
        _glu_kernel,
        grid=(NT, CONV_WIDTH // tn),
        in_specs=[
            pl.BlockSpec((TM, D_MODEL), lambda i, j: (i, 0)),
            pl.BlockSpec((D_MODEL, tn), lambda i, j: (0, j)),
            pl.BlockSpec((D_MODEL, tn), lambda i, j: (0, j)),
        ],
        out_specs=pl.BlockSpec((TM, tn), lambda i, j: (i, j)),
        out_shape=jax.ShapeDtypeStruct((LP, CONV_WIDTH), F32),
        compiler_params=_cparams(("arbitrary", "arbitrary")),
        name="glu",
    )(hn, w_a, w_g)


def _split3(x):
    hi = x.astype(BF16)
    r1 = x - hi.astype(F32)
    mid = r1.astype(BF16)
    lo = (r1 - mid.astype(F32)).astype(BF16)
    return hi, mid, lo


def _decay_kernel(x_ref, w_ref, b_ref, kb_ref, carry_ref):
    i = pl.program_id(0)

    @pl.when(i == 0)
    def _():
        carry_ref[...] = jnp.zeros_like(carry_ref)

    f = jnp.dot(x_ref[...], w_ref[...], preferred_element_type=F32) + b_ref[...]
    log_f = jnp.minimum(f, 0.0) - jnp.log1p(jnp.exp(-jnp.abs(f)))
    row = lax.broadcasted_iota(jnp.int32, log_f.shape, 0) + i * TM
    log_f = jnp.where(row >= PAD, log_f, 0.0)
    r = lax.broadcasted_iota(jnp.int32, (TM, TM), 0)
    c = lax.broadcasted_iota(jnp.int32, (TM, TM), 1)
    tri = jnp.where(c <= r, 1.0, 0.0).astype(BF16)
    hi, mid, lo = _split3(log_f)
    cs = (jnp.dot(tri, hi, preferred_element_type=F32)
          + jnp.dot(tri, mid, preferred_element_type=F32)
          + jnp.dot(tri, lo, preferred_element_type=F32)) + carry_ref[0:1, :]
    carry_ref[...] = jnp.broadcast_to(cs[TM - 1:TM, :], carry_ref.shape)
    lane = lax.broadcasted_iota(jnp.int32, (TM, LANE), 1)
    for h in range(N_HEADS):
        bias = jnp.where(row[:, h:h + 1] >= PAD, -LOG2E * cs[:, h:h + 1], MASK_VALUE)
        hi, mid, lo = (v.astype(F32) for v in _split3(jnp.broadcast_to(bias, (TM, LANE))))
        pieces = jnp.where(lane == 0, hi, jnp.where(lane == 1, mid, jnp.where(lane == 2, lo, 0.0)))
        kb_ref[h] = pieces.astype(BF16)


def _decay(hn, w_f, b_f):
    return pl.pallas_call(
        _decay_kernel,
        grid=(NT,),
        in_specs=[
            pl.BlockSpec((TM, D_MODEL), lambda i: (i, 0)),
            pl.BlockSpec((D_MODEL, LANE), lambda i: (0, 0)),
            pl.BlockSpec((1, LANE), lambda i: (0, 0)),
        ],
        out_specs=pl.BlockSpec((N_HEADS, TM, LANE), lambda i: (0, i, 0)),
        out_shape=jax.ShapeDtypeStruct((N_HEADS, LP, LANE), BF16),
        scratch_shapes=[pltpu.VMEM((8, LANE), F32)],
        compiler_params=_cparams(("arbitrary",)),
        name="decay",
    )(hn, w_f, b_f)


HEADS_PER_STEP = 2
ATTN_STRIP = 16
ATTN_WIDE = 2 * TM


def _attn_kernel(q_ref, k_ref, v_ref, kb_ref, g_ref, o_ref, *state_refs):
    i = pl.program_id(1)
    n_state = len(state_refs) // HEADS_PER_STEP
    heads = [state_refs[n_state * h:n_state * (h + 1)] for h in range(HEADS_PER_STEP)]
    for m_ref, l_ref, acc_ref, _, _ in heads:
        m_ref[...] = jnp.full_like(m_ref, -1e38)
        l_ref[...] = jnp.zeros_like(l_ref)
        acc_ref[...] = jnp.zeros_like(acc_ref)

    lane = lax.broadcasted_iota(jnp.int32, (TM, HEAD_DIM), 1)
    q_ones = jnp.where(lane < 3, 1.0, 0.0).astype(BF16)

    def lanes(h):
        return slice(h * HEAD_DIM, (h + 1) * HEAD_DIM)

    def block(key0, width, diagonal):
        keys = pl.ds(pl.multiple_of(key0, TM), width)
        for h, (_, _, _, s_ref, _) in enumerate(heads):
            q_aug = jnp.concatenate([q_ref[:, lanes(h)], q_ones], axis=1)
            k_aug = jnp.concatenate([k_ref[keys, lanes(h)], kb_ref[h, keys, :]], axis=1)
            s_ref[:, :width] = lax.dot_general(q_aug, k_aug, (((1,), (1,)), ((), ())),
                                               preferred_element_type=F32)
        for h, (m_ref, l_ref, acc_ref, s_ref, p_ref) in enumerate(heads):
            for r in range(TM // ATTN_STRIP):
                rs = slice(r * ATTN_STRIP, (r + 1) * ATTN_STRIP)
                nc = min(width, -(-((r + 1) * ATTN_STRIP) // LANE) * LANE) if diagonal else width

                def logits():
                    s = s_ref[rs, :nc]
                    if diagonal:
                        row = lax.broadcasted_iota(jnp.int32, s.shape, 0) + r * ATTN_STRIP
                        col = lax.broadcasted_iota(jnp.int32, s.shape, 1)
                        s = jnp.where(col <= row, s, MASK_VALUE)
                    return s

                m_prev = m_ref[rs, :]
                m_new = jnp.maximum(m_prev, jnp.max(logits(), axis=-1, keepdims=True))
                p = jnp.exp2(logits() - jnp.concatenate([m_new] * (nc // LANE), axis=1))
                p_ref[rs, :nc] = p.astype(BF16)
                if nc < width:
                    p_ref[rs, nc:width] = jnp.zeros((ATTN_STRIP, width - nc), BF16)
                alpha = jnp.exp2(m_prev - m_new)
                l_ref[rs, :] = alpha * l_ref[rs, :] + jnp.sum(p, axis=-1, keepdims=True)
                m_ref[rs, :] = m_new
                acc_ref[rs, :] = alpha * acc_ref[rs, :]
            acc_ref[...] += jnp.dot(p_ref[:, :width], v_ref[keys, lanes(h)],
                                    preferred_element_type=F32)

    def wide_block(j, carry):
        block(j * ATTN_WIDE, ATTN_WIDE, False)
        return carry

    lax.fori_loop(0, i // 2, wide_block, 0)

    @pl.when(i % 2 == 1)
    def _():
        block((i - 1) * TM, TM, False)

    block(i * TM, TM, True)
    for h, (_, l_ref, acc_ref, _, _) in enumerate(heads):
        hs = slice(h * HEAD_DIM, (h + 1) * HEAD_DIM)
        o = acc_ref[...] / l_ref[...]
        o_ref[:, hs] = _rms(o, g_ref[h]).astype(BF16)


def _attention(qkv, key_bias, g_attn):
    hw = HEADS_PER_STEP * HEAD_DIM
    nhp = N_HEADS // HEADS_PER_STEP
    return pl.pallas_call(
        _attn_kernel,
        grid=(nhp, NT),
        in_specs=[
            pl.BlockSpec((TM, hw), lambda hp, i: (i, hp)),
            pl.BlockSpec((LP, hw), lambda hp, i: (0, nhp + hp)),
            pl.BlockSpec((LP, hw), lambda hp, i: (0, 2 * nhp + hp)),
            pl.BlockSpec((HEADS_PER_STEP, LP, LANE), lambda hp, i: (hp, 0, 0)),
            pl.BlockSpec((HEADS_PER_STEP, 1, HEAD_DIM), lambda hp, i: (hp, 0, 0)),
        ],
        out_specs=pl.BlockSpec((TM, hw), lambda hp, i: (i, hp)),
        out_shape=jax.ShapeDtypeStruct((LP, ATTN_WIDTH), BF16),
        scratch_shapes=[
            pltpu.VMEM((TM, HEAD_DIM), F32),
            pltpu.VMEM((TM, HEAD_DIM), F32),
            pltpu.VMEM((TM, HEAD_DIM), F32),
            pltpu.VMEM((TM, ATTN_WIDE), F32),
            pltpu.VMEM((TM, ATTN_WIDE), BF16),
        ] * HEADS_PER_STEP,
        compiler_params=_cparams(("arbitrary", "arbitrary")),
        name="attention",
    )(qkv, qkv, qkv, key_bias, g_attn)


CONV_RC = 64
CONV_CC = 512
CONV_ROWS = HALO + TM - SUBLANES


def _conv_kernel(x_ref, halo_ref, wdw_ref, bdw_ref, lng_ref, lnb_ref, wpw_ref, og_ref, o_ref,
                 xs_ref, cv_ref):
    i = pl.program_id(0)
    xs_ref[0, 0:HALO, :] = jnp.where(i == 0, 0.0, halo_ref[...])
    xs_ref[0, HALO:HALO + TM, :] = x_ref[...]
    for s in range(1, SUBLANES):
        xs_ref[s, 0:CONV_ROWS, :] = xs_ref[0, s:s + CONV_ROWS, :]
    off = HALO - (CONV_KERNEL - 1)
    for rc in range(TM // CONV_RC):
        for cc in range(CONV_WIDTH // CONV_CC):
            cs = slice(cc * CONV_CC, (cc + 1) * CONV_CC)
            acc = jnp.broadcast_to(bdw_ref[:, cs], (CONV_RC, CONV_CC))
            for j in range(CONV_KERNEL):
                s, a = (off + j) % SUBLANES, (off + j) // SUBLANES * SUBLANES
                r0 = rc * CONV_RC + a
                acc = acc + wdw_ref[j:j + 1, cs] * xs_ref[s, r0:r0 + CONV_RC, cs]
            cv_ref[rc * CONV_RC:(rc + 1) * CONV_RC, cs] = acc
    y = cv_ref[...]
    mu = jnp.mean(y, axis=-1, keepdims=True)
    yc = y - mu
    var = jnp.mean(yc * yc, axis=-1, keepdims=True)
    a = _silu(yc * lax.rsqrt(var + LN_EPS) * lng_ref[...] + lnb_ref[...])
    z = jnp.dot(a.astype(BF16), wpw_ref[...], preferred_element_type=F32)
    gd = CONV_WIDTH // CONV_GROUPS
    for g in range(CONV_GROUPS):
        gs = slice(g * gd, (g + 1) * gd)
        o_ref[:, gs] = _rms(z[:, gs], og_ref[:, gs]).astype(BF16)


def _conv(hc, w_dw, b_dw, ln_g, ln_b, w_pw, out_g):
    row = lambda i: (0, 0)
    return pl.pallas_call(
        _conv_kernel,
        grid=(NT,),
        in_specs=[
            pl.BlockSpec((TM, CONV_WIDTH), lambda i: (i, 0)),
            pl.BlockSpec((HALO, CONV_WIDTH), lambda i: (jnp.maximum(i * (TM // HALO) - 1, 0), 0)),
            pl.BlockSpec((HALO, CONV_WIDTH), row),
            pl.BlockSpec((1, CONV_WIDTH), row),
            pl.BlockSpec((1, CONV_WIDTH), row),
            pl.BlockSpec((1, CONV_WIDTH), row),
            pl.BlockSpec((CONV_WIDTH, CONV_WIDTH), row),
            pl.BlockSpec((1, CONV_WIDTH), row),
        ],
        out_specs=pl.BlockSpec((TM, CONV_WIDTH), lambda i: (i, 0)),
        out_shape=jax.ShapeDtypeStruct((LP, CONV_WIDTH), BF16),
        scratch_shapes=[
            pltpu.VMEM((SUBLANES, HALO + TM, CONV_WIDTH), F32),
            pltpu.VMEM((TM, CONV_WIDTH), F32),
        ],
        compiler_params=_cparams(("arbitrary",)),
        name="conv",
    )(hc, hc, w_dw, b_dw, ln_g, ln_b, w_pw, out_g)


def _outproj_kernel(a_ref, c_ref, w_ref, h_ref, g_ref, ho_ref, hn_ref):
    acc = jnp.dot(a_ref[...], w_ref[0:ATTN_WIDTH, :], preferred_element_type=F32)
    acc = acc + jnp.dot(c_ref[...], w_ref[ATTN_WIDTH:, :], preferred_element_type=F32)
    h_new = h_ref[...] + acc
    ho_ref[...] = h_new
    hn_ref[...] = _rms(h_new, g_ref[...]).astype(hn_ref.dtype)


def _outproj(attn, conv, w_out, h, g_next, hn_dtype):
    return pl.pallas_call(
        _outproj_kernel,
        grid=(NT,),
        in_specs=[
            pl.BlockSpec((TM, ATTN_WIDTH), lambda i: (i, 0)),
            pl.BlockSpec((TM, CONV_WIDTH), lambda i: (i, 0)),
            pl.BlockSpec((D_MODEL, D_MODEL), lambda i: (0, 0)),
            pl.BlockSpec((TM, D_MODEL), lambda i: (i, 0)),
            pl.BlockSpec((1, D_MODEL), lambda i: (0, 0)),
        ],
        out_specs=[
            pl.BlockSpec((TM, D_MODEL), lambda i: (i, 0)),
            pl.BlockSpec((TM, D_MODEL), lambda i: (i, 0)),
        ],
        out_shape=[
            jax.ShapeDtypeStruct((LP, D_MODEL), F32),
            jax.ShapeDtypeStruct((LP, D_MODEL), hn_dtype),
        ],
        compiler_params=_cparams(("arbitrary",)),
        name="outproj",
    )(attn, conv, w_out, h, g_next)


def _ffn_kernel(x_ref, wg_ref, wu_ref, wd_ref, h_ref, g_ref, ho_ref, hn_ref, acc_ref):
    j = pl.program_id(1)

    @pl.when(j == 0)
    def _():
        acc_ref[...] = jnp.zeros_like(acc_ref)

    x = x_ref[...]
    g = jnp.dot(x, wg_ref[...], preferred_element_type=F32)
    u = jnp.dot(x, wu_ref[...], preferred_element_type=F32)
    a = (_silu(g) * u).astype(BF16)
    acc_ref[...] += jnp.dot(a, wd_ref[...], preferred_element_type=F32)

    @pl.when(j == pl.num_programs(1) - 1)
    def _():
        h_new = h_ref[...] + acc_ref[...]
        ho_ref[...] = h_new
        hn_ref[...] = _rms(h_new, g_ref[...]).astype(BF16)


def _ffn(hn, w_gate, w_up, w_down, h, g_next):
    return pl.pallas_call(
        _ffn_kernel,
        grid=(NT, D_FF // FFN_TF),
        in_specs=[
            pl.BlockSpec((TM, D_MODEL), lambda i, j: (i, 0)),
            pl.BlockSpec((D_MODEL, FFN_TF), lambda i, j: (0, j)),
            pl.BlockSpec((D_MODEL, FFN_TF), lambda i, j: (0, j)),
            pl.BlockSpec((FFN_TF, D_MODEL), lambda i, j: (j, 0)),
            pl.BlockSpec((TM, D_MODEL), lambda i, j: (i, 0)),
            pl.BlockSpec((1, D_MODEL), lambda i, j: (0, 0)),
        ],
        out_specs=[
            pl.BlockSpec((TM, D_MODEL), lambda i, j: (i, 0)),
            pl.BlockSpec((TM, D_MODEL), lambda i, j: (i, 0)),
        ],
        out_shape=[
            jax.ShapeDtypeStruct((LP, D_MODEL), F32),
            jax.ShapeDtypeStruct((LP, D_MODEL), BF16),
        ],
        scratch_shapes=[pltpu.VMEM((TM, D_MODEL), F32)],
        compiler_params=_cparams(("arbitrary", "arbitrary")),
        name="ffn",
    )(hn, w_gate, w_up, w_down, h, g_next)


INFO_E1, INFO_E2, INFO_G1, INFO_G2, INFO_R1, INFO_R2 = range(6)


def _router_kernel(x_ref, w_ref, info_ref, cnt_ref, carry_ref):
    i = pl.program_id(0)

    @pl.when(i == 0)
    def _():
        carry_ref[...] = jnp.zeros_like(carry_ref)

    logits = jnp.dot(x_ref[...], w_ref[...], preferred_element_type=F32,
                     precision=lax.Precision.HIGHEST)
    lane = lax.broadcasted_iota(jnp.int32, logits.shape, 1)
    neg = jnp.float32(-3e38)
    l1 = jnp.where(lane < N_EXPERTS, logits, neg)
    m1 = jnp.max(l1, axis=-1, keepdims=True)
    i1 = jnp.min(jnp.where(l1 == m1, lane, LANE), axis=-1, keepdims=True)
    l2 = jnp.where(lane == i1, neg, l1)
    m2 = jnp.max(l2, axis=-1, keepdims=True)
    i2 = jnp.min(jnp.where(l2 == m2, lane, LANE), axis=-1, keepdims=True)
    ex = jnp.exp(m2 - m1)
    g1 = 1.0 / (1.0 + ex)
    g2 = ex / (1.0 + ex)
    sel = jnp.where(lane == i1, 1.0, jnp.where(lane == i2, 1.0, 0.0))
    r = lax.broadcasted_iota(jnp.int32, (TM, TM), 0)
    c = lax.broadcasted_iota(jnp.int32, (TM, TM), 1)
    tri = jnp.where(c < r, 1.0, 0.0).astype(BF16)
    carry = carry_ref[0:1, :]
    rank = jnp.dot(tri, sel.astype(BF16), preferred_element_type=F32) + carry
    r1 = jnp.sum(jnp.where(lane == i1, rank, 0.0), axis=-1, keepdims=True)
    r2 = jnp.sum(jnp.where(lane == i2, rank, 0.0), axis=-1, keepdims=True)
    info = jnp.zeros_like(logits)
    for k, v in ((INFO_E1, i1.astype(F32)), (INFO_E2, i2.astype(F32)), (INFO_G1, g1),
                 (INFO_G2, g2), (INFO_R1, r1), (INFO_R2, r2)):
        info = jnp.where(lane == k, v, info)
    info_ref[...] = info
    total = carry + jnp.sum(sel, axis=0, keepdims=True)
    carry_ref[...] = jnp.broadcast_to(total, carry_ref.shape)
    cnt_ref[...] = jnp.broadcast_to(total, cnt_ref.shape)


def _router(hn, w_r):
    return pl.pallas_call(
        _router_kernel,
        grid=(NT,),
        in_specs=[
            pl.BlockSpec((TM, D_MODEL), lambda i: (i, 0)),
            pl.BlockSpec((D_MODEL, LANE), lambda i: (0, 0)),
        ],
        out_specs=[
            pl.BlockSpec((TM, LANE), lambda i: (i, 0)),
            pl.BlockSpec((8, LANE), lambda i: (0, 0)),
        ],
        out_shape=[
            jax.ShapeDtypeStruct((LP, LANE), F32),
            jax.ShapeDtypeStruct((8, LANE), F32),
        ],
        scratch_shapes=[pltpu.VMEM((8, LANE), F32)],
        compiler_params=_cparams(("arbitrary",)),
        name="router",
    )(hn, w_r)


def _dispatch_kernel(pos_ref, x_ref, xs_in_ref, xs_ref, sem):
    del xs_in_ref
    i = pl.program_id(0)

    def issue(r, carry):
        base = (i * BLK + r) * 2
        for k in range(2):
            pltpu.make_async_copy(x_ref.at[pl.ds(r, 1)],
                                  xs_ref.at[pl.ds(pos_ref[base + k], 1)], sem).start(priority=k)
        return carry

    lax.fori_loop(0, BLK, issue, 0)
    for _ in range(2):
        pltpu.make_async_copy(x_ref, xs_ref.at[pl.ds(0, BLK)], sem).wait()


def _dispatch(pos, hn):
    grid_spec = pltpu.PrefetchScalarGridSpec(
        num_scalar_prefetch=1,
        grid=(NB,),
        in_specs=[
            pl.BlockSpec((BLK, D_MODEL), lambda i, pos: (i, 0)),
            pl.BlockSpec(memory_space=pl.ANY),
        ],
        out_specs=pl.BlockSpec(memory_space=pl.ANY),
        scratch_shapes=[pltpu.SemaphoreType.DMA(())],
    )
    return pl.pallas_call(
        _dispatch_kernel,
        grid_spec=grid_spec,
        out_shape=jax.ShapeDtypeStruct((MOE_CAP, D_MODEL), F32),
        input_output_aliases={2: 0},
        compiler_params=_cparams(("arbitrary",)),
        name="dispatch",
    )(pos, hn, jnp.zeros((MOE_CAP, D_MODEL), F32))


def _stream_expert_rows(j, e, gs_ref, gn_ref, src_hbm, dst_hbm, tn, ibuf, obuf, isem, osem,
                        load_weights, compute):
    g0 = gs_ref[e]
    n = gn_ref[e]
    cols = pl.ds(pl.multiple_of(j * tn, tn), tn)

    def in_copy(t, slot):
        return pltpu.make_async_copy(src_hbm.at[pl.ds((g0 + t) * MOE_TM, MOE_TM)], ibuf.at[slot],
                                     isem.at[slot])

    def out_copy(t, slot):
        return pltpu.make_async_copy(obuf.at[slot], dst_hbm.at[pl.ds((g0 + t) * MOE_TM, MOE_TM), cols],
                                     osem.at[slot])

    @pl.when(n > 0)
    def _():
        in_copy(0, 0).start()

    load_weights()

    def body(t, carry):
        slot = lax.rem(t, 2)
        in_copy(t, slot).wait()

        @pl.when(t + 1 < n)
        def _():
            in_copy(t + 1, 1 - slot).start()

        out = compute(ibuf[slot])

        @pl.when(t >= 2)
        def _():
            out_copy(t - 2, slot).wait()

        obuf[slot] = out
        out_copy(t, slot).start()
        return carry

    lax.fori_loop(0, n, body, 0)
    for back in (2, 1):
        @pl.when(n >= back)
        def _():
            out_copy(n - back, lax.rem(n - back, 2)).wait()

    @pl.when(e == N_EXPERTS - 1)
    def _():
        obuf[0] = jnp.zeros(obuf.shape[1:], obuf.dtype)

        def fill(t, carry):
            cp = pltpu.make_async_copy(obuf.at[0], dst_hbm.at[pl.ds(t * MOE_TM, MOE_TM), cols],
                                       osem.at[0])
            cp.start()
            cp.wait()
            return carry

        lax.fori_loop(g0 + n, MOE_TILES, fill, 0)


def _moe_up_kernel(gs_ref, gn_ref, wg_ref, wu_ref, xs_hbm, h1_hbm, wgb_ref, wub_ref,
                   xbuf, obuf, xsem, osem):
    def load_weights():
        wgb_ref[...] = wg_ref[0].astype(BF16)
        wub_ref[...] = wu_ref[0].astype(BF16)

    def compute(x):
        xb = x.astype(BF16)
        g = jnp.dot(xb, wgb_ref[...], preferred_element_type=F32)
        u = jnp.dot(xb, wub_ref[...], preferred_element_type=F32)
        return (_silu(g) * u).astype(BF16)

    _stream_expert_rows(pl.program_id(0), pl.program_id(1), gs_ref, gn_ref, xs_hbm, h1_hbm, MOE_TN_A,
                        xbuf, obuf, xsem, osem, load_weights, compute)


def _moe_up(gs, gn, xs, w_gate, w_up):
    w_spec = pl.BlockSpec((1, D_MODEL, MOE_TN_A), lambda j, e, gs, gn: (e, 0, j))
    grid_spec = pltpu.PrefetchScalarGridSpec(
        num_scalar_prefetch=2,
        grid=(D_EXPERT // MOE_TN_A, N_EXPERTS),
        in_specs=[w_spec, w_spec, pl.BlockSpec(memory_space=pl.ANY)],
        out_specs=pl.BlockSpec(memory_space=pl.ANY),
        scratch_shapes=[
            pltpu.VMEM((D_MODEL, MOE_TN_A), BF16),
            pltpu.VMEM((D_MODEL, MOE_TN_A), BF16),
            pltpu.VMEM((2, MOE_TM, D_MODEL), F32),
            pltpu.VMEM((2, MOE_TM, MOE_TN_A), BF16),
            pltpu.SemaphoreType.DMA((2,)),
            pltpu.SemaphoreType.DMA((2,)),
        ],
    )
    return pl.pallas_call(
        _moe_up_kernel,
        grid_spec=grid_spec,
        out_shape=jax.ShapeDtypeStruct((MOE_CAP, D_EXPERT), BF16),
        compiler_params=_cparams(("arbitrary", "arbitrary")),
        name="moe_up",
    )(gs, gn, w_gate, w_up, xs)


def _moe_down_kernel(gs_ref, gn_ref, wd_ref, h1_hbm, y_hbm, wdb_ref, hbuf, obuf, hsem, osem):
    def load_weights():
        wdb_ref[...] = wd_ref[0].astype(BF16)

    def compute(h):
        return jnp.dot(h, wdb_ref[...], preferred_element_type=F32)

    _stream_expert_rows(pl.program_id(0), pl.program_id(1), gs_ref, gn_ref, h1_hbm, y_hbm, MOE_TN_B,
                        hbuf, obuf, hsem, osem, load_weights, compute)


def _moe_down(gs, gn, h1, w_down):
    grid_spec = pltpu.PrefetchScalarGridSpec(
        num_scalar_prefetch=2,
        grid=(D_MODEL // MOE_TN_B, N_EXPERTS),
        in_specs=[
            pl.BlockSpec((1, D_EXPERT, MOE_TN_B), lambda j, e, gs, gn: (e, 0, j)),
            pl.BlockSpec(memory_space=pl.ANY),
        ],
        out_specs=pl.BlockSpec(memory_space=pl.ANY),
        scratch_shapes=[
            pltpu.VMEM((D_EXPERT, MOE_TN_B), BF16),
            pltpu.VMEM((2, MOE_TM, D_EXPERT), BF16),
            pltpu.VMEM((2, MOE_TM, MOE_TN_B), F32),
            pltpu.SemaphoreType.DMA((2,)),
            pltpu.SemaphoreType.DMA((2,)),
        ],
    )
    return pl.pallas_call(
        _moe_down_kernel,
        grid_spec=grid_spec,
        out_shape=jax.ShapeDtypeStruct((MOE_CAP, D_MODEL), F32),
        compiler_params=_cparams(("arbitrary", "arbitrary")),
        name="moe_down",
    )(gs, gn, w_down, h1)


def _combine_kernel(pos_ref, info_ref, g_ref, h_hbm, y_hbm, o_ref, hbuf, ybuf, sem_h, sem_y):
    i = pl.program_id(0)
    h_copy = pltpu.make_async_copy(h_hbm.at[pl.ds(BLK + i * OUT_T, OUT_T)], hbuf, sem_h)
    h_copy.start()

    def issue(r, carry):
        base = (i * OUT_T + r) * 2
        for k in range(2):
            pltpu.make_async_copy(y_hbm.at[pl.ds(pos_ref[base + k], 1)],
                                  ybuf.at[k, pl.ds(r, 1)], sem_y).start(priority=k)
        return carry

    lax.fori_loop(0, OUT_T, issue, 0)
    h_copy.wait()
    for k in range(2):
        pltpu.make_async_copy(y_hbm.at[pl.ds(0, OUT_T)], ybuf.at[k], sem_y).wait()
    g1 = info_ref[:, INFO_G1:INFO_G1 + 1]
    g2 = info_ref[:, INFO_G2:INFO_G2 + 1]
    h_new = hbuf[...] + g1 * ybuf[0] + g2 * ybuf[1]
    o_ref[...] = _rms(h_new, g_ref[...])


def _combine(pos_x, info_x, g_final, h, y):
    grid_spec = pltpu.PrefetchScalarGridSpec(
        num_scalar_prefetch=1,
        grid=(SEQ // OUT_T,),
        in_specs=[
            pl.BlockSpec((OUT_T, LANE), lambda i, pos: (i, 0)),
            pl.BlockSpec((1, D_MODEL), lambda i, pos: (0, 0)),
            pl.BlockSpec(memory_space=pl.ANY),
            pl.BlockSpec(memory_space=pl.ANY),
        ],
        out_specs=pl.BlockSpec((OUT_T, D_MODEL), lambda i, pos: (i, 0)),
        scratch_shapes=[
            pltpu.VMEM((OUT_T, D_MODEL), F32),
            pltpu.VMEM((2, OUT_T, D_MODEL), F32),
            pltpu.SemaphoreType.DMA(()),
            pltpu.SemaphoreType.DMA(()),
        ],
    )
    return pl.pallas_call(
        _combine_kernel,
        grid_spec=grid_spec,
        out_shape=jax.ShapeDtypeStruct((SEQ, D_MODEL), F32),
        compiler_params=_cparams(("arbitrary",)),
        name="combine",
    )(pos_x, info_x, g_final, h, y)


def _mixer(h, hn, layer, p, g_next, hn_dtype):
    w_in = p["w_in"][layer]
    a0 = 3 * ATTN_WIDTH
    u0 = a0 + N_HEADS
    w_qkv = w_in[:, :a0].astype(BF16)
    w_f = jnp.pad(w_in[:, a0:u0], ((0, 0), (0, LANE - N_HEADS))).astype(BF16)
    w_a = w_in[:, u0:u0 + CONV_WIDTH].astype(BF16)
    w_g = w_in[:, u0 + CONV_WIDTH:].astype(BF16)
    b_f = jnp.pad(p["b_forget"][layer], (0, LANE - N_HEADS)).reshape(1, LANE)

    qkv = _qkv(hn, w_qkv)
    key_bias = _decay(hn, w_f, b_f)
    attn = _attention(qkv, key_bias, p["attn_out_g"][layer].reshape(N_HEADS, 1, HEAD_DIM))

    hc = _glu(hn, w_a, w_g)
    w_dw = jnp.pad(p["w_dw"][layer], ((0, HALO - CONV_KERNEL), (0, 0)))
    row = lambda v: v.reshape(1, -1)
    conv = _conv(hc, w_dw, row(p["b_dw"][layer]), row(p["conv_ln_g"][layer]),
                 row(p["conv_ln_b"][layer]), p["w_conv_out"][layer].astype(BF16),
                 row(p["conv_out_g"][layer]))
    return _outproj(attn, conv, p["w_out"][layer].astype(BF16), h, g_next, hn_dtype)


def _moe(h, hn, p, j, g_final):
    w_r = jnp.pad(p["moe_w_router"][j], ((0, 0), (0, LANE - N_EXPERTS)))
    info, cnt = _router(hn, w_r)
    e1 = info[:, INFO_E1].astype(jnp.int32)
    e2 = info[:, INFO_E2].astype(jnp.int32)
    r1 = info[:, INFO_R1].astype(jnp.int32)
    r2 = info[:, INFO_R2].astype(jnp.int32)
    counts = cnt[0, :N_EXPERTS].astype(jnp.int32)
    padded = (counts + MOE_TM - 1) // MOE_TM * MOE_TM
    gend = jnp.cumsum(padded)
    gstart = gend - padded
    pos = jnp.stack([gstart[e1] + r1, gstart[e2] + r2], axis=-1).reshape(-1)
    gs = (gstart // MOE_TM).astype(jnp.int32)
    gn = (padded // MOE_TM).astype(jnp.int32)

    xs = _dispatch(pos, hn)
    h1 = _moe_up(gs, gn, xs, p["moe_w_gate"][j], p["moe_w_up"][j])
    y = _moe_down(gs, gn, h1, p["moe_w_down"][j])
    return _combine(pos[2 * BLK:], info[BLK:], g_final, h, y)


def kernel(x, meta_tokens, mix_norm_g, ffn_norm_g, w_in, b_forget, w_dw, b_dw, conv_ln_g, conv_ln_b, w_conv_out, attn_out_g, conv_out_g, w_out, dense_w_gate, dense_w_up, dense_w_down, moe_w_router, moe_w_gate, moe_w_up, moe_w_down, final_norm_g):
    assert x.shape == (1, SEQ, D_MODEL) and meta_tokens.shape == (N_META, D_MODEL)
    p = dict(w_in=w_in, b_forget=b_forget, w_dw=w_dw, b_dw=b_dw, conv_ln_g=conv_ln_g,
             conv_ln_b=conv_ln_b, w_conv_out=w_conv_out, attn_out_g=attn_out_g,
             conv_out_g=conv_out_g, w_out=w_out, moe_w_router=moe_w_router,
             moe_w_gate=moe_w_gate, moe_w_up=moe_w_up, moe_w_down=moe_w_down)
    row = lambda v: v.reshape(1, D_MODEL)
    meta_blk = jnp.pad(meta_tokens.astype(F32), ((PAD, 0), (0, 0)))
    h, hn = _prep(meta_blk, x.reshape(SEQ, D_MODEL), row(mix_norm_g[0]))

    h, hn = _mixer(h, hn, 0, p, row(ffn_norm_g[0]), BF16)
    h, hn = _ffn(hn, dense_w_gate[0].astype(BF16), dense_w_up[0].astype(BF16),
                 dense_w_down[0].astype(BF16), h, row(mix_norm_g[1]))
    h, hn = _mixer(h, hn, 1, p, row(ffn_norm_g[1]), F32)
    out = _moe(h, hn, p, 0, row(final_norm_g))
    return out.reshape(1, SEQ, D_MODEL)
```

```python
import functools

import jax
import jax.numpy as jnp
from jax import lax
from jax.experimental import pallas as pl
from jax.experimental.pallas import tpu as pltpu

F32 = jnp.float32
BF16 = jnp.bfloat16

D_MODEL = 2048
SEQ = 8192
N_META = 16
N_HEADS = 8
HEAD_DIM = 128
ATTN_WIDTH = N_HEADS * HEAD_DIM
CONV_WIDTH = 1024
CONV_GROUPS = 8
CONV_KERNEL = 31
D_FF = 5632
N_EXPERTS = 8
D_EXPERT = 7168
RMS_EPS = 1e-6
LN_EPS = 1e-5
MASK_VALUE = -1e30
LOG2E = 1.4426950408889634

LANE = 128
SUBLANES = 8
BLK = 128
PAD = BLK - N_META
LP = PAD + N_META + SEQ
TM = 640
NT = LP // TM
NB = LP // BLK
HALO = 32
MOE_TM = 256
MOE_TILES = -(-(2 * LP + N_EXPERTS * (MOE_TM - 1)) // MOE_TM)
MOE_CAP = MOE_TILES * MOE_TM
MOE_TN_A = 1024
MOE_TN_B = 512
FFN_TF = 512
OUT_T = 512
VMEM_LIMIT = 56 * 1024 * 1024
ROW_DMA_PRIORITY = 1


def _cparams(sem):
    return pltpu.CompilerParams(dimension_semantics=sem, vmem_limit_bytes=VMEM_LIMIT)


def _rms(x, g):
    return x * lax.rsqrt(jnp.mean(x * x, axis=-1, keepdims=True) + RMS_EPS) * g


def _silu(x):
    return x * jax.nn.sigmoid(x)


def _prep_kernel(meta_ref, x_ref, g_ref, h_ref, hn_ref):
    i = pl.program_id(0)
    v = jnp.where(i == 0, meta_ref[...], x_ref[...])
    h_ref[...] = v
    hn_ref[...] = _rms(v, g_ref[...]).astype(BF16)


def _prep(meta_blk, x2d, g):
    return pl.pallas_call(
        _prep_kernel,
        grid=(NB,),
        in_specs=[
            pl.BlockSpec((BLK, D_MODEL), lambda i: (0, 0)),
            pl.BlockSpec((BLK, D_MODEL), lambda i: (jnp.maximum(i - 1, 0), 0)),
            pl.BlockSpec((1, D_MODEL), lambda i: (0, 0)),
        ],
        out_specs=[
            pl.BlockSpec((BLK, D_MODEL), lambda i: (i, 0)),
            pl.BlockSpec((BLK, D_MODEL), lambda i: (i, 0)),
        ],
        out_shape=[
            jax.ShapeDtypeStruct((LP, D_MODEL), F32),
            jax.ShapeDtypeStruct((LP, D_MODEL), BF16),
        ],
        compiler_params=_cparams(("arbitrary",)),
        name="prep",
    )(meta_blk, x2d, g)


def _qkv_kernel(x_ref, w_ref, o_ref):
    j = pl.program_id(1)
    acc = jnp.dot(x_ref[...], w_ref[...], preferred_element_type=F32)
    scale = jnp.where(j == 0, HEAD_DIM ** -0.5 * LOG2E, 1.0).astype(F32)
    o_ref[...] = (acc * scale).astype(BF16)


def _qkv(hn, w_qkv):
    return pl.pallas_call(
        _qkv_kernel,
        grid=(NT, 3),
        in_specs=[
            pl.BlockSpec((TM, D_MODEL), lambda i, j: (i, 0)),
            pl.BlockSpec((D_MODEL, ATTN_WIDTH), lambda i, j: (0, j)),
        ],
        out_specs=pl.BlockSpec((TM, ATTN_WIDTH), lambda i, j: (i, j)),
        out_shape=jax.ShapeDtypeStruct((LP, 3 * ATTN_WIDTH), BF16),
        compiler_params=_cparams(("arbitrary", "arbitrary")),
        name="qkv",
    )(hn, w_qkv)


def _glu_kernel(x_ref, wa_ref, wg_ref, o_ref):
    i = pl.program_id(0)
    x = x_ref[...]
    a = jnp.dot(x, wa_ref[...], preferred_element_type=F32)
    g = jnp.dot(x, wg_ref[...], preferred_element_type=F32)
    row = lax.broadcasted_iota(jnp.int32, a.shape, 0) + i * TM
    o_ref[...] = jnp.where(row >= PAD, a * jax.nn.sigmoid(g), 0.0)


def _glu(hn, w_a, w_g):
    tn = 512
    return pl.pallas_call(
        _glu_kernel,
        grid=(NT, CONV_WIDTH // tn),
        in_specs=[
            pl.BlockSpec((TM, D_MODEL), lambda i, j: (i, 0)),
            pl.BlockSpec((D_MODEL, tn), lambda i, j: (0, j)),
            pl.BlockSpec((D_MODEL, tn), lambda i, j: (0, j)),
        ],
        out_specs=pl.BlockSpec((TM, tn), lambda i, j: (i, j)),
        out_shape=jax.ShapeDtypeStruct((LP, CONV_WIDTH), F32),
        compiler_params=_cparams(("arbitrary", "arbitrary")),
        name="glu",
    )(hn, w_a, w_g)


def _split3(x):
    hi = x.astype(BF16)
    r1 = x - hi.astype(F32)
    mid = r1.astype(BF16)
    lo = (r1 - mid.astype(F32)).astype(BF16)
    return hi, mid, lo


def _decay_kernel(x_ref, w_ref, b_ref, kb_ref, carry_ref):
    i = pl.program_id(0)

    @pl.when(i == 0)
    def _():
        carry_ref[...] = jnp.zeros_like(carry_ref)

    f = jnp.dot(x_ref[...], w_ref[...], preferred_element_type=F32) + b_ref[...]
    log_f = jnp.minimum(f, 0.0) - jnp.log1p(jnp.exp(-jnp.abs(f)))
    row = lax.broadcasted_iota(jnp.int32, log_f.shape, 0) + i * TM
    log_f = jnp.where(row >= PAD, log_f, 0.0)
    r = lax.broadcasted_iota(jnp.int32, (TM, TM), 0)
    c = lax.broadcasted_iota(jnp.int32, (TM, TM), 1)
    tri = jnp.where(c <= r, 1.0, 0.0).astype(BF16)
    hi, mid, lo = _split3(log_f)
    cs = (jnp.dot(tri, hi, preferred_element_type=F32)
          + jnp.dot(tri, mid, preferred_element_type=F32)
          + jnp.dot(tri, lo, preferred_element_type=F32)) + carry_ref[0:1, :]
    carry_ref[...] = jnp.broadcast_to(cs[TM - 1:TM, :], carry_ref.shape)
    lane = lax.broadcasted_iota(jnp.int32, (TM, LANE), 1)
    for h in range(N_HEADS):
        bias = jnp.where(row[:, h:h + 1] >= PAD, -LOG2E * cs[:, h:h + 1], MASK_VALUE)
        hi, mid, lo = (v.astype(F32) for v in _split3(jnp.broadcast_to(bias, (TM, LANE))))
        pieces = jnp.where(lane == 0, hi, jnp.where(lane == 1, mid, jnp.where(lane == 2, lo, 0.0)))
        kb_ref[h] = pieces.astype(BF16)


def _decay(hn, w_f, b_f):
    return pl.pallas_call(
        _decay_kernel,
        grid=(NT,),
        in_specs=[
            pl.BlockSpec((TM, D_MODEL), lambda i: (i, 0)),
            pl.BlockSpec((D_MODEL, LANE), lambda i: (0, 0)),
            pl.BlockSpec((1, LANE), lambda i: (0, 0)),
        ],
        out_specs=pl.BlockSpec((N_HEADS, TM, LANE), lambda i: (0, i, 0)),
        out_shape=jax.ShapeDtypeStruct((N_HEADS, LP, LANE), BF16),
        scratch_shapes=[pltpu.VMEM((8, LANE), F32)],
        compiler_params=_cparams(("arbitrary",)),
        name="decay",
    )(hn, w_f, b_f)


HEADS_PER_STEP = 2
ATTN_STRIP = 16
ATTN_WIDE = 2 * TM


def _attn_kernel(q_ref, k_ref, v_ref, kb_ref, g_ref, o_ref, *state_refs):
    i = pl.program_id(1)
    n_state = len(state_refs) // HEADS_PER_STEP
    heads = [state_refs[n_state * h:n_state * (h + 1)] for h in range(HEADS_PER_STEP)]
    for m_ref, l_ref, acc_ref, _, _ in heads:
        m_ref[...] = jnp.full_like(m_ref, -1e38)
        l_ref[...] = jnp.zeros_like(l_ref)
        acc_ref[...] = jnp.zeros_like(acc_ref)

    lane = lax.broadcasted_iota(jnp.int32, (TM, HEAD_DIM), 1)
    q_ones = jnp.where(lane < 3, 1.0, 0.0).astype(BF16)

    def lanes(h):
        return slice(h * HEAD_DIM, (h + 1) * HEAD_DIM)

    def block(key0, width, diagonal):
        keys = pl.ds(pl.multiple_of(key0, TM), width)
        for h, (_, _, _, s_ref, _) in enumerate(heads):
            q_aug = jnp.concatenate([q_ref[:, lanes(h)], q_ones], axis=1)
            k_aug = jnp.concatenate([k_ref[keys, lanes(h)], kb_ref[h, keys, :]], axis=1)
            s_ref[:, :width] = lax.dot_general(q_aug, k_aug, (((1,), (1,)), ((), ())),
                                               preferred_element_type=F32)
        for h, (m_ref, l_ref, acc_ref, s_ref, p_ref) in enumerate(heads):
            for r in range(TM // ATTN_STRIP):
                rs = slice(r * ATTN_STRIP, (r + 1) * ATTN_STRIP)
                nc = min(width, -(-((r + 1) * ATTN_STRIP) // LANE) * LANE) if diagonal else width

                def logits():
                    s = s_ref[rs, :nc]
                    if diagonal:
                        row = lax.broadcasted_iota(jnp.int32, s.shape, 0) + r * ATTN_STRIP
                        col = lax.broadcasted_iota(jnp.int32, s.shape, 1)
                        s = jnp.where(col <= row, s, MASK_VALUE)
                    return s

                m_prev = m_ref[rs, :]
                m_new = jnp.maximum(m_prev, jnp.max(logits(), axis=-1, keepdims=True))
                p = jnp.exp2(logits() - jnp.concatenate([m_new] * (nc // LANE), axis=1))
                p_ref[rs, :nc] = p.astype(BF16)
                if nc < width:
                    p_ref[rs, nc:width] = jnp.zeros((ATTN_STRIP, width - nc), BF16)
                alpha = jnp.exp2(m_prev - m_new)
                l_ref[rs, :] = alpha * l_ref[rs, :] + jnp.sum(p, axis=-1, keepdims=True)
                m_ref[rs, :] = m_new
                acc_ref[rs, :] = alpha * acc_ref[rs, :]
            acc_ref[...] += jnp.dot(p_ref[:, :width], v_ref[keys, lanes(h)],
                                    preferred_element_type=F32)

    def wide_block(j, carry):
        block(j * ATTN_WIDE, ATTN_WIDE, False)
        return carry

    lax.fori_loop(0, i // 2, wide_block, 0)

    @pl.when(i % 2 == 1)
    def _():
        block((i - 1) * TM, TM, False)

    block(i * TM, TM, True)
    for h, (_, l_ref, acc_ref, _, _) in enumerate(heads):
        hs = slice(h * HEAD_DIM, (h + 1) * HEAD_DIM)
        o = acc_ref[...] / l_ref[...]
        o_ref[:, hs] = _rms(o, g_ref[h]).astype(BF16)


def _attention(qkv, key_bias, g_attn):
    hw = HEADS_PER_STEP * HEAD_DIM
    nhp = N_HEADS // HEADS_PER_STEP
    return pl.pallas_call(
        _attn_kernel,
        grid=(nhp, NT),
        in_specs=[
            pl.BlockSpec((TM, hw), lambda hp, i: (i, hp)),
            pl.BlockSpec((LP, hw), lambda hp, i: (0, nhp + hp)),
            pl.BlockSpec((LP, hw), lambda hp, i: (0, 2 * nhp + hp)),
            pl.BlockSpec((HEADS_PER_STEP, LP, LANE), lambda hp, i: (hp, 0, 0)),
            pl.BlockSpec((HEADS_PER_STEP, 1, HEAD_DIM), lambda hp, i: (hp, 0, 0)),
        ],
        out_specs=pl.BlockSpec((TM, hw), lambda hp, i: (i, hp)),
        out_shape=jax.ShapeDtypeStruct((LP, ATTN_WIDTH), BF16),
        scratch_shapes=[
            pltpu.VMEM((TM, HEAD_DIM), F32),
            pltpu.VMEM((TM, HEAD_DIM), F32),
            pltpu.VMEM((TM, HEAD_DIM), F32),
            pltpu.VMEM((TM, ATTN_WIDE), F32),
            pltpu.VMEM((TM, ATTN_WIDE), BF16),
        ] * HEADS_PER_STEP,
        compiler_params=_cparams(("arbitrary", "arbitrary")),
        name="attention",
    )(qkv, qkv, qkv, key_bias, g_attn)


CONV_RC = 64
CONV_CC = 512
CONV_ROWS = HALO + TM - SUBLANES


def _conv_kernel(x_ref, halo_ref, wdw_ref, bdw_ref, lng_ref, lnb_ref, wpw_ref, og_ref, o_ref,
                 xs_ref, cv_ref):
    i = pl.program_id(0)
    xs_ref[0, 0:HALO, :] = jnp.where(i == 0, 0.0, halo_ref[...])
    xs_ref[0, HALO:HALO + TM, :] = x_ref[...]
    for s in range(1, SUBLANES):
        xs_ref[s, 0:CONV_ROWS, :] = xs_ref[0, s:s + CONV_ROWS, :]
    off = HALO - (CONV_KERNEL - 1)
    for rc in range(TM // CONV_RC):
        for cc in range(CONV_WIDTH // CONV_CC):
            cs = slice(cc * CONV_CC, (cc + 1) * CONV_CC)
            acc = jnp.broadcast_to(bdw_ref[:, cs], (CONV_RC, CONV_CC))
            for j in range(CONV_KERNEL):
                s, a = (off + j) % SUBLANES, (off + j) // SUBLANES * SUBLANES
                r0 = rc * CONV_RC + a
                acc = acc + wdw_ref[j:j + 1, cs] * xs_ref[s, r0:r0 + CONV_RC, cs]
            cv_ref[rc * CONV_RC:(rc + 1) * CONV_RC, cs] = acc
    y = cv_ref[...]
    mu = jnp.mean(y, axis=-1, keepdims=True)
    yc = y - mu
    var = jnp.mean(yc * yc, axis=-1, keepdims=True)
    a = _silu(yc * lax.rsqrt(var + LN_EPS) * lng_ref[...] + lnb_ref[...])
    z = jnp.dot(a.astype(BF16), wpw_ref[...], preferred_element_type=F32)
    gd = CONV_WIDTH // CONV_GROUPS
    for g in range(CONV_GROUPS):
        gs = slice(g * gd, (g + 1) * gd)
        o_ref[:, gs] = _rms(z[:, gs], og_ref[:, gs]).astype(BF16)


def _conv(hc, w_dw, b_dw, ln_g, ln_b, w_pw, out_g):
    row = lambda i: (0, 0)
    return pl.pallas_call(
        _conv_kernel,
        grid=(NT,),
        in_specs=[
            pl.BlockSpec((TM, CONV_WIDTH), lambda i: (i, 0)),
            pl.BlockSpec((HALO, CONV_WIDTH), lambda i: (jnp.maximum(i * (TM // HALO) - 1, 0), 0)),
            pl.BlockSpec((HALO, CONV_WIDTH), row),
            pl.BlockSpec((1, CONV_WIDTH), row),
            pl.BlockSpec((1, CONV_WIDTH), row),
            pl.BlockSpec((1, CONV_WIDTH), row),
            pl.BlockSpec((CONV_WIDTH, CONV_WIDTH), row),
            pl.BlockSpec((1, CONV_WIDTH), row),
        ],
        out_specs=pl.BlockSpec((TM, CONV_WIDTH), lambda i: (i, 0)),
        out_shape=jax.ShapeDtypeStruct((LP, CONV_WIDTH), BF16),
        scratch_shapes=[
            pltpu.VMEM((SUBLANES, HALO + TM, CONV_WIDTH), F32),
            pltpu.VMEM((TM, CONV_WIDTH), F32),
        ],
        compiler_params=_cparams(("arbitrary",)),
        name="conv",
    )(hc, hc, w_dw, b_dw, ln_g, ln_b, w_pw, out_g)


def _outproj_kernel(a_ref, c_ref, w_ref, h_ref, g_ref, ho_ref, hn_ref):
    acc = jnp.dot(a_ref[...], w_ref[0:ATTN_WIDTH, :], preferred_element_type=F32)
    acc = acc + jnp.dot(c_ref[...], w_ref[ATTN_WIDTH:, :], preferred_element_type=F32)
    h_new = h_ref[...] + acc
    ho_ref[...] = h_new
    hn_ref[...] = _rms(h_new, g_ref[...]).astype(hn_ref.dtype)


def _outproj(attn, conv, w_out, h, g_next, hn_dtype):
    return pl.pallas_call(
        _outproj_kernel,
        grid=(NT,),
        in_specs=[
            pl.BlockSpec((TM, ATTN_WIDTH), lambda i: (i, 0)),
            pl.BlockSpec((TM, CONV_WIDTH), lambda i: (i, 0)),
            pl.BlockSpec((D_MODEL, D_MODEL), lambda i: (0, 0)),
            pl.BlockSpec((TM, D_MODEL), lambda i: (i, 0)),
            pl.BlockSpec((1, D_MODEL), lambda i: (0, 0)),
        ],
        out_specs=[
            pl.BlockSpec((TM, D_MODEL), lambda i: (i, 0)),
            pl.BlockSpec((TM, D_MODEL), lambda i: (i, 0)),
        ],
        out_shape=[
            jax.ShapeDtypeStruct((LP, D_MODEL), F32),
            jax.ShapeDtypeStruct((LP, D_MODEL), hn_dtype),
        ],
        compiler_params=_cparams(("arbitrary",)),
        name="outproj",
    )(attn, conv, w_out, h, g_next)


def _ffn_kernel(x_ref, wg_ref, wu_ref, wd_ref, h_ref, g_ref, ho_ref, hn_ref, acc_ref):
    j = pl.program_id(1)

    @pl.when(j == 0)
    def _():
        acc_ref[...] = jnp.zeros_like(acc_ref)

    x = x_ref[...]
    g = jnp.dot(x, wg_ref[...], preferred_element_type=F32)
    u = jnp.dot(x, wu_ref[...], preferred_element_type=F32)
    a = (_silu(g) * u).astype(BF16)
    acc_ref[...] += jnp.dot(a, wd_ref[...], preferred_element_type=F32)

    @pl.when(j == pl.num_programs(1) - 1)
    def _():
        h_new = h_ref[...] + acc_ref[...]
        ho_ref[...] = h_new
        hn_ref[...] = _rms(h_new, g_ref[...]).astype(BF16)


def _ffn(hn, w_gate, w_up, w_down, h, g_next):
    return pl.pallas_call(
        _ffn_kernel,
        grid=(NT, D_FF // FFN_TF),
        in_specs=[
            pl.BlockSpec((TM, D_MODEL), lambda i, j: (i, 0)),
            pl.BlockSpec((D_MODEL, FFN_TF), lambda i, j: (0, j)),
            pl.BlockSpec((D_MODEL, FFN_TF), lambda i, j: (0, j)),
            pl.BlockSpec((FFN_TF, D_MODEL), lambda i, j: (j, 0)),
            pl.BlockSpec((TM, D_MODEL), lambda i, j: (i, 0)),
            pl.BlockSpec((1, D_MODEL), lambda i, j: (0, 0)),
        ],
        out_specs=[
            pl.BlockSpec((TM, D_MODEL), lambda i, j: (i, 0)),
            pl.BlockSpec((TM, D_MODEL), lambda i, j: (i, 0)),
        ],
        out_shape=[
            jax.ShapeDtypeStruct((LP, D_MODEL), F32),
            jax.ShapeDtypeStruct((LP, D_MODEL), BF16),
        ],
        scratch_shapes=[pltpu.VMEM((TM, D_MODEL), F32)],
        compiler_params=_cparams(("arbitrary", "arbitrary")),
        name="ffn",
    )(hn, w_gate, w_up, w_down, h, g_next)


INFO_E1, INFO_E2, INFO_G1, INFO_G2, INFO_R1, INFO_R2 = range(6)


def _router_kernel(x_ref, w_ref, info_ref, cnt_ref, carry_ref):
    i = pl.program_id(0)

    @pl.when(i == 0)
    def _():
        carry_ref[...] = jnp.zeros_like(carry_ref)

    logits = jnp.dot(x_ref[...], w_ref[...], preferred_element_type=F32,
                     precision=lax.Precision.HIGHEST)
    lane = lax.broadcasted_iota(jnp.int32, logits.shape, 1)
    neg = jnp.float32(-3e38)
    l1 = jnp.where(lane < N_EXPERTS, logits, neg)
    m1 = jnp.max(l1, axis=-1, keepdims=True)
    i1 = jnp.min(jnp.where(l1 == m1, lane, LANE), axis=-1, keepdims=True)
    l2 = jnp.where(lane == i1, neg, l1)
    m2 = jnp.max(l2, axis=-1, keepdims=True)
    i2 = jnp.min(jnp.where(l2 == m2, lane, LANE), axis=-1, keepdims=True)
    ex = jnp.exp(m2 - m1)
    g1 = 1.0 / (1.0 + ex)
    g2 = ex / (1.0 + ex)
    sel = jnp.where(lane == i1, 1.0, jnp.where(lane == i2, 1.0, 0.0))
    r = lax.broadcasted_iota(jnp.int32, (TM, TM), 0)
    c = lax.broadcasted_iota(jnp.int32, (TM, TM), 1)
    tri = jnp.where(c < r, 1.0, 0.0).astype(BF16)
    carry = carry_ref[0:1, :]
    rank = jnp.dot(tri, sel.astype(BF16), preferred_element_type=F32) + carry
    r1 = jnp.sum(jnp.where(lane == i1, rank, 0.0), axis=-1, keepdims=True)
    r2 = jnp.sum(jnp.where(lane == i2, rank, 0.0), axis=-1, keepdims=True)
    info = jnp.zeros_like(logits)
    for k, v in ((INFO_E1, i1.astype(F32)), (INFO_E2, i2.astype(F32)), (INFO_G1, g1),
                 (INFO_G2, g2), (INFO_R1, r1), (INFO_R2, r2)):
        info = jnp.where(lane == k, v, info)
    info_ref[...] = info
    total = carry + jnp.sum(sel, axis=0, keepdims=True)
    carry_ref[...] = jnp.broadcast_to(total, carry_ref.shape)
    cnt_ref[...] = jnp.broadcast_to(total, cnt_ref.shape)


def _router(hn, w_r):
    return pl.pallas_call(
        _router_kernel,
        grid=(NT,),
        in_specs=[
            pl.BlockSpec((TM, D_MODEL), lambda i: (i, 0)),
            pl.BlockSpec((D_MODEL, LANE), lambda i: (0, 0)),
        ],
        out_specs=[
            pl.BlockSpec((TM, LANE), lambda i: (i, 0)),
            pl.BlockSpec((8, LANE), lambda i: (0, 0)),
        ],
        out_shape=[
            jax.ShapeDtypeStruct((LP, LANE), F32),
            jax.ShapeDtypeStruct((8, LANE), F32),
        ],
        scratch_shapes=[pltpu.VMEM((8, LANE), F32)],
        compiler_params=_cparams(("arbitrary",)),
        name="router",
    )(hn, w_r)


def _dispatch_kernel(pos_ref, x_ref, xs_in_ref, xs_ref, sem):
    del xs_in_ref
    i = pl.program_id(0)

    def issue(r, carry):
        base = (i * BLK + r) * 2
        for k in range(2):
            pltpu.make_async_copy(x_ref.at[pl.ds(r, 1)],
                                  xs_ref.at[pl.ds(pos_ref[base + k], 1)], sem).start()
        return carry

    lax.fori_loop(0, BLK, issue, 0, unroll=4)
    for _ in range(2):
        pltpu.make_async_copy(x_ref, xs_ref.at[pl.ds(0, BLK)], sem).wait()


def _dispatch(pos, hn):
    grid_spec = pltpu.PrefetchScalarGridSpec(
        num_scalar_prefetch=1,
        grid=(NB,),
        in_specs=[
            pl.BlockSpec((BLK, D_MODEL), lambda i, pos: (i, 0)),
            pl.BlockSpec(memory_space=pl.ANY),
        ],
        out_specs=pl.BlockSpec(memory_space=pl.ANY),
        scratch_shapes=[pltpu.SemaphoreType.DMA(())],
    )
    return pl.pallas_call(
        _dispatch_kernel,
        grid_spec=grid_spec,
        out_shape=jax.ShapeDtypeStruct((MOE_CAP, D_MODEL), F32),
        input_output_aliases={2: 0},
        compiler_params=_cparams(("arbitrary",)),
        name="dispatch",
    )(pos, hn, jnp.zeros((MOE_CAP, D_MODEL), F32))


def _stream_expert_rows(j, e, gs_ref, gn_ref, src_hbm, dst_hbm, tn, ibuf, obuf, isem, osem,
                        load_weights, compute):
    g0 = gs_ref[e]
    n = gn_ref[e]
    cols = pl.ds(pl.multiple_of(j * tn, tn), tn)

    def in_copy(t, slot):
        return pltpu.make_async_copy(src_hbm.at[pl.ds((g0 + t) * MOE_TM, MOE_TM)], ibuf.at[slot],
                                     isem.at[slot])

    def out_copy(t, slot):
        return pltpu.make_async_copy(obuf.at[slot], dst_hbm.at[pl.ds((g0 + t) * MOE_TM, MOE_TM), cols],
                                     osem.at[slot])

    @pl.when(n > 0)
    def _():
        in_copy(0, 0).start(priority=ROW_DMA_PRIORITY)

    load_weights()

    def body(t, carry):
        slot = lax.rem(t, 2)
        in_copy(t, slot).wait()

        @pl.when(t + 1 < n)
        def _():
            in_copy(t + 1, 1 - slot).start(priority=ROW_DMA_PRIORITY)

        out = compute(ibuf[slot])

        @pl.when(t >= 2)
        def _():
            out_copy(t - 2, slot).wait()

        obuf[slot] = out
        out_copy(t, slot).start(priority=ROW_DMA_PRIORITY)
        return carry

    lax.fori_loop(0, n, body, 0)
    for back in (2, 1):
        @pl.when(n >= back)
        def _():
            out_copy(n - back, lax.rem(n - back, 2)).wait()

    @pl.when(e == N_EXPERTS - 1)
    def _():
        obuf[0] = jnp.zeros(obuf.shape[1:], obuf.dtype)

        def fill(t, carry):
            cp = pltpu.make_async_copy(obuf.at[0], dst_hbm.at[pl.ds(t * MOE_TM, MOE_TM), cols],
                                       osem.at[0])
            cp.start()
            cp.wait()
            return carry

        lax.fori_loop(g0 + n, MOE_TILES, fill, 0)


def _moe_up_kernel(gs_ref, gn_ref, wg_ref, wu_ref, xs_hbm, h1_hbm, wgb_ref, wub_ref,
                   xbuf, obuf, xsem, osem):
    def load_weights():
        wgb_ref[...] = wg_ref[0].astype(BF16)
        wub_ref[...] = wu_ref[0].astype(BF16)

    def compute(x):
        xb = x.astype(BF16)
        g = jnp.dot(xb, wgb_ref[...], preferred_element_type=F32)
        u = jnp.dot(xb, wub_ref[...], preferred_element_type=F32)
        return (_silu(g) * u).astype(BF16)

    _stream_expert_rows(pl.program_id(0), pl.program_id(1), gs_ref, gn_ref, xs_hbm, h1_hbm, MOE_TN_A,
                        xbuf, obuf, xsem, osem, load_weights, compute)


def _moe_up(gs, gn, xs, w_gate, w_up):
    w_spec = pl.BlockSpec((1, D_MODEL, MOE_TN_A), lambda j, e, gs, gn: (e, 0, j))
    grid_spec = pltpu.PrefetchScalarGridSpec(
        num_scalar_prefetch=2,
        grid=(D_EXPERT // MOE_TN_A, N_EXPERTS),
        in_specs=[w_spec, w_spec, pl.BlockSpec(memory_space=pl.ANY)],
        out_specs=pl.BlockSpec(memory_space=pl.ANY),
        scratch_shapes=[
            pltpu.VMEM((D_MODEL, MOE_TN_A), BF16),
            pltpu.VMEM((D_MODEL, MOE_TN_A), BF16),
            pltpu.VMEM((2, MOE_TM, D_MODEL), F32),
            pltpu.VMEM((2, MOE_TM, MOE_TN_A), BF16),
            pltpu.SemaphoreType.DMA((2,)),
            pltpu.SemaphoreType.DMA((2,)),
        ],
    )
    return pl.pallas_call(
        _moe_up_kernel,
        grid_spec=grid_spec,
        out_shape=jax.ShapeDtypeStruct((MOE_CAP, D_EXPERT), BF16),
        compiler_params=_cparams(("arbitrary", "arbitrary")),
        name="moe_up",
    )(gs, gn, w_gate, w_up, xs)


def _moe_down_kernel(gs_ref, gn_ref, wd_ref, h1_hbm, y_hbm, wdb_ref, hbuf, obuf, hsem, osem):
    def load_weights():
        wdb_ref[...] = wd_ref[0].astype(BF16)

    def compute(h):
        return jnp.dot(h, wdb_ref[...], preferred_element_type=F32)

    _stream_expert_rows(pl.program_id(0), pl.program_id(1), gs_ref, gn_ref, h1_hbm, y_hbm, MOE_TN_B,
                        hbuf, obuf, hsem, osem, load_weights, compute)


def _moe_down(gs, gn, h1, w_down):
    grid_spec = pltpu.PrefetchScalarGridSpec(
        num_scalar_prefetch=2,
        grid=(D_MODEL // MOE_TN_B, N_EXPERTS),
        in_specs=[
            pl.BlockSpec((1, D_EXPERT, MOE_TN_B), lambda j, e, gs, gn: (e, 0, j)),
            pl.BlockSpec(memory_space=pl.ANY),
        ],
        out_specs=pl.BlockSpec(memory_space=pl.ANY),
        scratch_shapes=[
            pltpu.VMEM((D_EXPERT, MOE_TN_B), BF16),
            pltpu.VMEM((2, MOE_TM, D_EXPERT), BF16),
            pltpu.VMEM((2, MOE_TM, MOE_TN_B), F32),
            pltpu.SemaphoreType.DMA((2,)),
            pltpu.SemaphoreType.DMA((2,)),
        ],
    )
    return pl.pallas_call(
        _moe_down_kernel,
        grid_spec=grid_spec,
        out_shape=jax.ShapeDtypeStruct((MOE_CAP, D_MODEL), F32),
        compiler_params=_cparams(("arbitrary", "arbitrary")),
        name="moe_down",
    )(gs, gn, w_down, h1)


def _combine_kernel(pos_ref, info_ref, g_ref, h_hbm, y_hbm, o_ref, hbuf, ybuf, sem_h, sem_y):
    i = pl.program_id(0)
    slot = lax.rem(i, 2)

    def h_copy(t, s):
        return pltpu.make_async_copy(h_hbm.at[pl.ds(BLK + t * OUT_T, OUT_T)], hbuf.at[s], sem_h.at[s])

    def fetch(t, s):
        h_copy(t, s).start()

        def issue(r, carry):
            base = (t * OUT_T + r) * 2
            for k in range(2):
                pltpu.make_async_copy(y_hbm.at[pl.ds(pos_ref[base + k], 1)],
                                      ybuf.at[s, k, pl.ds(r, 1)], sem_y.at[s]).start()
            return carry

        lax.fori_loop(0, OUT_T, issue, 0, unroll=4)

    @pl.when(i == 0)
    def _():
        fetch(0, 0)

    @pl.when(i + 1 < pl.num_programs(0))
    def _():
        fetch(i + 1, 1 - slot)

    h_copy(i, slot).wait()
    for k in range(2):
        pltpu.make_async_copy(y_hbm.at[pl.ds(0, OUT_T)], ybuf.at[slot, k], sem_y.at[slot]).wait()
    g1 = info_ref[:, INFO_G1:INFO_G1 + 1]
    g2 = info_ref[:, INFO_G2:INFO_G2 + 1]
    h_new = hbuf[slot] + g1 * ybuf[slot, 0] + g2 * ybuf[slot, 1]
    o_ref[...] = _rms(h_new, g_ref[...])


def _combine(pos_x, info_x, g_final, h, y):
    grid_spec = pltpu.PrefetchScalarGridSpec(
        num_scalar_prefetch=1,
        grid=(SEQ // OUT_T,),
        in_specs=[
            pl.BlockSpec((OUT_T, LANE), lambda i, pos: (i, 0)),
            pl.BlockSpec((1, D_MODEL), lambda i, pos: (0, 0)),
            pl.BlockSpec(memory_space=pl.ANY),
            pl.BlockSpec(memory_space=pl.ANY),
        ],
        out_specs=pl.BlockSpec((OUT_T, D_MODEL), lambda i, pos: (i, 0)),
        scratch_shapes=[
            pltpu.VMEM((2, OUT_T, D_MODEL), F32),
            pltpu.VMEM((2, 2, OUT_T, D_MODEL), F32),
            pltpu.SemaphoreType.DMA((2,)),
            pltpu.SemaphoreType.DMA((2,)),
        ],
    )
    return pl.pallas_call(
        _combine_kernel,
        grid_spec=grid_spec,
        out_shape=jax.ShapeDtypeStruct((SEQ, D_MODEL), F32),
        compiler_params=_cparams(("arbitrary",)),
        name="combine",
    )(pos_x, info_x, g_final, h, y)


def _mixer(h, hn, layer, p, g_next, hn_dtype):
    w_in = p["w_in"][layer]
    a0 = 3 * ATTN_WIDTH
    u0 = a0 + N_HEADS
    w_qkv = w_in[:, :a0].astype(BF16)
    w_f = jnp.pad(w_in[:, a0:u0], ((0, 0), (0, LANE - N_HEADS))).astype(BF16)
    w_a = w_in[:, u0:u0 + CONV_WIDTH].astype(BF16)
    w_g = w_in[:, u0 + CONV_WIDTH:].astype(BF16)
    b_f = jnp.pad(p["b_forget"][layer], (0, LANE - N_HEADS)).reshape(1, LANE)

    qkv = _qkv(hn, w_qkv)
    key_bias = _decay(hn, w_f, b_f)
    attn = _attention(qkv, key_bias, p["attn_out_g"][layer].reshape(N_HEADS, 1, HEAD_DIM))

    hc = _glu(hn, w_a, w_g)
    w_dw = jnp.pad(p["w_dw"][layer], ((0, HALO - CONV_KERNEL), (0, 0)))
    row = lambda v: v.reshape(1, -1)
    conv = _conv(hc, w_dw, row(p["b_dw"][layer]), row(p["conv_ln_g"][layer]),
                 row(p["conv_ln_b"][layer]), p["w_conv_out"][layer].astype(BF16),
                 row(p["conv_out_g"][layer]))
    return _outproj(attn, conv, p["w_out"][layer].astype(BF16), h, g_next, hn_dtype)


def _moe(h, hn, p, j, g_final):
    w_r = jnp.pad(p["moe_w_router"][j], ((0, 0), (0, LANE - N_EXPERTS)))
    info, cnt = _router(hn, w_r)
    e1 = info[:, INFO_E1].astype(jnp.int32)
    e2 = info[:, INFO_E2].astype(jnp.int32)
    r1 = info[:, INFO_R1].astype(jnp.int32)
    r2 = info[:, INFO_R2].astype(jnp.int32)
    counts = cnt[0, :N_EXPERTS].astype(jnp.int32)
    padded = (counts + MOE_TM - 1) // MOE_TM * MOE_TM
    gend = jnp.cumsum(padded)
    gstart = gend - padded
    pos = jnp.stack([gstart[e1] + r1, gstart[e2] + r2], axis=-1).reshape(-1)
    gs = (gstart // MOE_TM).astype(jnp.int32)
    gn = (padded // MOE_TM).astype(jnp.int32)

    xs = _dispatch(pos, hn)
    h1 = _moe_up(gs, gn, xs, p["moe_w_gate"][j], p["moe_w_up"][j])
    y = _moe_down(gs, gn, h1, p["moe_w_down"][j])
    return _combine(pos[2 * BLK:], info[BLK:], g_final, h, y)


def kernel(x, meta_tokens, mix_norm_g, ffn_norm_g, w_in, b_forget, w_dw, b_dw, conv_ln_g, conv_ln_b, w_conv_out, attn_out_g, conv_out_g, w_out, dense_w_gate, dense_w_up, dense_w_down, moe_w_router, moe_w_gate, moe_w_up, moe_w_down, final_norm_g):
    assert x.shape == (1, SEQ, D_MODEL) and meta_tokens.shape == (N_META, D_MODEL)
    p = dict(w_in=w_in, b_forget=b_forget, w_dw=w_dw, b_dw=b_dw, conv_ln_g=conv_ln_g,
             conv_ln_b=conv_ln_b, w_conv_out=w_conv_out, attn_out_g=attn_out_g,
             conv_out_g=conv_out_g, w_out=w_out, moe_w_router=moe_w_router,
             moe_w_gate=moe_w_gate, moe_w_up=moe_w_up, moe_w_down=moe_w_down)
    row = lambda v: v.reshape(1, D_MODEL)
    meta_blk = jnp.pad(meta_tokens.astype(F32), ((PAD, 0), (0, 0)))
    h, hn = _prep(meta_blk, x.reshape(SEQ, D_MODEL), row(mix_norm_g[0]))

    h, hn = _mixer(h, hn, 0, p, row(ffn_norm_g[0]), BF16)
    h, hn = _ffn(hn, dense_w_gate[0].astype(BF16), dense_w_up[0].astype(BF16),
                 dense_w_down[0].astype(BF16), h, row(mix_norm_g[1]))
    h, hn = _mixer(h, hn, 1, p, row(ffn_norm_g[1]), F32)
    out = _moe(h, hn, p, 0, row(final_norm_g))
    return out.reshape(1, SEQ, D_MODEL)
```

```python
import functools

import jax
import jax.numpy as jnp
from jax import lax
from jax.experimental import pallas as pl
from jax.experimental.pallas import tpu as pltpu

F32 = jnp.float32
BF16 = jnp.bfloat16

D_MODEL = 2048
SEQ = 8192
N_META = 16
N_HEADS = 8
HEAD_DIM = 128
ATTN_WIDTH = N_HEADS * HEAD_DIM
CONV_WIDTH = 1024
CONV_GROUPS = 8
CONV_KERNEL = 31
D_FF = 5632
N_EXPERTS = 8
D_EXPERT = 7168
RMS_EPS = 1e-6
LN_EPS = 1e-5
MASK_VALUE = -1e30
LOG2E = 1.4426950408889634

LANE = 128
SUBLANES = 8
BLK = 128
PAD = BLK - N_META
LP = PAD + N_META + SEQ
TM = 640
NT = LP // TM
NB = LP // BLK
HALO = 32
MOE_TM = 512
MOE_TILES = -(-(2 * LP + N_EXPERTS * (MOE_TM - 1)) // MOE_TM)
MOE_CAP = MOE_TILES * MOE_TM
MOE_TN_A = 1024
MOE_TN_B = 512
FFN_TF = 512
OUT_T = 512
VMEM_LIMIT = 56 * 1024 * 1024
ROW_DMA_PRIORITY = 1


def _cparams(sem):
    return pltpu.CompilerParams(dimension_semantics=sem, vmem_limit_bytes=VMEM_LIMIT)


def _rms(x, g):
    return x * lax.rsqrt(jnp.mean(x * x, axis=-1, keepdims=True) + RMS_EPS) * g


def _silu(x):
    return x * jax.nn.sigmoid(x)


def _prep_kernel(meta_ref, x_ref, g_ref, h_ref, hn_ref):
    i = pl.program_id(0)
    v = jnp.where(i == 0, meta_ref[...], x_ref[...])
    h_ref[...] = v
    hn_ref[...] = _rms(v, g_ref[...]).astype(BF16)


def _prep(meta_blk, x2d, g):
    return pl.pallas_call(
        _prep_kernel,
        grid=(NB,),
        in_specs=[
            pl.BlockSpec((BLK, D_MODEL), lambda i: (0, 0)),
            pl.BlockSpec((BLK, D_MODEL), lambda i: (jnp.maximum(i - 1, 0), 0)),
            pl.BlockSpec((1, D_MODEL), lambda i: (0, 0)),
        ],
        out_specs=[
            pl.BlockSpec((BLK, D_MODEL), lambda i: (i, 0)),
            pl.BlockSpec((BLK, D_MODEL), lambda i: (i, 0)),
        ],
        out_shape=[
            jax.ShapeDtypeStruct((LP, D_MODEL), F32),
            jax.ShapeDtypeStruct((LP, D_MODEL), BF16),
        ],
        compiler_params=_cparams(("arbitrary",)),
        name="prep",
    )(meta_blk, x2d, g)


def _qkv_kernel(x_ref, w_ref, o_ref):
    j = pl.program_id(1)
    acc = jnp.dot(x_ref[...], w_ref[...], preferred_element_type=F32)
    scale = jnp.where(j == 0, HEAD_DIM ** -0.5 * LOG2E, 1.0).astype(F32)
    o_ref[...] = (acc * scale).astype(BF16)


def _qkv(hn, w_qkv):
    return pl.pallas_call(
        _qkv_kernel,
        grid=(NT, 3),
        in_specs=[
            pl.BlockSpec((TM, D_MODEL), lambda i, j: (i, 0)),
            pl.BlockSpec((D_MODEL, ATTN_WIDTH), lambda i, j: (0, j)),
        ],
        out_specs=pl.BlockSpec((TM, ATTN_WIDTH), lambda i, j: (i, j)),
        out_shape=jax.ShapeDtypeStruct((LP, 3 * ATTN_WIDTH), BF16),
        compiler_params=_cparams(("arbitrary", "arbitrary")),
        name="qkv",
    )(hn, w_qkv)


def _glu_kernel(x_ref, wa_ref, wg_ref, o_ref):
    i = pl.program_id(0)
    x = x_ref[...]
    a = jnp.dot(x, wa_ref[...], preferred_element_type=F32)
    g = jnp.dot(x, wg_ref[...], preferred_element_type=F32)
    row = lax.broadcasted_iota(jnp.int32, a.shape, 0) + i * TM
    o_ref[...] = jnp.where(row >= PAD, a * jax.nn.sigmoid(g), 0.0)


def _glu(hn, w_a, w_g):
    tn = 512
    return pl.pallas_call(
        _glu_kernel,
        grid=(NT, CONV_WIDTH // tn),
        in_specs=[
            pl.BlockSpec((TM, D_MODEL), lambda i, j: (i, 0)),
            pl.BlockSpec((D_MODEL, tn), lambda i, j: (0, j)),
            pl.BlockSpec((D_MODEL, tn), lambda i, j: (0, j)),
        ],
        out_specs=pl.BlockSpec((TM, tn), lambda i, j: (i, j)),
        out_shape=jax.ShapeDtypeStruct((LP, CONV_WIDTH), F32),
        compiler_params=_cparams(("arbitrary", "arbitrary")),
        name="glu",
    )(hn, w_a, w_g)


def _split3(x):
    hi = x.astype(BF16)
    r1 = x - hi.astype(F32)
    mid = r1.astype(BF16)
    lo = (r1 - mid.astype(F32)).astype(BF16)
    return hi, mid, lo


def _decay_kernel(x_ref, w_ref, b_ref, kb_ref, carry_ref):
    i = pl.program_id(0)

    @pl.when(i == 0)
    def _():
        carry_ref[...] = jnp.zeros_like(carry_ref)

    f = jnp.dot(x_ref[...], w_ref[...], preferred_element_type=F32) + b_ref[...]
    log_f = jnp.minimum(f, 0.0) - jnp.log1p(jnp.exp(-jnp.abs(f)))
    row = lax.broadcasted_iota(jnp.int32, log_f.shape, 0) + i * TM
    log_f = jnp.where(row >= PAD, log_f, 0.0)
    r = lax.broadcasted_iota(jnp.int32, (TM, TM), 0)
    c = lax.broadcasted_iota(jnp.int32, (TM, TM), 1)
    tri = jnp.where(c <= r, 1.0, 0.0).astype(BF16)
    hi, mid, lo = _split3(log_f)
    cs = (jnp.dot(tri, hi, preferred_element_type=F32)
          + jnp.dot(tri, mid, preferred_element_type=F32)
          + jnp.dot(tri, lo, preferred_element_type=F32)) + carry_ref[0:1, :]
    carry_ref[...] = jnp.broadcast_to(cs[TM - 1:TM, :], carry_ref.shape)
    lane = lax.broadcasted_iota(jnp.int32, (TM, LANE), 1)
    for h in range(N_HEADS):
        bias = jnp.where(row[:, h:h + 1] >= PAD, -LOG2E * cs[:, h:h + 1], MASK_VALUE)
        hi, mid, lo = (v.astype(F32) for v in _split3(jnp.broadcast_to(bias, (TM, LANE))))
        pieces = jnp.where(lane == 0, hi, jnp.where(lane == 1, mid, jnp.where(lane == 2, lo, 0.0)))
        kb_ref[h] = pieces.astype(BF16)


def _decay(hn, w_f, b_f):
    return pl.pallas_call(
        _decay_kernel,
        grid=(NT,),
        in_specs=[
            pl.BlockSpec((TM, D_MODEL), lambda i: (i, 0)),
            pl.BlockSpec((D_MODEL, LANE), lambda i: (0, 0)),
            pl.BlockSpec((1, LANE), lambda i: (0, 0)),
        ],
        out_specs=pl.BlockSpec((N_HEADS, TM, LANE), lambda i: (0, i, 0)),
        out_shape=jax.ShapeDtypeStruct((N_HEADS, LP, LANE), BF16),
        scratch_shapes=[pltpu.VMEM((8, LANE), F32)],
        compiler_params=_cparams(("arbitrary",)),
        name="decay",
    )(hn, w_f, b_f)


HEADS_PER_STEP = 2
ATTN_STRIP = 16
ATTN_WIDE = 2 * TM


def _attn_kernel(q_ref, k_ref, v_ref, kb_ref, g_ref, o_ref, *state_refs):
    i = pl.program_id(1)
    n_state = len(state_refs) // HEADS_PER_STEP
    heads = [state_refs[n_state * h:n_state * (h + 1)] for h in range(HEADS_PER_STEP)]
    for m_ref, l_ref, acc_ref, _, _ in heads:
        m_ref[...] = jnp.full_like(m_ref, -1e38)
        l_ref[...] = jnp.zeros_like(l_ref)
        acc_ref[...] = jnp.zeros_like(acc_ref)

    lane = lax.broadcasted_iota(jnp.int32, (TM, HEAD_DIM), 1)
    q_ones = jnp.where(lane < 3, 1.0, 0.0).astype(BF16)

    def lanes(h):
        return slice(h * HEAD_DIM, (h + 1) * HEAD_DIM)

    def block(key0, width, diagonal):
        keys = pl.ds(pl.multiple_of(key0, TM), width)
        for h, (_, _, _, s_ref, _) in enumerate(heads):
            q_aug = jnp.concatenate([q_ref[:, lanes(h)], q_ones], axis=1)
            k_aug = jnp.concatenate([k_ref[keys, lanes(h)], kb_ref[h, keys, :]], axis=1)
            s_ref[:, :width] = lax.dot_general(q_aug, k_aug, (((1,), (1,)), ((), ())),
                                               preferred_element_type=F32)
        for h, (m_ref, l_ref, acc_ref, s_ref, p_ref) in enumerate(heads):
            for r in range(TM // ATTN_STRIP):
                rs = slice(r * ATTN_STRIP, (r + 1) * ATTN_STRIP)
                nc = min(width, -(-((r + 1) * ATTN_STRIP) // LANE) * LANE) if diagonal else width

                def logits():
                    s = s_ref[rs, :nc]
                    if diagonal:
                        row = lax.broadcasted_iota(jnp.int32, s.shape, 0) + r * ATTN_STRIP
                        col = lax.broadcasted_iota(jnp.int32, s.shape, 1)
                        s = jnp.where(col <= row, s, MASK_VALUE)
                    return s

                m_prev = m_ref[rs, :]
                m_new = jnp.maximum(m_prev, jnp.max(logits(), axis=-1, keepdims=True))
                p = jnp.exp2(logits() - jnp.concatenate([m_new] * (nc // LANE), axis=1))
                p_ref[rs, :nc] = p.astype(BF16)
                if nc < width:
                    p_ref[rs, nc:width] = jnp.zeros((ATTN_STRIP, width - nc), BF16)
                alpha = jnp.exp2(m_prev - m_new)
                l_ref[rs, :] = alpha * l_ref[rs, :] + jnp.sum(p, axis=-1, keepdims=True)
                m_ref[rs, :] = m_new
                acc_ref[rs, :] = alpha * acc_ref[rs, :]
            acc_ref[...] += jnp.dot(p_ref[:, :width], v_ref[keys, lanes(h)],
                                    preferred_element_type=F32)

    def wide_block(j, carry):
        block(j * ATTN_WIDE, ATTN_WIDE, False)
        return carry

    lax.fori_loop(0, i // 2, wide_block, 0)

    @pl.when(i % 2 == 1)
    def _():
        block((i - 1) * TM, TM, False)

    block(i * TM, TM, True)
    for h, (_, l_ref, acc_ref, _, _) in enumerate(heads):
        hs = slice(h * HEAD_DIM, (h + 1) * HEAD_DIM)
        o = acc_ref[...] / l_ref[...]
        o_ref[:, hs] = _rms(o, g_ref[h]).astype(BF16)


def _attention(qkv, key_bias, g_attn):
    hw = HEADS_PER_STEP * HEAD_DIM
    nhp = N_HEADS // HEADS_PER_STEP
    return pl.pallas_call(
        _attn_kernel,
        grid=(nhp, NT),
        in_specs=[
            pl.BlockSpec((TM, hw), lambda hp, i: (i, hp)),
            pl.BlockSpec((LP, hw), lambda hp, i: (0, nhp + hp)),
            pl.BlockSpec((LP, hw), lambda hp, i: (0, 2 * nhp + hp)),
            pl.BlockSpec((HEADS_PER_STEP, LP, LANE), lambda hp, i: (hp, 0, 0)),
            pl.BlockSpec((HEADS_PER_STEP, 1, HEAD_DIM), lambda hp, i: (hp, 0, 0)),
        ],
        out_specs=pl.BlockSpec((TM, hw), lambda hp, i: (i, hp)),
        out_shape=jax.ShapeDtypeStruct((LP, ATTN_WIDTH), BF16),
        scratch_shapes=[
            pltpu.VMEM((TM, HEAD_DIM), F32),
            pltpu.VMEM((TM, HEAD_DIM), F32),
            pltpu.VMEM((TM, HEAD_DIM), F32),
            pltpu.VMEM((TM, ATTN_WIDE), F32),
            pltpu.VMEM((TM, ATTN_WIDE), BF16),
        ] * HEADS_PER_STEP,
        compiler_params=_cparams(("arbitrary", "arbitrary")),
        name="attention",
    )(qkv, qkv, qkv, key_bias, g_attn)


CONV_RC = 64
CONV_CC = 512
CONV_ROWS = HALO + TM - SUBLANES


def _conv_kernel(x_ref, halo_ref, wdw_ref, bdw_ref, lng_ref, lnb_ref, wpw_ref, og_ref, o_ref,
                 xs_ref, cv_ref):
    i = pl.program_id(0)
    xs_ref[0, 0:HALO, :] = jnp.where(i == 0, 0.0, halo_ref[...])
    xs_ref[0, HALO:HALO + TM, :] = x_ref[...]
    for s in range(1, SUBLANES):
        xs_ref[s, 0:CONV_ROWS, :] = xs_ref[0, s:s + CONV_ROWS, :]
    off = HALO - (CONV_KERNEL - 1)
    for rc in range(TM // CONV_RC):
        for cc in range(CONV_WIDTH // CONV_CC):
            cs = slice(cc * CONV_CC, (cc + 1) * CONV_CC)
            acc = jnp.broadcast_to(bdw_ref[:, cs], (CONV_RC, CONV_CC))
            for j in range(CONV_KERNEL):
                s, a = (off + j) % SUBLANES, (off + j) // SUBLANES * SUBLANES
                r0 = rc * CONV_RC + a
                acc = acc + wdw_ref[j:j + 1, cs] * xs_ref[s, r0:r0 + CONV_RC, cs]
            cv_ref[rc * CONV_RC:(rc + 1) * CONV_RC, cs] = acc
    y = cv_ref[...]
    mu = jnp.mean(y, axis=-1, keepdims=True)
    yc = y - mu
    var = jnp.mean(yc * yc, axis=-1, keepdims=True)
    a = _silu(yc * lax.rsqrt(var + LN_EPS) * lng_ref[...] + lnb_ref[...])
    z = jnp.dot(a.astype(BF16), wpw_ref[...], preferred_element_type=F32)
    gd = CONV_WIDTH // CONV_GROUPS
    for g in range(CONV_GROUPS):
        gs = slice(g * gd, (g + 1) * gd)
        o_ref[:, gs] = _rms(z[:, gs], og_ref[:, gs]).astype(BF16)


def _conv(hc, w_dw, b_dw, ln_g, ln_b, w_pw, out_g):
    row = lambda i: (0, 0)
    return pl.pallas_call(
        _conv_kernel,
        grid=(NT,),
        in_specs=[
            pl.BlockSpec((TM, CONV_WIDTH), lambda i: (i, 0)),
            pl.BlockSpec((HALO, CONV_WIDTH), lambda i: (jnp.maximum(i * (TM // HALO) - 1, 0), 0)),
            pl.BlockSpec((HALO, CONV_WIDTH), row),
            pl.BlockSpec((1, CONV_WIDTH), row),
            pl.BlockSpec((1, CONV_WIDTH), row),
            pl.BlockSpec((1, CONV_WIDTH), row),
            pl.BlockSpec((CONV_WIDTH, CONV_WIDTH), row),
            pl.BlockSpec((1, CONV_WIDTH), row),
        ],
        out_specs=pl.BlockSpec((TM, CONV_WIDTH), lambda i: (i, 0)),
        out_shape=jax.ShapeDtypeStruct((LP, CONV_WIDTH), BF16),
        scratch_shapes=[
            pltpu.VMEM((SUBLANES, HALO + TM, CONV_WIDTH), F32),
            pltpu.VMEM((TM, CONV_WIDTH), F32),
        ],
        compiler_params=_cparams(("arbitrary",)),
        name="conv",
    )(hc, hc, w_dw, b_dw, ln_g, ln_b, w_pw, out_g)


def _outproj_kernel(a_ref, c_ref, w_ref, h_ref, g_ref, ho_ref, hn_ref):
    acc = jnp.dot(a_ref[...], w_ref[0:ATTN_WIDTH, :], preferred_element_type=F32)
    acc = acc + jnp.dot(c_ref[...], w_ref[ATTN_WIDTH:, :], preferred_element_type=F32)
    h_new = h_ref[...] + acc
    ho_ref[...] = h_new
    hn_ref[...] = _rms(h_new, g_ref[...]).astype(hn_ref.dtype)


def _outproj(attn, conv, w_out, h, g_next, hn_dtype):
    return pl.pallas_call(
        _outproj_kernel,
        grid=(NT,),
        in_specs=[
            pl.BlockSpec((TM, ATTN_WIDTH), lambda i: (i, 0)),
            pl.BlockSpec((TM, CONV_WIDTH), lambda i: (i, 0)),
            pl.BlockSpec((D_MODEL, D_MODEL), lambda i: (0, 0)),
            pl.BlockSpec((TM, D_MODEL), lambda i: (i, 0)),
            pl.BlockSpec((1, D_MODEL), lambda i: (0, 0)),
        ],
        out_specs=[
            pl.BlockSpec((TM, D_MODEL), lambda i: (i, 0)),
            pl.BlockSpec((TM, D_MODEL), lambda i: (i, 0)),
        ],
        out_shape=[
            jax.ShapeDtypeStruct((LP, D_MODEL), F32),
            jax.ShapeDtypeStruct((LP, D_MODEL), hn_dtype),
        ],
        compiler_params=_cparams(("arbitrary",)),
        name="outproj",
    )(attn, conv, w_out, h, g_next)


def _ffn_kernel(x_ref, wg_ref, wu_ref, wd_ref, h_ref, g_ref, ho_ref, hn_ref, acc_ref):
    j = pl.program_id(1)

    @pl.when(j == 0)
    def _():
        acc_ref[...] = jnp.zeros_like(acc_ref)

    x = x_ref[...]
    g = jnp.dot(x, wg_ref[...], preferred_element_type=F32)
    u = jnp.dot(x, wu_ref[...], preferred_element_type=F32)
    a = (_silu(g) * u).astype(BF16)
    acc_ref[...] += jnp.dot(a, wd_ref[...], preferred_element_type=F32)

    @pl.when(j == pl.num_programs(1) - 1)
    def _():
        h_new = h_ref[...] + acc_ref[...]
        ho_ref[...] = h_new
        hn_ref[...] = _rms(h_new, g_ref[...]).astype(BF16)


def _ffn(hn, w_gate, w_up, w_down, h, g_next):
    return pl.pallas_call(
        _ffn_kernel,
        grid=(NT, D_FF // FFN_TF),
        in_specs=[
            pl.BlockSpec((TM, D_MODEL), lambda i, j: (i, 0)),
            pl.BlockSpec((D_MODEL, FFN_TF), lambda i, j: (0, j)),
            pl.BlockSpec((D_MODEL, FFN_TF), lambda i, j: (0, j)),
            pl.BlockSpec((FFN_TF, D_MODEL), lambda i, j: (j, 0)),
            pl.BlockSpec((TM, D_MODEL), lambda i, j: (i, 0)),
            pl.BlockSpec((1, D_MODEL), lambda i, j: (0, 0)),
        ],
        out_specs=[
            pl.BlockSpec((TM, D_MODEL), lambda i, j: (i, 0)),
            pl.BlockSpec((TM, D_MODEL), lambda i, j: (i, 0)),
        ],
        out_shape=[
            jax.ShapeDtypeStruct((LP, D_MODEL), F32),
            jax.ShapeDtypeStruct((LP, D_MODEL), BF16),
        ],
        scratch_shapes=[pltpu.VMEM((TM, D_MODEL), F32)],
        compiler_params=_cparams(("arbitrary", "arbitrary")),
        name="ffn",
    )(hn, w_gate, w_up, w_down, h, g_next)


INFO_E1, INFO_E2, INFO_G1, INFO_G2, INFO_R1, INFO_R2 = range(6)


def _router_kernel(x_ref, w_ref, info_ref, cnt_ref, carry_ref):
    i = pl.program_id(0)

    @pl.when(i == 0)
    def _():
        carry_ref[...] = jnp.zeros_like(carry_ref)

    logits = jnp.dot(x_ref[...], w_ref[...], preferred_element_type=F32,
                     precision=lax.Precision.HIGHEST)
    lane = lax.broadcasted_iota(jnp.int32, logits.shape, 1)
    neg = jnp.float32(-3e38)
    l1 = jnp.where(lane < N_EXPERTS, logits, neg)
    m1 = jnp.max(l1, axis=-1, keepdims=True)
    i1 = jnp.min(jnp.where(l1 == m1, lane, LANE), axis=-1, keepdims=True)
    l2 = jnp.where(lane == i1, neg, l1)
    m2 = jnp.max(l2, axis=-1, keepdims=True)
    i2 = jnp.min(jnp.where(l2 == m2, lane, LANE), axis=-1, keepdims=True)
    ex = jnp.exp(m2 - m1)
    g1 = 1.0 / (1.0 + ex)
    g2 = ex / (1.0 + ex)
    sel = jnp.where(lane == i1, 1.0, jnp.where(lane == i2, 1.0, 0.0))
    r = lax.broadcasted_iota(jnp.int32, (TM, TM), 0)
    c = lax.broadcasted_iota(jnp.int32, (TM, TM), 1)
    tri = jnp.where(c < r, 1.0, 0.0).astype(BF16)
    carry = carry_ref[0:1, :]
    rank = jnp.dot(tri, sel.astype(BF16), preferred_element_type=F32) + carry
    r1 = jnp.sum(jnp.where(lane == i1, rank, 0.0), axis=-1, keepdims=True)
    r2 = jnp.sum(jnp.where(lane == i2, rank, 0.0), axis=-1, keepdims=True)
    info = jnp.zeros_like(logits)
    for k, v in ((INFO_E1, i1.astype(F32)), (INFO_E2, i2.astype(F32)), (INFO_G1, g1),
                 (INFO_G2, g2), (INFO_R1, r1), (INFO_R2, r2)):
        info = jnp.where(lane == k, v, info)
    info_ref[...] = info
    total = carry + jnp.sum(sel, axis=0, keepdims=True)
    carry_ref[...] = jnp.broadcast_to(total, carry_ref.shape)
    cnt_ref[...] = jnp.broadcast_to(total, cnt_ref.shape)


def _router(hn, w_r):
    return pl.pallas_call(
        _router_kernel,
        grid=(NT,),
        in_specs=[
            pl.BlockSpec((TM, D_MODEL), lambda i: (i, 0)),
            pl.BlockSpec((D_MODEL, LANE), lambda i: (0, 0)),
        ],
        out_specs=[
            pl.BlockSpec((TM, LANE), lambda i: (i, 0)),
            pl.BlockSpec((8, LANE), lambda i: (0, 0)),
        ],
        out_shape=[
            jax.ShapeDtypeStruct((LP, LANE), F32),
            jax.ShapeDtypeStruct((8, LANE), F32),
        ],
        scratch_shapes=[pltpu.VMEM((8, LANE), F32)],
        compiler_params=_cparams(("arbitrary",)),
        name="router",
    )(hn, w_r)


def _dispatch_kernel(pos_ref, x_ref, xs_in_ref, xs_ref, sem):
    del xs_in_ref
    i = pl.program_id(0)

    def issue(r, carry):
        base = (i * BLK + r) * 2
        for k in range(2):
            pltpu.make_async_copy(x_ref.at[pl.ds(r, 1)],
                                  xs_ref.at[pl.ds(pos_ref[base + k], 1)], sem).start()
        return carry

    lax.fori_loop(0, BLK, issue, 0, unroll=4)
    for _ in range(2):
        pltpu.make_async_copy(x_ref, xs_ref.at[pl.ds(0, BLK)], sem).wait()


def _dispatch(pos, hn):
    grid_spec = pltpu.PrefetchScalarGridSpec(
        num_scalar_prefetch=1,
        grid=(NB,),
        in_specs=[
            pl.BlockSpec((BLK, D_MODEL), lambda i, pos: (i, 0)),
            pl.BlockSpec(memory_space=pl.ANY),
        ],
        out_specs=pl.BlockSpec(memory_space=pl.ANY),
        scratch_shapes=[pltpu.SemaphoreType.DMA(())],
    )
    return pl.pallas_call(
        _dispatch_kernel,
        grid_spec=grid_spec,
        out_shape=jax.ShapeDtypeStruct((MOE_CAP, D_MODEL), F32),
        input_output_aliases={2: 0},
        compiler_params=_cparams(("arbitrary",)),
        name="dispatch",
    )(pos, hn, jnp.zeros((MOE_CAP, D_MODEL), F32))


def _stream_expert_rows(gs_ref, gn_ref, w_hbm, w_stage, wsem, src_hbm, dst_hbm, tn, ibuf, obuf, isem, osem,
                        cast_weights, compute):
    j, e = pl.program_id(0), pl.program_id(1)
    nj, ne = pl.num_programs(0), pl.num_programs(1)
    g0 = gs_ref[e]
    n = gn_ref[e]
    cols = pl.ds(pl.multiple_of(j * tn, tn), tn)

    def w_copy(jj, ee, k):
        src = w_hbm[k].at[ee, :, pl.ds(pl.multiple_of(jj * tn, tn), tn)]
        return pltpu.make_async_copy(src, w_stage[k], wsem.at[k])

    def in_copy(t, slot):
        return pltpu.make_async_copy(src_hbm.at[pl.ds((g0 + t) * MOE_TM, MOE_TM)], ibuf.at[slot],
                                     isem.at[slot])

    def out_copy(t, slot):
        return pltpu.make_async_copy(obuf.at[slot], dst_hbm.at[pl.ds((g0 + t) * MOE_TM, MOE_TM), cols],
                                     osem.at[slot])

    @pl.when(jnp.logical_and(j == 0, e == 0))
    def _():
        for k in range(len(w_hbm)):
            w_copy(j, e, k).start()

    @pl.when(n > 0)
    def _():
        in_copy(0, 0).start(priority=ROW_DMA_PRIORITY)

    for k in range(len(w_hbm)):
        w_copy(j, e, k).wait()
    cast_weights()
    last_e = e == ne - 1

    @pl.when(jnp.logical_not(jnp.logical_and(j == nj - 1, last_e)))
    def _():
        for k in range(len(w_hbm)):
            w_copy(jnp.where(last_e, j + 1, j), jnp.where(last_e, 0, e + 1), k).start()

    def body(t, carry):
        slot = lax.rem(t, 2)
        in_copy(t, slot).wait()

        @pl.when(t + 1 < n)
        def _():
            in_copy(t + 1, 1 - slot).start(priority=ROW_DMA_PRIORITY)

        out = compute(ibuf[slot])

        @pl.when(t >= 2)
        def _():
            out_copy(t - 2, slot).wait()

        obuf[slot] = out
        out_copy(t, slot).start(priority=ROW_DMA_PRIORITY)
        return carry

    lax.fori_loop(0, n, body, 0)
    for back in (2, 1):
        @pl.when(n >= back)
        def _():
            out_copy(n - back, lax.rem(n - back, 2)).wait()

    @pl.when(last_e)
    def _():
        obuf[0] = jnp.zeros(obuf.shape[1:], obuf.dtype)

        def fill(t, carry):
            cp = pltpu.make_async_copy(obuf.at[0], dst_hbm.at[pl.ds(t * MOE_TM, MOE_TM), cols],
                                       osem.at[0])
            cp.start()
            cp.wait()
            return carry

        lax.fori_loop(g0 + n, MOE_TILES, fill, 0)


def _moe_up_kernel(gs_ref, gn_ref, wg_hbm, wu_hbm, xs_hbm, h1_hbm, wg_stage, wu_stage, wgb_ref, wub_ref,
                   xbuf, obuf, wsem, xsem, osem):
    def cast_weights():
        wgb_ref[...] = wg_stage[...].astype(BF16)
        wub_ref[...] = wu_stage[...].astype(BF16)

    def compute(x):
        xb = x.astype(BF16)
        g = jnp.dot(xb, wgb_ref[...], preferred_element_type=F32)
        u = jnp.dot(xb, wub_ref[...], preferred_element_type=F32)
        return (_silu(g) * u).astype(BF16)

    _stream_expert_rows(gs_ref, gn_ref, (wg_hbm, wu_hbm), (wg_stage, wu_stage), wsem, xs_hbm, h1_hbm,
                        MOE_TN_A, xbuf, obuf, xsem, osem, cast_weights, compute)


def _moe_up(gs, gn, xs, w_gate, w_up):
    any_spec = pl.BlockSpec(memory_space=pl.ANY)
    grid_spec = pltpu.PrefetchScalarGridSpec(
        num_scalar_prefetch=2,
        grid=(D_EXPERT // MOE_TN_A, N_EXPERTS),
        in_specs=[any_spec, any_spec, any_spec],
        out_specs=any_spec,
        scratch_shapes=[
            pltpu.VMEM((D_MODEL, MOE_TN_A), F32),
            pltpu.VMEM((D_MODEL, MOE_TN_A), F32),
            pltpu.VMEM((D_MODEL, MOE_TN_A), BF16),
            pltpu.VMEM((D_MODEL, MOE_TN_A), BF16),
            pltpu.VMEM((2, MOE_TM, D_MODEL), F32),
            pltpu.VMEM((2, MOE_TM, MOE_TN_A), BF16),
            pltpu.SemaphoreType.DMA((2,)),
            pltpu.SemaphoreType.DMA((2,)),
            pltpu.SemaphoreType.DMA((2,)),
        ],
    )
    return pl.pallas_call(
        _moe_up_kernel,
        grid_spec=grid_spec,
        out_shape=jax.ShapeDtypeStruct((MOE_CAP, D_EXPERT), BF16),
        compiler_params=_cparams(("arbitrary", "arbitrary")),
        name="moe_up",
    )(gs, gn, w_gate, w_up, xs)


def _moe_down_kernel(gs_ref, gn_ref, wd_hbm, h1_hbm, y_hbm, wd_stage, wdb_ref, hbuf, obuf,
                     wsem, hsem, osem):
    def cast_weights():
        wdb_ref[...] = wd_stage[...].astype(BF16)

    def compute(h):
        return jnp.dot(h, wdb_ref[...], preferred_element_type=F32)

    _stream_expert_rows(gs_ref, gn_ref, (wd_hbm,), (wd_stage,), wsem, h1_hbm, y_hbm, MOE_TN_B,
                        hbuf, obuf, hsem, osem, cast_weights, compute)


def _moe_down(gs, gn, h1, w_down):
    any_spec = pl.BlockSpec(memory_space=pl.ANY)
    grid_spec = pltpu.PrefetchScalarGridSpec(
        num_scalar_prefetch=2,
        grid=(D_MODEL // MOE_TN_B, N_EXPERTS),
        in_specs=[any_spec, any_spec],
        out_specs=any_spec,
        scratch_shapes=[
            pltpu.VMEM((D_EXPERT, MOE_TN_B), F32),
            pltpu.VMEM((D_EXPERT, MOE_TN_B), BF16),
            pltpu.VMEM((2, MOE_TM, D_EXPERT), BF16),
            pltpu.VMEM((2, MOE_TM, MOE_TN_B), F32),
            pltpu.SemaphoreType.DMA((1,)),
            pltpu.SemaphoreType.DMA((2,)),
            pltpu.SemaphoreType.DMA((2,)),
        ],
    )
    return pl.pallas_call(
        _moe_down_kernel,
        grid_spec=grid_spec,
        out_shape=jax.ShapeDtypeStruct((MOE_CAP, D_MODEL), F32),
        compiler_params=_cparams(("arbitrary", "arbitrary")),
        name="moe_down",
    )(gs, gn, w_down, h1)


def _combine_kernel(pos_ref, info_ref, g_ref, h_hbm, y_hbm, o_ref, hbuf, ybuf, sem_h, sem_y):
    i = pl.program_id(0)
    slot = lax.rem(i, 2)

    def h_copy(t, s):
        return pltpu.make_async_copy(h_hbm.at[pl.ds(BLK + t * OUT_T, OUT_T)], hbuf.at[s], sem_h.at[s])

    def fetch(t, s):
        h_copy(t, s).start()

        def issue(r, carry):
            base = (t * OUT_T + r) * 2
            for k in range(2):
                pltpu.make_async_copy(y_hbm.at[pl.ds(pos_ref[base + k], 1)],
                                      ybuf.at[s, k, pl.ds(r, 1)], sem_y.at[s]).start()
            return carry

        lax.fori_loop(0, OUT_T, issue, 0, unroll=4)

    @pl.when(i == 0)
    def _():
        fetch(0, 0)

    @pl.when(i + 1 < pl.num_programs(0))
    def _():
        fetch(i + 1, 1 - slot)

    h_copy(i, slot).wait()
    for k in range(2):
        pltpu.make_async_copy(y_hbm.at[pl.ds(0, OUT_T)], ybuf.at[slot, k], sem_y.at[slot]).wait()
    g1 = info_ref[:, INFO_G1:INFO_G1 + 1]
    g2 = info_ref[:, INFO_G2:INFO_G2 + 1]
    h_new = hbuf[slot] + g1 * ybuf[slot, 0] + g2 * ybuf[slot, 1]
    o_ref[...] = _rms(h_new, g_ref[...])


def _combine(pos_x, info_x, g_final, h, y):
    grid_spec = pltpu.PrefetchScalarGridSpec(
        num_scalar_prefetch=1,
        grid=(SEQ // OUT_T,),
        in_specs=[
            pl.BlockSpec((OUT_T, LANE), lambda i, pos: (i, 0)),
            pl.BlockSpec((1, D_MODEL), lambda i, pos: (0, 0)),
            pl.BlockSpec(memory_space=pl.ANY),
            pl.BlockSpec(memory_space=pl.ANY),
        ],
        out_specs=pl.BlockSpec((OUT_T, D_MODEL), lambda i, pos: (i, 0)),
        scratch_shapes=[
            pltpu.VMEM((2, OUT_T, D_MODEL), F32),
            pltpu.VMEM((2, 2, OUT_T, D_MODEL), F32),
            pltpu.SemaphoreType.DMA((2,)),
            pltpu.SemaphoreType.DMA((2,)),
        ],
    )
    return pl.pallas_call(
        _combine_kernel,
        grid_spec=grid_spec,
        out_shape=jax.ShapeDtypeStruct((SEQ, D_MODEL), F32),
        compiler_params=_cparams(("arbitrary",)),
        name="combine",
    )(pos_x, info_x, g_final, h, y)


def _mixer(h, hn, layer, p, g_next, hn_dtype):
    w_in = p["w_in"][layer]
    a0 = 3 * ATTN_WIDTH
    u0 = a0 + N_HEADS
    w_qkv = w_in[:, :a0].astype(BF16)
    w_f = jnp.pad(w_in[:, a0:u0], ((0, 0), (0, LANE - N_HEADS))).astype(BF16)
    w_a = w_in[:, u0:u0 + CONV_WIDTH].astype(BF16)
    w_g = w_in[:, u0 + CONV_WIDTH:].astype(BF16)
    b_f = jnp.pad(p["b_forget"][layer], (0, LANE - N_HEADS)).reshape(1, LANE)

    qkv = _qkv(hn, w_qkv)
    key_bias = _decay(hn, w_f, b_f)
    attn = _attention(qkv, key_bias, p["attn_out_g"][layer].reshape(N_HEADS, 1, HEAD_DIM))

    hc = _glu(hn, w_a, w_g)
    w_dw = jnp.pad(p["w_dw"][layer], ((0, HALO - CONV_KERNEL), (0, 0)))
    row = lambda v: v.reshape(1, -1)
    conv = _conv(hc, w_dw, row(p["b_dw"][layer]), row(p["conv_ln_g"][layer]),
                 row(p["conv_ln_b"][layer]), p["w_conv_out"][layer].astype(BF16),
                 row(p["conv_out_g"][layer]))
    return _outproj(attn, conv, p["w_out"][layer].astype(BF16), h, g_next, hn_dtype)


def _moe(h, hn, p, j, g_final):
    w_r = jnp.pad(p["moe_w_router"][j], ((0, 0), (0, LANE - N_EXPERTS)))
    info, cnt = _router(hn, w_r)
    e1 = info[:, INFO_E1].astype(jnp.int32)
    e2 = info[:, INFO_E2].astype(jnp.int32)
    r1 = info[:, INFO_R1].astype(jnp.int32)
    r2 = info[:, INFO_R2].astype(jnp.int32)
    counts = cnt[0, :N_EXPERTS].astype(jnp.int32)
    padded = (counts + MOE_TM - 1) // MOE_TM * MOE_TM
    gend = jnp.cumsum(padded)
    gstart = gend - padded
    pos = jnp.stack([gstart[e1] + r1, gstart[e2] + r2], axis=-1).reshape(-1)
    gs = (gstart // MOE_TM).astype(jnp.int32)
    gn = (padded // MOE_TM).astype(jnp.int32)

    xs = _dispatch(pos, hn)
    h1 = _moe_up(gs, gn, xs, p["moe_w_gate"][j], p["moe_w_up"][j])
    y = _moe_down(gs, gn, h1, p["moe_w_down"][j])
    return _combine(pos[2 * BLK:], info[BLK:], g_final, h, y)


def kernel(x, meta_tokens, mix_norm_g, ffn_norm_g, w_in, b_forget, w_dw, b_dw, conv_ln_g, conv_ln_b, w_conv_out, attn_out_g, conv_out_g, w_out, dense_w_gate, dense_w_up, dense_w_down, moe_w_router, moe_w_gate, moe_w_up, moe_w_down, final_norm_g):
    assert x.shape == (1, SEQ, D_MODEL) and meta_tokens.shape == (N_META, D_MODEL)
    p = dict(w_in=w_in, b_forget=b_forget, w_dw=w_dw, b_dw=b_dw, conv_ln_g=conv_ln_g,
             conv_ln_b=conv_ln_b, w_conv_out=w_conv_out, attn_out_g=attn_out_g,
             conv_out_g=conv_out_g, w_out=w_out, moe_w_router=moe_w_router,
             moe_w_gate=moe_w_gate, moe_w_up=moe_w_up, moe_w_down=moe_w_down)
    row = lambda v: v.reshape(1, D_MODEL)
    meta_blk = jnp.pad(meta_tokens.astype(F32), ((PAD, 0), (0, 0)))
    h, hn = _prep(meta_blk, x.reshape(SEQ, D_MODEL), row(mix_norm_g[0]))

    h, hn = _mixer(h, hn, 0, p, row(ffn_norm_g[0]), BF16)
    h, hn = _ffn(hn, dense_w_gate[0].astype(BF16), dense_w_up[0].astype(BF16),
                 dense_w_down[0].astype(BF16), h, row(mix_norm_g[1]))
    h, hn = _mixer(h, hn, 1, p, row(ffn_norm_g[1]), F32)
    out = _moe(h, hn, p, 0, row(final_norm_g))
    return out.reshape(1, SEQ, D_MODEL)
```

```python
import functools

import jax
import jax.numpy as jnp
from jax import lax
from jax.experimental import pallas as pl
from jax.experimental.pallas import tpu as pltpu

F32 = jnp.float32
BF16 = jnp.bfloat16

D_MODEL = 2048
SEQ = 8192
N_META = 16
N_HEADS = 8
HEAD_DIM = 128
ATTN_WIDTH = N_HEADS * HEAD_DIM
CONV_WIDTH = 1024
CONV_GROUPS = 8
CONV_KERNEL = 31
D_FF = 5632
N_EXPERTS = 8
D_EXPERT = 7168
RMS_EPS = 1e-6
LN_EPS = 1e-5
MASK_VALUE = -1e30
LOG2E = 1.4426950408889634

LANE = 128
SUBLANES = 8
BLK = 128
PAD = BLK - N_META
LP = PAD + N_META + SEQ
TM = 640
NT = LP // TM
NB = LP // BLK
HALO = 32
MOE_TM = 512
MOE_TILES = -(-(2 * LP + N_EXPERTS * (MOE_TM - 1)) // MOE_TM)
MOE_CAP = MOE_TILES * MOE_TM
MOE_TN_A = 1024
MOE_TN_B = 512
FFN_TF = 512
OUT_T = 512
VMEM_LIMIT = 56 * 1024 * 1024
HEADS_PER_STEP = 2
ATTN_HW = HEADS_PER_STEP * HEAD_DIM
ATTN_GROUPS = N_HEADS // HEADS_PER_STEP
ROW_DMA_PRIORITY = 1
W_SPLIT = 8
ROW_SPLIT = 4


def _cparams(sem):
    return pltpu.CompilerParams(dimension_semantics=sem, vmem_limit_bytes=VMEM_LIMIT)


def _rms(x, g):
    return x * lax.rsqrt(jnp.mean(x * x, axis=-1, keepdims=True) + RMS_EPS) * g


def _silu(x):
    return x * jax.nn.sigmoid(x)


def _prep_kernel(meta_ref, x_ref, g_ref, h_ref, hn_ref):
    i = pl.program_id(0)
    v = jnp.where(i == 0, meta_ref[...], x_ref[...])
    h_ref[...] = v
    hn_ref[...] = _rms(v, g_ref[...]).astype(BF16)


def _prep(meta_blk, x2d, g):
    return pl.pallas_call(
        _prep_kernel,
        grid=(NB,),
        in_specs=[
            pl.BlockSpec((BLK, D_MODEL), lambda i: (0, 0)),
            pl.BlockSpec((BLK, D_MODEL), lambda i: (jnp.maximum(i - 1, 0), 0)),
            pl.BlockSpec((1, D_MODEL), lambda i: (0, 0)),
        ],
        out_specs=[
            pl.BlockSpec((BLK, D_MODEL), lambda i: (i, 0)),
            pl.BlockSpec((BLK, D_MODEL), lambda i: (i, 0)),
        ],
        out_shape=[
            jax.ShapeDtypeStruct((LP, D_MODEL), F32),
            jax.ShapeDtypeStruct((LP, D_MODEL), BF16),
        ],
        compiler_params=_cparams(("arbitrary",)),
        name="prep",
    )(meta_blk, x2d, g)


def _qkv_kernel(x_ref, w_ref, o_ref):
    j = pl.program_id(0)
    acc = jnp.dot(x_ref[...], w_ref[...], preferred_element_type=F32)
    scale = jnp.where(j == 0, HEAD_DIM ** -0.5 * LOG2E, 1.0).astype(F32)
    res = (acc * scale).astype(BF16)
    for s in range(o_ref.shape[0]):
        o_ref[s] = res[:, s * ATTN_HW:(s + 1) * ATTN_HW]


def _qkv(hn, w_qkv):
    return pl.pallas_call(
        _qkv_kernel,
        grid=(3, NT),
        in_specs=[
            pl.BlockSpec((TM, D_MODEL), lambda j, i: (i, 0)),
            pl.BlockSpec((D_MODEL, ATTN_WIDTH), lambda j, i: (0, j)),
        ],
        out_specs=pl.BlockSpec((ATTN_GROUPS, TM, ATTN_HW), lambda j, i: (j, i, 0)),
        out_shape=jax.ShapeDtypeStruct((3 * ATTN_GROUPS, LP, ATTN_HW), BF16),
        compiler_params=_cparams(("arbitrary", "arbitrary")),
        name="qkv",
    )(hn, w_qkv)


def _glu_kernel(x_ref, wa_ref, wg_ref, o_ref):
    i = pl.program_id(0)
    x = x_ref[...]
    a = jnp.dot(x, wa_ref[...], preferred_element_type=F32)
    g = jnp.dot(x, wg_ref[...], preferred_element_type=F32)
    row = lax.broadcasted_iota(jnp.int32, a.shape, 0) + i * TM
    o_ref[...] = jnp.where(row >= PAD, a * jax.nn.sigmoid(g), 0.0)


def _glu(hn, w_a, w_g):
    return pl.pallas_call(
        _glu_kernel,
        grid=(NT,),
        in_specs=[
            pl.BlockSpec((TM, D_MODEL), lambda i: (i, 0)),
            pl.BlockSpec((D_MODEL, CONV_WIDTH), lambda i: (0, 0)),
            pl.BlockSpec((D_MODEL, CONV_WIDTH), lambda i: (0, 0)),
        ],
        out_specs=pl.BlockSpec((TM, CONV_WIDTH), lambda i: (i, 0)),
        out_shape=jax.ShapeDtypeStruct((LP, CONV_WIDTH), F32),
        compiler_params=_cparams(("arbitrary",)),
        name="glu",
    )(hn, w_a, w_g)


def _split3(x):
    hi = x.astype(BF16)
    r1 = x - hi.astype(F32)
    mid = r1.astype(BF16)
    lo = (r1 - mid.astype(F32)).astype(BF16)
    return hi, mid, lo


def _decay_kernel(x_ref, w_ref, b_ref, kb_ref, carry_ref):
    i = pl.program_id(0)

    @pl.when(i == 0)
    def _():
        carry_ref[...] = jnp.zeros_like(carry_ref)

    f = jnp.dot(x_ref[...], w_ref[...], preferred_element_type=F32) + b_ref[...]
    log_f = jnp.minimum(f, 0.0) - jnp.log1p(jnp.exp(-jnp.abs(f)))
    row = lax.broadcasted_iota(jnp.int32, log_f.shape, 0) + i * TM
    log_f = jnp.where(row >= PAD, log_f, 0.0)
    r = lax.broadcasted_iota(jnp.int32, (TM, TM), 0)
    c = lax.broadcasted_iota(jnp.int32, (TM, TM), 1)
    tri = jnp.where(c <= r, 1.0, 0.0).astype(BF16)
    hi, mid, lo = _split3(log_f)
    cs = (jnp.dot(tri, hi, preferred_element_type=F32)
          + jnp.dot(tri, mid, preferred_element_type=F32)
          + jnp.dot(tri, lo, preferred_element_type=F32)) + carry_ref[0:1, :]
    carry_ref[...] = jnp.broadcast_to(cs[TM - 1:TM, :], carry_ref.shape)
    lane = lax.broadcasted_iota(jnp.int32, (TM, LANE), 1)
    for h in range(N_HEADS):
        bias = jnp.where(row[:, h:h + 1] >= PAD, -LOG2E * cs[:, h:h + 1], MASK_VALUE)
        hi, mid, lo = (v.astype(F32) for v in _split3(jnp.broadcast_to(bias, (TM, LANE))))
        pieces = jnp.where(lane == 0, hi, jnp.where(lane == 1, mid, jnp.where(lane == 2, lo, 0.0)))
        kb_ref[h] = pieces.astype(BF16)


def _decay(hn, w_f, b_f):
    return pl.pallas_call(
        _decay_kernel,
        grid=(NT,),
        in_specs=[
            pl.BlockSpec((TM, D_MODEL), lambda i: (i, 0)),
            pl.BlockSpec((D_MODEL, LANE), lambda i: (0, 0)),
            pl.BlockSpec((1, LANE), lambda i: (0, 0)),
        ],
        out_specs=pl.BlockSpec((N_HEADS, TM, LANE), lambda i: (0, i, 0)),
        out_shape=jax.ShapeDtypeStruct((N_HEADS, LP, LANE), BF16),
        scratch_shapes=[pltpu.VMEM((8, LANE), F32)],
        compiler_params=_cparams(("arbitrary",)),
        name="decay",
    )(hn, w_f, b_f)


ATTN_STRIP = 16
ATTN_WIDE = 2 * TM


def _attn_kernel(q_ref, k_ref, v_ref, kb_ref, g_ref, o_ref, *state_refs):
    i = pl.program_id(1)
    n_state = len(state_refs) // HEADS_PER_STEP
    heads = [state_refs[n_state * h:n_state * (h + 1)] for h in range(HEADS_PER_STEP)]
    for m_ref, l_ref, acc_ref, _, _ in heads:
        m_ref[...] = jnp.full_like(m_ref, -1e38)
        l_ref[...] = jnp.zeros_like(l_ref)
        acc_ref[...] = jnp.zeros_like(acc_ref)

    lane = lax.broadcasted_iota(jnp.int32, (TM, HEAD_DIM), 1)
    q_ones = jnp.where(lane < 3, 1.0, 0.0).astype(BF16)

    def lanes(h):
        return slice(h * HEAD_DIM, (h + 1) * HEAD_DIM)

    def block(key0, width, diagonal):
        keys = pl.ds(pl.multiple_of(key0, TM), width)
        for h, (_, _, _, s_ref, _) in enumerate(heads):
            q_aug = jnp.concatenate([q_ref[0, :, lanes(h)], q_ones], axis=1)
            k_aug = jnp.concatenate([k_ref[0, keys, lanes(h)], kb_ref[h, keys, :]], axis=1)
            s_ref[:, :width] = lax.dot_general(q_aug, k_aug, (((1,), (1,)), ((), ())),
                                               preferred_element_type=F32)
        for h, (m_ref, l_ref, acc_ref, s_ref, p_ref) in enumerate(heads):
            for r in range(TM // ATTN_STRIP):
                rs = slice(r * ATTN_STRIP, (r + 1) * ATTN_STRIP)
                nc = min(width, -(-((r + 1) * ATTN_STRIP) // LANE) * LANE) if diagonal else width

                def logits():
                    s = s_ref[rs, :nc]
                    if diagonal:
                        row = lax.broadcasted_iota(jnp.int32, s.shape, 0) + r * ATTN_STRIP
                        col = lax.broadcasted_iota(jnp.int32, s.shape, 1)
                        s = jnp.where(col <= row, s, MASK_VALUE)
                    return s

                m_prev = m_ref[rs, :]
                m_new = jnp.maximum(m_prev, jnp.max(logits(), axis=-1, keepdims=True))
                p = jnp.exp2(logits() - jnp.concatenate([m_new] * (nc // LANE), axis=1))
                p_ref[rs, :nc] = p.astype(BF16)
                if nc < width:
                    p_ref[rs, nc:width] = jnp.zeros((ATTN_STRIP, width - nc), BF16)
                alpha = jnp.exp2(m_prev - m_new)
                l_ref[rs, :] = alpha * l_ref[rs, :] + jnp.sum(p, axis=-1, keepdims=True)
                m_ref[rs, :] = m_new
                acc_ref[rs, :] = alpha * acc_ref[rs, :]
            acc_ref[...] += jnp.dot(p_ref[:, :width], v_ref[0, keys, lanes(h)],
                                    preferred_element_type=F32)

    def wide_block(j, carry):
        block(j * ATTN_WIDE, ATTN_WIDE, False)
        return carry

    lax.fori_loop(0, i // 2, wide_block, 0)

    @pl.when(i % 2 == 1)
    def _():
        block((i - 1) * TM, TM, False)

    block(i * TM, TM, True)
    for h, (_, l_ref, acc_ref, _, _) in enumerate(heads):
        hs = slice(h * HEAD_DIM, (h + 1) * HEAD_DIM)
        o = acc_ref[...] / l_ref[...]
        o_ref[:, hs] = _rms(o, g_ref[h]).astype(BF16)


def _attention(qkv, key_bias, g_attn):
    hw = ATTN_HW
    nhp = ATTN_GROUPS
    return pl.pallas_call(
        _attn_kernel,
        grid=(nhp, NT),
        in_specs=[
            pl.BlockSpec((1, TM, hw), lambda hp, i: (hp, i, 0)),
            pl.BlockSpec((1, LP, hw), lambda hp, i: (nhp + hp, 0, 0)),
            pl.BlockSpec((1, LP, hw), lambda hp, i: (2 * nhp + hp, 0, 0)),
            pl.BlockSpec((HEADS_PER_STEP, LP, LANE), lambda hp, i: (hp, 0, 0)),
            pl.BlockSpec((HEADS_PER_STEP, 1, HEAD_DIM), lambda hp, i: (hp, 0, 0)),
        ],
        out_specs=pl.BlockSpec((TM, hw), lambda hp, i: (i, hp)),
        out_shape=jax.ShapeDtypeStruct((LP, ATTN_WIDTH), BF16),
        scratch_shapes=[
            pltpu.VMEM((TM, HEAD_DIM), F32),
            pltpu.VMEM((TM, HEAD_DIM), F32),
            pltpu.VMEM((TM, HEAD_DIM), F32),
            pltpu.VMEM((TM, ATTN_WIDE), F32),
            pltpu.VMEM((TM, ATTN_WIDE), BF16),
        ] * HEADS_PER_STEP,
        compiler_params=_cparams(("arbitrary", "arbitrary")),
        name="attention",
    )(qkv, qkv, qkv, key_bias, g_attn)


CONV_RC = 64
CONV_CC = 512
CONV_ROWS = HALO + TM - SUBLANES


def _conv_kernel(x_ref, halo_ref, wdw_ref, bdw_ref, lng_ref, lnb_ref, wpw_ref, og_ref, o_ref,
                 xs_ref, cv_ref):
    i = pl.program_id(0)
    xs_ref[0, 0:HALO, :] = jnp.where(i == 0, 0.0, halo_ref[...])
    xs_ref[0, HALO:HALO + TM, :] = x_ref[...]
    for s in range(1, SUBLANES):
        xs_ref[s, 0:CONV_ROWS, :] = xs_ref[0, s:s + CONV_ROWS, :]
    off = HALO - (CONV_KERNEL - 1)
    for rc in range(TM // CONV_RC):
        for cc in range(CONV_WIDTH // CONV_CC):
            cs = slice(cc * CONV_CC, (cc + 1) * CONV_CC)
            acc = jnp.broadcast_to(bdw_ref[:, cs], (CONV_RC, CONV_CC))
            for j in range(CONV_KERNEL):
                s, a = (off + j) % SUBLANES, (off + j) // SUBLANES * SUBLANES
                r0 = rc * CONV_RC + a
                acc = acc + wdw_ref[j:j + 1, cs] * xs_ref[s, r0:r0 + CONV_RC, cs]
            cv_ref[rc * CONV_RC:(rc + 1) * CONV_RC, cs] = acc
    y = cv_ref[...]
    mu = jnp.mean(y, axis=-1, keepdims=True)
    yc = y - mu
    var = jnp.mean(yc * yc, axis=-1, keepdims=True)
    a = _silu(yc * lax.rsqrt(var + LN_EPS) * lng_ref[...] + lnb_ref[...])
    z = jnp.dot(a.astype(BF16), wpw_ref[...], preferred_element_type=F32)
    gd = CONV_WIDTH // CONV_GROUPS
    for g in range(CONV_GROUPS):
        gs = slice(g * gd, (g + 1) * gd)
        o_ref[:, gs] = _rms(z[:, gs], og_ref[:, gs]).astype(BF16)


def _conv(hc, w_dw, b_dw, ln_g, ln_b, w_pw, out_g):
    row = lambda i: (0, 0)
    return pl.pallas_call(
        _conv_kernel,
        grid=(NT,),
        in_specs=[
            pl.BlockSpec((TM, CONV_WIDTH), lambda i: (i, 0)),
            pl.BlockSpec((HALO, CONV_WIDTH), lambda i: (jnp.maximum(i * (TM // HALO) - 1, 0), 0)),
            pl.BlockSpec((HALO, CONV_WIDTH), row),
            pl.BlockSpec((1, CONV_WIDTH), row),
            pl.BlockSpec((1, CONV_WIDTH), row),
            pl.BlockSpec((1, CONV_WIDTH), row),
            pl.BlockSpec((CONV_WIDTH, CONV_WIDTH), row),
            pl.BlockSpec((1, CONV_WIDTH), row),
        ],
        out_specs=pl.BlockSpec((TM, CONV_WIDTH), lambda i: (i, 0)),
        out_shape=jax.ShapeDtypeStruct((LP, CONV_WIDTH), BF16),
        scratch_shapes=[
            pltpu.VMEM((SUBLANES, HALO + TM, CONV_WIDTH), F32),
            pltpu.VMEM((TM, CONV_WIDTH), F32),
        ],
        compiler_params=_cparams(("arbitrary",)),
        name="conv",
    )(hc, hc, w_dw, b_dw, ln_g, ln_b, w_pw, out_g)


def _outproj_kernel(a_ref, c_ref, w_ref, h_ref, g_ref, ho_ref, hn_ref):
    acc = jnp.dot(a_ref[...], w_ref[0:ATTN_WIDTH, :], preferred_element_type=F32)
    acc = acc + jnp.dot(c_ref[...], w_ref[ATTN_WIDTH:, :], preferred_element_type=F32)
    h_new = h_ref[...] + acc
    ho_ref[...] = h_new
    hn_ref[...] = _rms(h_new, g_ref[...]).astype(hn_ref.dtype)


def _outproj(attn, conv, w_out, h, g_next, hn_dtype):
    return pl.pallas_call(
        _outproj_kernel,
        grid=(NT,),
        in_specs=[
            pl.BlockSpec((TM, ATTN_WIDTH), lambda i: (i, 0)),
            pl.BlockSpec((TM, CONV_WIDTH), lambda i: (i, 0)),
            pl.BlockSpec((D_MODEL, D_MODEL), lambda i: (0, 0)),
            pl.BlockSpec((TM, D_MODEL), lambda i: (i, 0)),
            pl.BlockSpec((1, D_MODEL), lambda i: (0, 0)),
        ],
        out_specs=[
            pl.BlockSpec((TM, D_MODEL), lambda i: (i, 0)),
            pl.BlockSpec((TM, D_MODEL), lambda i: (i, 0)),
        ],
        out_shape=[
            jax.ShapeDtypeStruct((LP, D_MODEL), F32),
            jax.ShapeDtypeStruct((LP, D_MODEL), hn_dtype),
        ],
        compiler_params=_cparams(("arbitrary",)),
        name="outproj",
    )(attn, conv, w_out, h, g_next)


def _ffn_kernel(x_ref, wg_ref, wu_ref, wd_ref, h_ref, g_ref, ho_ref, hn_ref, acc_ref):
    j = pl.program_id(1)

    @pl.when(j == 0)
    def _():
        acc_ref[...] = jnp.zeros_like(acc_ref)

    x = x_ref[...]
    g = jnp.dot(x, wg_ref[0], preferred_element_type=F32)
    u = jnp.dot(x, wu_ref[0], preferred_element_type=F32)
    a = (_silu(g) * u).astype(BF16)
    acc_ref[...] += jnp.dot(a, wd_ref[...], preferred_element_type=F32)

    @pl.when(j == pl.num_programs(1) - 1)
    def _():
        h_new = h_ref[...] + acc_ref[...]
        ho_ref[...] = h_new
        hn_ref[...] = _rms(h_new, g_ref[...]).astype(BF16)


def _ffn(hn, w_gate, w_up, w_down, h, g_next):
    return pl.pallas_call(
        _ffn_kernel,
        grid=(NT, D_FF // FFN_TF),
        in_specs=[
            pl.BlockSpec((TM, D_MODEL), lambda i, j: (i, 0)),
            pl.BlockSpec((1, D_MODEL, FFN_TF), lambda i, j: (j, 0, 0)),
            pl.BlockSpec((1, D_MODEL, FFN_TF), lambda i, j: (j, 0, 0)),
            pl.BlockSpec((FFN_TF, D_MODEL), lambda i, j: (j, 0)),
            pl.BlockSpec((TM, D_MODEL), lambda i, j: (i, 0)),
            pl.BlockSpec((1, D_MODEL), lambda i, j: (0, 0)),
        ],
        out_specs=[
            pl.BlockSpec((TM, D_MODEL), lambda i, j: (i, 0)),
            pl.BlockSpec((TM, D_MODEL), lambda i, j: (i, 0)),
        ],
        out_shape=[
            jax.ShapeDtypeStruct((LP, D_MODEL), F32),
            jax.ShapeDtypeStruct((LP, D_MODEL), BF16),
        ],
        scratch_shapes=[pltpu.VMEM((TM, D_MODEL), F32)],
        compiler_params=_cparams(("arbitrary", "arbitrary")),
        name="ffn",
    )(hn, w_gate, w_up, w_down, h, g_next)


INFO_E1, INFO_E2, INFO_G1, INFO_G2, INFO_R1, INFO_R2 = range(6)


def _router_kernel(x_ref, w_ref, info_ref, cnt_ref, carry_ref):
    i = pl.program_id(0)

    @pl.when(i == 0)
    def _():
        carry_ref[...] = jnp.zeros_like(carry_ref)

    logits = jnp.dot(x_ref[...], w_ref[...], preferred_element_type=F32,
                     precision=lax.Precision.HIGHEST)
    lane = lax.broadcasted_iota(jnp.int32, logits.shape, 1)
    neg = jnp.float32(-3e38)
    l1 = jnp.where(lane < N_EXPERTS, logits, neg)
    m1 = jnp.max(l1, axis=-1, keepdims=True)
    i1 = jnp.min(jnp.where(l1 == m1, lane, LANE), axis=-1, keepdims=True)
    l2 = jnp.where(lane == i1, neg, l1)
    m2 = jnp.max(l2, axis=-1, keepdims=True)
    i2 = jnp.min(jnp.where(l2 == m2, lane, LANE), axis=-1, keepdims=True)
    ex = jnp.exp(m2 - m1)
    g1 = 1.0 / (1.0 + ex)
    g2 = ex / (1.0 + ex)
    sel = jnp.where(lane == i1, 1.0, jnp.where(lane == i2, 1.0, 0.0))
    r = lax.broadcasted_iota(jnp.int32, (TM, TM), 0)
    c = lax.broadcasted_iota(jnp.int32, (TM, TM), 1)
    tri = jnp.where(c < r, 1.0, 0.0).astype(BF16)
    carry = carry_ref[0:1, :]
    rank = jnp.dot(tri, sel.astype(BF16), preferred_element_type=F32) + carry
    r1 = jnp.sum(jnp.where(lane == i1, rank, 0.0), axis=-1, keepdims=True)
    r2 = jnp.sum(jnp.where(lane == i2, rank, 0.0), axis=-1, keepdims=True)
    info = jnp.zeros_like(logits)
    for k, v in ((INFO_E1, i1.astype(F32)), (INFO_E2, i2.astype(F32)), (INFO_G1, g1),
                 (INFO_G2, g2), (INFO_R1, r1), (INFO_R2, r2)):
        info = jnp.where(lane == k, v, info)
    info_ref[...] = info
    total = carry + jnp.sum(sel, axis=0, keepdims=True)
    carry_ref[...] = jnp.broadcast_to(total, carry_ref.shape)
    cnt_ref[...] = jnp.broadcast_to(total, cnt_ref.shape)


def _router(hn, w_r):
    return pl.pallas_call(
        _router_kernel,
        grid=(NT,),
        in_specs=[
            pl.BlockSpec((TM, D_MODEL), lambda i: (i, 0)),
            pl.BlockSpec((D_MODEL, LANE), lambda i: (0, 0)),
        ],
        out_specs=[
            pl.BlockSpec((TM, LANE), lambda i: (i, 0)),
            pl.BlockSpec((8, LANE), lambda i: (0, 0)),
        ],
        out_shape=[
            jax.ShapeDtypeStruct((LP, LANE), F32),
            jax.ShapeDtypeStruct((8, LANE), F32),
        ],
        scratch_shapes=[pltpu.VMEM((8, LANE), F32)],
        compiler_params=_cparams(("arbitrary",)),
        name="router",
    )(hn, w_r)


def _dispatch_kernel(pos_ref, x_ref, xs_in_ref, xs_ref, sem):
    del xs_in_ref
    i = pl.program_id(0)

    def issue(r, carry):
        base = (i * BLK + r) * 2
        for k in range(2):
            pltpu.make_async_copy(x_ref.at[pl.ds(r, 1)],
                                  xs_ref.at[pl.ds(pos_ref[base + k], 1)], sem).start()
        return carry

    lax.fori_loop(0, BLK, issue, 0, unroll=4)
    for _ in range(2):
        pltpu.make_async_copy(x_ref, xs_ref.at[pl.ds(0, BLK)], sem).wait()


def _dispatch(pos, hn):
    grid_spec = pltpu.PrefetchScalarGridSpec(
        num_scalar_prefetch=1,
        grid=(NB,),
        in_specs=[
            pl.BlockSpec((BLK, D_MODEL), lambda i, pos: (i, 0)),
            pl.BlockSpec(memory_space=pl.ANY),
        ],
        out_specs=pl.BlockSpec(memory_space=pl.ANY),
        scratch_shapes=[pltpu.SemaphoreType.DMA(())],
    )
    return pl.pallas_call(
        _dispatch_kernel,
        grid_spec=grid_spec,
        out_shape=jax.ShapeDtypeStruct((MOE_CAP, D_MODEL), F32),
        input_output_aliases={2: 0},
        compiler_params=_cparams(("arbitrary",)),
        name="dispatch",
    )(pos, hn, jnp.zeros((MOE_CAP, D_MODEL), F32))


def _stream_expert_rows(gs_ref, gn_ref, w_hbm, w_stage, wsem, src_hbm, dst_hbm, tn, ibuf, obuf, isem, osem,
                        cast_weights, compute):
    j, e = pl.program_id(0), pl.program_id(1)
    nj, ne = pl.num_programs(0), pl.num_programs(1)
    g0 = gs_ref[e]
    n = gn_ref[e]
    cols = pl.ds(pl.multiple_of(j * tn, tn), tn)

    def w_copies(jj, ee):
        wcols = pl.ds(pl.multiple_of(jj * tn, tn), tn)
        copies = []
        for k in range(len(w_hbm)):
            chunk = w_stage[k].shape[0] // W_SPLIT
            for c in range(W_SPLIT):
                rows = pl.ds(c * chunk, chunk)
                copies.append(pltpu.make_async_copy(w_hbm[k].at[ee, rows, wcols], w_stage[k].at[rows],
                                                    wsem.at[k]))
        return copies

    def start_weights(jj, ee):
        for c, cp in enumerate(w_copies(jj, ee)):
            cp.start(priority=c % 2)

    class in_copy:
        def __init__(self, t, slot):
            chunk = MOE_TM // ROW_SPLIT
            self.copies = [
                pltpu.make_async_copy(src_hbm.at[pl.ds((g0 + t) * MOE_TM + c * chunk, chunk)],
                                      ibuf.at[slot, pl.ds(c * chunk, chunk)], isem.at[slot])
                for c in range(ROW_SPLIT)]

        def start(self, priority):
            for cp in self.copies:
                cp.start(priority=priority)

        def wait(self):
            for cp in self.copies:
                cp.wait()

    def out_copy(t, slot):
        return pltpu.make_async_copy(obuf.at[slot], dst_hbm.at[pl.ds((g0 + t) * MOE_TM, MOE_TM), cols],
                                     osem.at[slot])

    @pl.when(jnp.logical_and(j == 0, e == 0))
    def _():
        start_weights(j, e)

    @pl.when(n > 0)
    def _():
        in_copy(0, 0).start(priority=ROW_DMA_PRIORITY)

    for cp in w_copies(j, e):
        cp.wait()
    cast_weights()
    last_e = e == ne - 1

    @pl.when(jnp.logical_not(jnp.logical_and(j == nj - 1, last_e)))
    def _():
        start_weights(jnp.where(last_e, j + 1, j), jnp.where(last_e, 0, e + 1))

    def body(t, carry):
        slot = lax.rem(t, 2)
        in_copy(t, slot).wait()

        @pl.when(t + 1 < n)
        def _():
            in_copy(t + 1, 1 - slot).start(priority=ROW_DMA_PRIORITY)

        out = compute(ibuf[slot])

        @pl.when(t >= 2)
        def _():
            out_copy(t - 2, slot).wait()

        obuf[slot] = out
        out_copy(t, slot).start(priority=ROW_DMA_PRIORITY)
        return carry

    lax.fori_loop(0, n, body, 0)
    for back in (2, 1):
        @pl.when(n >= back)
        def _():
            out_copy(n - back, lax.rem(n - back, 2)).wait()

    @pl.when(last_e)
    def _():
        obuf[0] = jnp.zeros(obuf.shape[1:], obuf.dtype)

        def fill(t, carry):
            cp = pltpu.make_async_copy(obuf.at[0], dst_hbm.at[pl.ds(t * MOE_TM, MOE_TM), cols],
                                       osem.at[0])
            cp.start()
            cp.wait()
            return carry

        lax.fori_loop(g0 + n, MOE_TILES, fill, 0)


def _moe_up_kernel(gs_ref, gn_ref, wg_hbm, wu_hbm, xs_hbm, h1_hbm, wg_stage, wu_stage, wgb_ref, wub_ref,
                   xbuf, obuf, wsem, xsem, osem):
    def cast_weights():
        wgb_ref[...] = wg_stage[...].astype(BF16)
        wub_ref[...] = wu_stage[...].astype(BF16)

    def compute(x):
        xb = x.astype(BF16)
        g = jnp.dot(xb, wgb_ref[...], preferred_element_type=F32)
        u = jnp.dot(xb, wub_ref[...], preferred_element_type=F32)
        return (_silu(g) * u).astype(BF16)

    _stream_expert_rows(gs_ref, gn_ref, (wg_hbm, wu_hbm), (wg_stage, wu_stage), wsem, xs_hbm, h1_hbm,
                        MOE_TN_A, xbuf, obuf, xsem, osem, cast_weights, compute)


def _moe_up(gs, gn, xs, w_gate, w_up):
    any_spec = pl.BlockSpec(memory_space=pl.ANY)
    grid_spec = pltpu.PrefetchScalarGridSpec(
        num_scalar_prefetch=2,
        grid=(D_EXPERT // MOE_TN_A, N_EXPERTS),
        in_specs=[any_spec, any_spec, any_spec],
        out_specs=any_spec,
        scratch_shapes=[
            pltpu.VMEM((D_MODEL, MOE_TN_A), F32),
            pltpu.VMEM((D_MODEL, MOE_TN_A), F32),
            pltpu.VMEM((D_MODEL, MOE_TN_A), BF16),
            pltpu.VMEM((D_MODEL, MOE_TN_A), BF16),
            pltpu.VMEM((2, MOE_TM, D_MODEL), F32),
            pltpu.VMEM((2, MOE_TM, MOE_TN_A), BF16),
            pltpu.SemaphoreType.DMA((2,)),
            pltpu.SemaphoreType.DMA((2,)),
            pltpu.SemaphoreType.DMA((2,)),
        ],
    )
    return pl.pallas_call(
        _moe_up_kernel,
        grid_spec=grid_spec,
        out_shape=jax.ShapeDtypeStruct((MOE_CAP, D_EXPERT), BF16),
        compiler_params=_cparams(("arbitrary", "arbitrary")),
        name="moe_up",
    )(gs, gn, w_gate, w_up, xs)


def _moe_down_kernel(gs_ref, gn_ref, wd_hbm, h1_hbm, y_hbm, wd_stage, wdb_ref, hbuf, obuf,
                     wsem, hsem, osem):
    def cast_weights():
        wdb_ref[...] = wd_stage[...].astype(BF16)

    def compute(h):
        return jnp.dot(h, wdb_ref[...], preferred_element_type=F32)

    _stream_expert_rows(gs_ref, gn_ref, (wd_hbm,), (wd_stage,), wsem, h1_hbm, y_hbm, MOE_TN_B,
                        hbuf, obuf, hsem, osem, cast_weights, compute)


def _moe_down(gs, gn, h1, w_down):
    any_spec = pl.BlockSpec(memory_space=pl.ANY)
    grid_spec = pltpu.PrefetchScalarGridSpec(
        num_scalar_prefetch=2,
        grid=(D_MODEL // MOE_TN_B, N_EXPERTS),
        in_specs=[any_spec, any_spec],
        out_specs=any_spec,
        scratch_shapes=[
            pltpu.VMEM((D_EXPERT, MOE_TN_B), F32),
            pltpu.VMEM((D_EXPERT, MOE_TN_B), BF16),
            pltpu.VMEM((2, MOE_TM, D_EXPERT), BF16),
            pltpu.VMEM((2, MOE_TM, MOE_TN_B), F32),
            pltpu.SemaphoreType.DMA((1,)),
            pltpu.SemaphoreType.DMA((2,)),
            pltpu.SemaphoreType.DMA((2,)),
        ],
    )
    return pl.pallas_call(
        _moe_down_kernel,
        grid_spec=grid_spec,
        out_shape=jax.ShapeDtypeStruct((MOE_CAP, D_MODEL), F32),
        compiler_params=_cparams(("arbitrary", "arbitrary")),
        name="moe_down",
    )(gs, gn, w_down, h1)


def _combine_kernel(pos_ref, info_ref, g_ref, h_hbm, y_hbm, o_ref, hbuf, ybuf, sem_h, sem_y):
    i = pl.program_id(0)
    slot = lax.rem(i, 2)

    def h_copy(t, s):
        return pltpu.make_async_copy(h_hbm.at[pl.ds(BLK + t * OUT_T, OUT_T)], hbuf.at[s], sem_h.at[s])

    def fetch(t, s):
        h_copy(t, s).start()

        def issue(r, carry):
            base = (t * OUT_T + r) * 2
            for k in range(2):
                pltpu.make_async_copy(y_hbm.at[pl.ds(pos_ref[base + k], 1)],
                                      ybuf.at[s, k, pl.ds(r, 1)], sem_y.at[s]).start()
            return carry

        lax.fori_loop(0, OUT_T, issue, 0, unroll=4)

    @pl.when(i == 0)
    def _():
        fetch(0, 0)

    @pl.when(i + 1 < pl.num_programs(0))
    def _():
        fetch(i + 1, 1 - slot)

    h_copy(i, slot).wait()
    for k in range(2):
        pltpu.make_async_copy(y_hbm.at[pl.ds(0, OUT_T)], ybuf.at[slot, k], sem_y.at[slot]).wait()
    g1 = info_ref[:, INFO_G1:INFO_G1 + 1]
    g2 = info_ref[:, INFO_G2:INFO_G2 + 1]
    h_new = hbuf[slot] + g1 * ybuf[slot, 0] + g2 * ybuf[slot, 1]
    o_ref[...] = _rms(h_new, g_ref[...])


def _combine(pos_x, info_x, g_final, h, y):
    grid_spec = pltpu.PrefetchScalarGridSpec(
        num_scalar_prefetch=1,
        grid=(SEQ // OUT_T,),
        in_specs=[
            pl.BlockSpec((OUT_T, LANE), lambda i, pos: (i, 0)),
            pl.BlockSpec((1, D_MODEL), lambda i, pos: (0, 0)),
            pl.BlockSpec(memory_space=pl.ANY),
            pl.BlockSpec(memory_space=pl.ANY),
        ],
        out_specs=pl.BlockSpec((OUT_T, D_MODEL), lambda i, pos: (i, 0)),
        scratch_shapes=[
            pltpu.VMEM((2, OUT_T, D_MODEL), F32),
            pltpu.VMEM((2, 2, OUT_T, D_MODEL), F32),
            pltpu.SemaphoreType.DMA((2,)),
            pltpu.SemaphoreType.DMA((2,)),
        ],
    )
    return pl.pallas_call(
        _combine_kernel,
        grid_spec=grid_spec,
        out_shape=jax.ShapeDtypeStruct((SEQ, D_MODEL), F32),
        compiler_params=_cparams(("arbitrary",)),
        name="combine",
    )(pos_x, info_x, g_final, h, y)


def _mixer(h, hn, layer, p, g_next, hn_dtype):
    w_in = p["w_in"][layer]
    a0 = 3 * ATTN_WIDTH
    u0 = a0 + N_HEADS
    w_qkv = w_in[:, :a0].astype(BF16)
    w_f = jnp.pad(w_in[:, a0:u0], ((0, 0), (0, LANE - N_HEADS))).astype(BF16)
    w_a = w_in[:, u0:u0 + CONV_WIDTH].astype(BF16)
    w_g = w_in[:, u0 + CONV_WIDTH:].astype(BF16)
    b_f = jnp.pad(p["b_forget"][layer], (0, LANE - N_HEADS)).reshape(1, LANE)

    qkv = _qkv(hn, w_qkv)
    key_bias = _decay(hn, w_f, b_f)
    attn = _attention(qkv, key_bias, p["attn_out_g"][layer].reshape(N_HEADS, 1, HEAD_DIM))

    hc = _glu(hn, w_a, w_g)
    w_dw = jnp.pad(p["w_dw"][layer], ((0, HALO - CONV_KERNEL), (0, 0)))
    row = lambda v: v.reshape(1, -1)
    conv = _conv(hc, w_dw, row(p["b_dw"][layer]), row(p["conv_ln_g"][layer]),
                 row(p["conv_ln_b"][layer]), p["w_conv_out"][layer].astype(BF16),
                 row(p["conv_out_g"][layer]))
    return _outproj(attn, conv, p["w_out"][layer].astype(BF16), h, g_next, hn_dtype)


def _column_blocks(w, tn):
    k, n = w.shape
    return w.astype(BF16).reshape(k, n // tn, tn).transpose(1, 0, 2)


def _moe(h, hn, p, j, g_final):
    w_r = jnp.pad(p["moe_w_router"][j], ((0, 0), (0, LANE - N_EXPERTS)))
    info, cnt = _router(hn, w_r)
    e1 = info[:, INFO_E1].astype(jnp.int32)
    e2 = info[:, INFO_E2].astype(jnp.int32)
    r1 = info[:, INFO_R1].astype(jnp.int32)
    r2 = info[:, INFO_R2].astype(jnp.int32)
    counts = cnt[0, :N_EXPERTS].astype(jnp.int32)
    padded = (counts + MOE_TM - 1) // MOE_TM * MOE_TM
    gend = jnp.cumsum(padded)
    gstart = gend - padded
    pos = jnp.stack([gstart[e1] + r1, gstart[e2] + r2], axis=-1).reshape(-1)
    gs = (gstart // MOE_TM).astype(jnp.int32)
    gn = (padded // MOE_TM).astype(jnp.int32)

    xs = _dispatch(pos, hn)
    h1 = _moe_up(gs, gn, xs, p["moe_w_gate"][j], p["moe_w_up"][j])
    y = _moe_down(gs, gn, h1, p["moe_w_down"][j])
    return _combine(pos[2 * BLK:], info[BLK:], g_final, h, y)


def kernel(x, meta_tokens, mix_norm_g, ffn_norm_g, w_in, b_forget, w_dw, b_dw, conv_ln_g, conv_ln_b, w_conv_out, attn_out_g, conv_out_g, w_out, dense_w_gate, dense_w_up, dense_w_down, moe_w_router, moe_w_gate, moe_w_up, moe_w_down, final_norm_g):
    assert x.shape == (1, SEQ, D_MODEL) and meta_tokens.shape == (N_META, D_MODEL)
    p = dict(w_in=w_in, b_forget=b_forget, w_dw=w_dw, b_dw=b_dw, conv_ln_g=conv_ln_g,
             conv_ln_b=conv_ln_b, w_conv_out=w_conv_out, attn_out_g=attn_out_g,
             conv_out_g=conv_out_g, w_out=w_out, moe_w_router=moe_w_router,
             moe_w_gate=moe_w_gate, moe_w_up=moe_w_up, moe_w_down=moe_w_down)
    row = lambda v: v.reshape(1, D_MODEL)
    meta_blk = jnp.pad(meta_tokens.astype(F32), ((PAD, 0), (0, 0)))
    h, hn = _prep(meta_blk, x.reshape(SEQ, D_MODEL), row(mix_norm_g[0]))

    h, hn = _mixer(h, hn, 0, p, row(ffn_norm_g[0]), BF16)
    h, hn = _ffn(hn, _column_blocks(dense_w_gate[0], FFN_TF), _column_blocks(dense_w_up[0], FFN_TF),
                 dense_w_down[0].astype(BF16), h, row(mix_norm_g[1]))
    h, hn = _mixer(h, hn, 1, p, row(ffn_norm_g[1]), F32)
    out = _moe(h, hn, p, 0, row(final_norm_g))
    return out.reshape(1, SEQ, D_MODEL)
```

```python
import functools

import jax
import jax.numpy as jnp
from jax import lax
from jax.experimental import pallas as pl
from jax.experimental.pallas import tpu as pltpu

F32 = jnp.float32
BF16 = jnp.bfloat16

D_MODEL = 2048
SEQ = 8192
N_META = 16
N_HEADS = 8
HEAD_DIM = 128
ATTN_WIDTH = N_HEADS * HEAD_DIM
CONV_WIDTH = 1024
CONV_GROUPS = 8
CONV_KERNEL = 31
D_FF = 5632
N_EXPERTS = 8
D_EXPERT = 7168
RMS_EPS = 1e-6
LN_EPS = 1e-5
MASK_VALUE = -1e30
LOG2E = 1.4426950408889634

LANE = 128
SUBLANES = 8
BLK = 128
PAD = BLK - N_META
LP = PAD + N_META + SEQ
TM = 640
NT = LP // TM
NB = LP // BLK
HALO = 32
MOE_TM = 256
MOE_TILES = -(-(2 * LP + N_EXPERTS * (MOE_TM - 1)) // MOE_TM)
MOE_CAP = MOE_TILES * MOE_TM
MOE_TN_A = 1024
MOE_TN_B = 512
FFN_TF = 512
OUT_T = 512
VMEM_LIMIT = 56 * 1024 * 1024
HEADS_PER_STEP = 2
ATTN_HW = HEADS_PER_STEP * HEAD_DIM
ATTN_GROUPS = N_HEADS // HEADS_PER_STEP
ROW_DMA_PRIORITY = 1
W_SPLIT = 8
ROW_SPLIT = 4


def _cparams(sem):
    return pltpu.CompilerParams(dimension_semantics=sem, vmem_limit_bytes=VMEM_LIMIT)


def _rms(x, g):
    return x * lax.rsqrt(jnp.mean(x * x, axis=-1, keepdims=True) + RMS_EPS) * g


def _silu(x):
    return x * jax.nn.sigmoid(x)


def _prep_kernel(meta_ref, x_ref, g_ref, h_ref, hn_ref):
    i = pl.program_id(0)
    v = jnp.where(i == 0, meta_ref[...], x_ref[...])
    h_ref[...] = v
    hn_ref[...] = _rms(v, g_ref[...]).astype(BF16)


def _prep(meta_blk, x2d, g):
    return pl.pallas_call(
        _prep_kernel,
        grid=(NB,),
        in_specs=[
            pl.BlockSpec((BLK, D_MODEL), lambda i: (0, 0)),
            pl.BlockSpec((BLK, D_MODEL), lambda i: (jnp.maximum(i - 1, 0), 0)),
            pl.BlockSpec((1, D_MODEL), lambda i: (0, 0)),
        ],
        out_specs=[
            pl.BlockSpec((BLK, D_MODEL), lambda i: (i, 0)),
            pl.BlockSpec((BLK, D_MODEL), lambda i: (i, 0)),
        ],
        out_shape=[
            jax.ShapeDtypeStruct((LP, D_MODEL), F32),
            jax.ShapeDtypeStruct((LP, D_MODEL), BF16),
        ],
        compiler_params=_cparams(("arbitrary",)),
        name="prep",
    )(meta_blk, x2d, g)


def _qkv_kernel(x_ref, w_ref, o_ref):
    j = pl.program_id(0)
    acc = jnp.dot(x_ref[...], w_ref[...], preferred_element_type=F32)
    scale = jnp.where(j == 0, HEAD_DIM ** -0.5 * LOG2E, 1.0).astype(F32)
    res = (acc * scale).astype(BF16)
    for s in range(o_ref.shape[0]):
        o_ref[s] = res[:, s * ATTN_HW:(s + 1) * ATTN_HW]


def _qkv(hn, w_qkv):
    return pl.pallas_call(
        _qkv_kernel,
        grid=(3, NT),
        in_specs=[
            pl.BlockSpec((TM, D_MODEL), lambda j, i: (i, 0)),
            pl.BlockSpec((D_MODEL, ATTN_WIDTH), lambda j, i: (0, j)),
        ],
        out_specs=pl.BlockSpec((ATTN_GROUPS, TM, ATTN_HW), lambda j, i: (j, i, 0)),
        out_shape=jax.ShapeDtypeStruct((3 * ATTN_GROUPS, LP, ATTN_HW), BF16),
        compiler_params=_cparams(("arbitrary", "arbitrary")),
        name="qkv",
    )(hn, w_qkv)


def _glu_kernel(x_ref, wa_ref, wg_ref, o_ref):
    i = pl.program_id(0)
    x = x_ref[...]
    a = jnp.dot(x, wa_ref[...], preferred_element_type=F32)
    g = jnp.dot(x, wg_ref[...], preferred_element_type=F32)
    row = lax.broadcasted_iota(jnp.int32, a.shape, 0) + i * TM
    o_ref[...] = jnp.where(row >= PAD, a * jax.nn.sigmoid(g), 0.0)


def _glu(hn, w_a, w_g):
    return pl.pallas_call(
        _glu_kernel,
        grid=(NT,),
        in_specs=[
            pl.BlockSpec((TM, D_MODEL), lambda i: (i, 0)),
            pl.BlockSpec((D_MODEL, CONV_WIDTH), lambda i: (0, 0)),
            pl.BlockSpec((D_MODEL, CONV_WIDTH), lambda i: (0, 0)),
        ],
        out_specs=pl.BlockSpec((TM, CONV_WIDTH), lambda i: (i, 0)),
        out_shape=jax.ShapeDtypeStruct((LP, CONV_WIDTH), F32),
        compiler_params=_cparams(("arbitrary",)),
        name="glu",
    )(hn, w_a, w_g)


def _split3(x):
    hi = x.astype(BF16)
    r1 = x - hi.astype(F32)
    mid = r1.astype(BF16)
    lo = (r1 - mid.astype(F32)).astype(BF16)
    return hi, mid, lo


def _decay_kernel(x_ref, w_ref, b_ref, kb_ref, carry_ref):
    i = pl.program_id(0)

    @pl.when(i == 0)
    def _():
        carry_ref[...] = jnp.zeros_like(carry_ref)

    f = jnp.dot(x_ref[...], w_ref[...], preferred_element_type=F32) + b_ref[...]
    log_f = jnp.minimum(f, 0.0) - jnp.log1p(jnp.exp(-jnp.abs(f)))
    row = lax.broadcasted_iota(jnp.int32, log_f.shape, 0) + i * TM
    log_f = jnp.where(row >= PAD, log_f, 0.0)
    r = lax.broadcasted_iota(jnp.int32, (TM, TM), 0)
    c = lax.broadcasted_iota(jnp.int32, (TM, TM), 1)
    tri = jnp.where(c <= r, 1.0, 0.0).astype(BF16)
    hi, mid, lo = _split3(log_f)
    cs = (jnp.dot(tri, hi, preferred_element_type=F32)
          + jnp.dot(tri, mid, preferred_element_type=F32)
          + jnp.dot(tri, lo, preferred_element_type=F32)) + carry_ref[0:1, :]
    carry_ref[...] = jnp.broadcast_to(cs[TM - 1:TM, :], carry_ref.shape)
    lane = lax.broadcasted_iota(jnp.int32, (TM, LANE), 1)
    for h in range(N_HEADS):
        bias = jnp.where(row[:, h:h + 1] >= PAD, -LOG2E * cs[:, h:h + 1], MASK_VALUE)
        hi, mid, lo = (v.astype(F32) for v in _split3(jnp.broadcast_to(bias, (TM, LANE))))
        pieces = jnp.where(lane == 0, hi, jnp.where(lane == 1, mid, jnp.where(lane == 2, lo, 0.0)))
        kb_ref[h] = pieces.astype(BF16)


def _decay(hn, w_f, b_f):
    return pl.pallas_call(
        _decay_kernel,
        grid=(NT,),
        in_specs=[
            pl.BlockSpec((TM, D_MODEL), lambda i: (i, 0)),
            pl.BlockSpec((D_MODEL, LANE), lambda i: (0, 0)),
            pl.BlockSpec((1, LANE), lambda i: (0, 0)),
        ],
        out_specs=pl.BlockSpec((N_HEADS, TM, LANE), lambda i: (0, i, 0)),
        out_shape=jax.ShapeDtypeStruct((N_HEADS, LP, LANE), BF16),
        scratch_shapes=[pltpu.VMEM((8, LANE), F32)],
        compiler_params=_cparams(("arbitrary",)),
        name="decay",
    )(hn, w_f, b_f)


ATTN_STRIP = 16
ATTN_WIDE = 2 * TM
ATTN_PART = (512, ATTN_WIDE - 512)
assert ATTN_PART[1] >= TM and all(w % 256 == 0 for w in ATTN_PART)


def _attn_kernel(q_ref, k_ref, v_ref, kb_ref, g_ref, o_ref, *state_refs):
    i = pl.program_id(1)
    n_state = len(state_refs) // HEADS_PER_STEP
    heads = []
    for h in range(HEADS_PER_STEP):
        m_ref, l_ref, acc_ref, s0, s1, p0, p1 = state_refs[n_state * h:n_state * (h + 1)]
        heads.append((m_ref, l_ref, acc_ref, (s0, s1), (p0, p1)))
    for m_ref, l_ref, acc_ref, _, _ in heads:
        m_ref[...] = jnp.full_like(m_ref, -1e38)
        l_ref[...] = jnp.zeros_like(l_ref)
        acc_ref[...] = jnp.zeros_like(acc_ref)

    lane = lax.broadcasted_iota(jnp.int32, (TM, HEAD_DIM), 1)
    q_ones = jnp.where(lane < 3, 1.0, 0.0).astype(BF16)

    def lanes(h):
        return slice(h * HEAD_DIM, (h + 1) * HEAD_DIM)

    def key_rows(key0, width):
        return pl.ds(pl.multiple_of(key0, LANE), width)

    def qk(h, key0, width, part):
        s_ref = heads[h][3][part]
        keys = key_rows(key0, width)
        q_aug = jnp.concatenate([q_ref[0, :, lanes(h)], q_ones], axis=1)
        k_aug = jnp.concatenate([k_ref[0, keys, lanes(h)], kb_ref[h, keys, :]], axis=1)
        s_ref[:, :width] = lax.dot_general(q_aug, k_aug, (((1,), (1,)), ((), ())),
                                           preferred_element_type=F32)

    def softmax_pv(h, key0, width, part, diagonal):
        m_ref, l_ref, acc_ref, s_refs, p_refs = heads[h]
        s_ref, p_ref = s_refs[part], p_refs[part]
        for r in range(TM // ATTN_STRIP):
            rs = slice(r * ATTN_STRIP, (r + 1) * ATTN_STRIP)
            nc = min(width, -(-((r + 1) * ATTN_STRIP) // LANE) * LANE) if diagonal else width

            def logits():
                s = s_ref[rs, :nc]
                if diagonal:
                    row = lax.broadcasted_iota(jnp.int32, s.shape, 0) + r * ATTN_STRIP
                    col = lax.broadcasted_iota(jnp.int32, s.shape, 1)
                    s = jnp.where(col <= row, s, MASK_VALUE)
                return s

            m_prev = m_ref[rs, :]
            m_new = jnp.maximum(m_prev, jnp.max(logits(), axis=-1, keepdims=True))
            p = jnp.exp2(logits() - jnp.concatenate([m_new] * (nc // LANE), axis=1))
            p_ref[rs, :nc] = p.astype(BF16)
            if nc < width:
                p_ref[rs, nc:width] = jnp.zeros((ATTN_STRIP, width - nc), BF16)
            alpha = jnp.exp2(m_prev - m_new)
            l_ref[rs, :] = alpha * l_ref[rs, :] + jnp.sum(p, axis=-1, keepdims=True)
            m_ref[rs, :] = m_new
            acc_ref[rs, :] = alpha * acc_ref[rs, :]
        acc_ref[...] += jnp.dot(p_ref[:, :width], v_ref[0, key_rows(key0, width), lanes(h)],
                                preferred_element_type=F32)

    def block(parts, diagonal=False):
        for key0, width, part in parts:
            for h in range(HEADS_PER_STEP):
                qk(h, key0, width, part)
        for key0, width, part in parts:
            for h in range(HEADS_PER_STEP):
                softmax_pv(h, key0, width, part, diagonal)

    def wide_block(j, carry):
        k0 = j * ATTN_WIDE
        block(((k0, ATTN_PART[0], 0), (k0 + ATTN_PART[0], ATTN_PART[1], 1)))
        return carry

    lax.fori_loop(0, i // 2, wide_block, 0)

    @pl.when(i % 2 == 1)
    def _():
        block((((i - 1) * TM, TM, 1),))

    block(((i * TM, TM, 1),), diagonal=True)
    for h, (_, l_ref, acc_ref, _, _) in enumerate(heads):
        hs = slice(h * HEAD_DIM, (h + 1) * HEAD_DIM)
        o = acc_ref[...] / l_ref[...]
        o_ref[:, hs] = _rms(o, g_ref[h]).astype(BF16)


def _attention(qkv, key_bias, g_attn):
    hw = ATTN_HW
    nhp = ATTN_GROUPS
    return pl.pallas_call(
        _attn_kernel,
        grid=(nhp, NT),
        in_specs=[
            pl.BlockSpec((1, TM, hw), lambda hp, i: (hp, i, 0)),
            pl.BlockSpec((1, LP, hw), lambda hp, i: (nhp + hp, 0, 0)),
            pl.BlockSpec((1, LP, hw), lambda hp, i: (2 * nhp + hp, 0, 0)),
            pl.BlockSpec((HEADS_PER_STEP, LP, LANE), lambda hp, i: (hp, 0, 0)),
            pl.BlockSpec((HEADS_PER_STEP, 1, HEAD_DIM), lambda hp, i: (hp, 0, 0)),
        ],
        out_specs=pl.BlockSpec((TM, hw), lambda hp, i: (i, hp)),
        out_shape=jax.ShapeDtypeStruct((LP, ATTN_WIDTH), BF16),
        scratch_shapes=[
            pltpu.VMEM((TM, HEAD_DIM), F32),
            pltpu.VMEM((TM, HEAD_DIM), F32),
            pltpu.VMEM((TM, HEAD_DIM), F32),
            pltpu.VMEM((TM, ATTN_PART[0]), F32),
            pltpu.VMEM((TM, ATTN_PART[1]), F32),
            pltpu.VMEM((TM, ATTN_PART[0]), BF16),
            pltpu.VMEM((TM, ATTN_PART[1]), BF16),
        ] * HEADS_PER_STEP,
        compiler_params=_cparams(("arbitrary", "arbitrary")),
        name="attention",
    )(qkv, qkv, qkv, key_bias, g_attn)


CONV_RC = 64
CONV_CC = 512
CONV_ROWS = HALO + TM - SUBLANES


def _conv_kernel(x_ref, halo_ref, wdw_ref, bdw_ref, lng_ref, lnb_ref, wpw_ref, og_ref, o_ref,
                 xs_ref, cv_ref):
    i = pl.program_id(0)
    xs_ref[0, 0:HALO, :] = jnp.where(i == 0, 0.0, halo_ref[...])
    xs_ref[0, HALO:HALO + TM, :] = x_ref[...]
    for s in range(1, SUBLANES):
        xs_ref[s, 0:CONV_ROWS, :] = xs_ref[0, s:s + CONV_ROWS, :]
    off = HALO - (CONV_KERNEL - 1)
    for rc in range(TM // CONV_RC):
        for cc in range(CONV_WIDTH // CONV_CC):
            cs = slice(cc * CONV_CC, (cc + 1) * CONV_CC)
            acc = jnp.broadcast_to(bdw_ref[:, cs], (CONV_RC, CONV_CC))
            for j in range(CONV_KERNEL):
                s, a = (off + j) % SUBLANES, (off + j) // SUBLANES * SUBLANES
                r0 = rc * CONV_RC + a
                acc = acc + wdw_ref[j:j + 1, cs] * xs_ref[s, r0:r0 + CONV_RC, cs]
            cv_ref[rc * CONV_RC:(rc + 1) * CONV_RC, cs] = acc
    y = cv_ref[...]
    mu = jnp.mean(y, axis=-1, keepdims=True)
    yc = y - mu
    var = jnp.mean(yc * yc, axis=-1, keepdims=True)
    a = _silu(yc * lax.rsqrt(var + LN_EPS) * lng_ref[...] + lnb_ref[...])
    z = jnp.dot(a.astype(BF16), wpw_ref[...], preferred_element_type=F32)
    gd = CONV_WIDTH // CONV_GROUPS
    for g in range(CONV_GROUPS):
        gs = slice(g * gd, (g + 1) * gd)
        o_ref[:, gs] = _rms(z[:, gs], og_ref[:, gs]).astype(BF16)


def _conv(hc, w_dw, b_dw, ln_g, ln_b, w_pw, out_g):
    row = lambda i: (0, 0)
    return pl.pallas_call(
        _conv_kernel,
        grid=(NT,),
        in_specs=[
            pl.BlockSpec((TM, CONV_WIDTH), lambda i: (i, 0)),
            pl.BlockSpec((HALO, CONV_WIDTH), lambda i: (jnp.maximum(i * (TM // HALO) - 1, 0), 0)),
            pl.BlockSpec((HALO, CONV_WIDTH), row),
            pl.BlockSpec((1, CONV_WIDTH), row),
            pl.BlockSpec((1, CONV_WIDTH), row),
            pl.BlockSpec((1, CONV_WIDTH), row),
            pl.BlockSpec((CONV_WIDTH, CONV_WIDTH), row),
            pl.BlockSpec((1, CONV_WIDTH), row),
        ],
        out_specs=pl.BlockSpec((TM, CONV_WIDTH), lambda i: (i, 0)),
        out_shape=jax.ShapeDtypeStruct((LP, CONV_WIDTH), BF16),
        scratch_shapes=[
            pltpu.VMEM((SUBLANES, HALO + TM, CONV_WIDTH), F32),
            pltpu.VMEM((TM, CONV_WIDTH), F32),
        ],
        compiler_params=_cparams(("arbitrary",)),
        name="conv",
    )(hc, hc, w_dw, b_dw, ln_g, ln_b, w_pw, out_g)


def _outproj_kernel(a_ref, c_ref, w_ref, h_ref, g_ref, ho_ref, hn_ref):
    acc = jnp.dot(a_ref[...], w_ref[0:ATTN_WIDTH, :], preferred_element_type=F32)
    acc = acc + jnp.dot(c_ref[...], w_ref[ATTN_WIDTH:, :], preferred_element_type=F32)
    h_new = h_ref[...] + acc
    ho_ref[...] = h_new
    hn_ref[...] = _rms(h_new, g_ref[...]).astype(hn_ref.dtype)


def _outproj(attn, conv, w_out, h, g_next, hn_dtype):
    return pl.pallas_call(
        _outproj_kernel,
        grid=(NT,),
        in_specs=[
            pl.BlockSpec((TM, ATTN_WIDTH), lambda i: (i, 0)),
            pl.BlockSpec((TM, CONV_WIDTH), lambda i: (i, 0)),
            pl.BlockSpec((D_MODEL, D_MODEL), lambda i: (0, 0)),
            pl.BlockSpec((TM, D_MODEL), lambda i: (i, 0)),
            pl.BlockSpec((1, D_MODEL), lambda i: (0, 0)),
        ],
        out_specs=[
            pl.BlockSpec((TM, D_MODEL), lambda i: (i, 0)),
            pl.BlockSpec((TM, D_MODEL), lambda i: (i, 0)),
        ],
        out_shape=[
            jax.ShapeDtypeStruct((LP, D_MODEL), F32),
            jax.ShapeDtypeStruct((LP, D_MODEL), hn_dtype),
        ],
        compiler_params=_cparams(("arbitrary",)),
        name="outproj",
    )(attn, conv, w_out, h, g_next)


def _ffn_kernel(x_ref, wg_ref, wu_ref, wd_ref, h_ref, g_ref, ho_ref, hn_ref, acc_ref):
    j = pl.program_id(1)

    @pl.when(j == 0)
    def _():
        acc_ref[...] = jnp.zeros_like(acc_ref)

    x = x_ref[...]
    g = jnp.dot(x, wg_ref[...], preferred_element_type=F32)
    u = jnp.dot(x, wu_ref[...], preferred_element_type=F32)
    a = (_silu(g) * u).astype(BF16)
    acc_ref[...] += jnp.dot(a, wd_ref[...], preferred_element_type=F32)

    @pl.when(j == pl.num_programs(1) - 1)
    def _():
        h_new = h_ref[...] + acc_ref[...]
        ho_ref[...] = h_new
        hn_ref[...] = _rms(h_new, g_ref[...]).astype(BF16)


def _ffn(hn, w_gate, w_up, w_down, h, g_next):
    return pl.pallas_call(
        _ffn_kernel,
        grid=(NT, D_FF // FFN_TF),
        in_specs=[
            pl.BlockSpec((TM, D_MODEL), lambda i, j: (i, 0)),
            pl.BlockSpec((D_MODEL, FFN_TF), lambda i, j: (0, j)),
            pl.BlockSpec((D_MODEL, FFN_TF), lambda i, j: (0, j)),
            pl.BlockSpec((FFN_TF, D_MODEL), lambda i, j: (j, 0)),
            pl.BlockSpec((TM, D_MODEL), lambda i, j: (i, 0)),
            pl.BlockSpec((1, D_MODEL), lambda i, j: (0, 0)),
        ],
        out_specs=[
            pl.BlockSpec((TM, D_MODEL), lambda i, j: (i, 0)),
            pl.BlockSpec((TM, D_MODEL), lambda i, j: (i, 0)),
        ],
        out_shape=[
            jax.ShapeDtypeStruct((LP, D_MODEL), F32),
            jax.ShapeDtypeStruct((LP, D_MODEL), BF16),
        ],
        scratch_shapes=[pltpu.VMEM((TM, D_MODEL), F32)],
        compiler_params=_cparams(("arbitrary", "arbitrary")),
        name="ffn",
    )(hn, w_gate, w_up, w_down, h, g_next)


INFO_E1, INFO_E2, INFO_G1, INFO_G2, INFO_R1, INFO_R2 = range(6)


def _router_kernel(x_ref, w_ref, info_ref, cnt_ref, carry_ref):
    i = pl.program_id(0)

    @pl.when(i == 0)
    def _():
        carry_ref[...] = jnp.zeros_like(carry_ref)

    logits = jnp.dot(x_ref[...], w_ref[...], preferred_element_type=F32,
                     precision=lax.Precision.HIGHEST)
    lane = lax.broadcasted_iota(jnp.int32, logits.shape, 1)
    neg = jnp.float32(-3e38)
    l1 = jnp.where(lane < N_EXPERTS, logits, neg)
    m1 = jnp.max(l1, axis=-1, keepdims=True)
    i1 = jnp.min(jnp.where(l1 == m1, lane, LANE), axis=-1, keepdims=True)
    l2 = jnp.where(lane == i1, neg, l1)
    m2 = jnp.max(l2, axis=-1, keepdims=True)
    i2 = jnp.min(jnp.where(l2 == m2, lane, LANE), axis=-1, keepdims=True)
    ex = jnp.exp(m2 - m1)
    g1 = 1.0 / (1.0 + ex)
    g2 = ex / (1.0 + ex)
    sel = jnp.where(lane == i1, 1.0, jnp.where(lane == i2, 1.0, 0.0))
    r = lax.broadcasted_iota(jnp.int32, (TM, TM), 0)
    c = lax.broadcasted_iota(jnp.int32, (TM, TM), 1)
    tri = jnp.where(c < r, 1.0, 0.0).astype(BF16)
    carry = carry_ref[0:1, :]
    rank = jnp.dot(tri, sel.astype(BF16), preferred_element_type=F32) + carry
    r1 = jnp.sum(jnp.where(lane == i1, rank, 0.0), axis=-1, keepdims=True)
    r2 = jnp.sum(jnp.where(lane == i2, rank, 0.0), axis=-1, keepdims=True)
    info = jnp.zeros_like(logits)
    for k, v in ((INFO_E1, i1.astype(F32)), (INFO_E2, i2.astype(F32)), (INFO_G1, g1),
                 (INFO_G2, g2), (INFO_R1, r1), (INFO_R2, r2)):
        info = jnp.where(lane == k, v, info)
    info_ref[...] = info
    total = carry + jnp.sum(sel, axis=0, keepdims=True)
    carry_ref[...] = jnp.broadcast_to(total, carry_ref.shape)
    cnt_ref[...] = jnp.broadcast_to(total, cnt_ref.shape)


def _router(hn, w_r):
    return pl.pallas_call(
        _router_kernel,
        grid=(NT,),
        in_specs=[
            pl.BlockSpec((TM, D_MODEL), lambda i: (i, 0)),
            pl.BlockSpec((D_MODEL, LANE), lambda i: (0, 0)),
        ],
        out_specs=[
            pl.BlockSpec((TM, LANE), lambda i: (i, 0)),
            pl.BlockSpec((8, LANE), lambda i: (0, 0)),
        ],
        out_shape=[
            jax.ShapeDtypeStruct((LP, LANE), F32),
            jax.ShapeDtypeStruct((8, LANE), F32),
        ],
        scratch_shapes=[pltpu.VMEM((8, LANE), F32)],
        compiler_params=_cparams(("arbitrary",)),
        name="router",
    )(hn, w_r)


def _dispatch_kernel(pos_ref, x_ref, xs_in_ref, xs_ref, sem):
    del xs_in_ref
    i = pl.program_id(0)

    def issue(r, carry):
        base = (i * BLK + r) * 2
        for k in range(2):
            pltpu.make_async_copy(x_ref.at[pl.ds(r, 1)],
                                  xs_ref.at[pl.ds(pos_ref[base + k], 1)], sem).start()
        return carry

    lax.fori_loop(0, BLK, issue, 0, unroll=4)
    for _ in range(2):
        pltpu.make_async_copy(x_ref, xs_ref.at[pl.ds(0, BLK)], sem).wait()


def _dispatch(pos, hn):
    grid_spec = pltpu.PrefetchScalarGridSpec(
        num_scalar_prefetch=1,
        grid=(NB,),
        in_specs=[
            pl.BlockSpec((BLK, D_MODEL), lambda i, pos: (i, 0)),
            pl.BlockSpec(memory_space=pl.ANY),
        ],
        out_specs=pl.BlockSpec(memory_space=pl.ANY),
        scratch_shapes=[pltpu.SemaphoreType.DMA(())],
    )
    return pl.pallas_call(
        _dispatch_kernel,
        grid_spec=grid_spec,
        out_shape=jax.ShapeDtypeStruct((MOE_CAP, D_MODEL), F32),
        input_output_aliases={2: 0},
        compiler_params=_cparams(("arbitrary",)),
        name="dispatch",
    )(pos, hn, jnp.zeros((MOE_CAP, D_MODEL), F32))


def _stream_expert_rows(gs_ref, gn_ref, w_hbm, w_stage, wsem, src_hbm, dst_hbm, tn, ibuf, obuf, isem, osem,
                        cast_weights, compute):
    j, e = pl.program_id(0), pl.program_id(1)
    nj, ne = pl.num_programs(0), pl.num_programs(1)
    g0 = gs_ref[e]
    n = gn_ref[e]
    cols = pl.ds(pl.multiple_of(j * tn, tn), tn)

    def w_copies(jj, ee):
        wcols = pl.ds(pl.multiple_of(jj * tn, tn), tn)
        copies = []
        for k in range(len(w_hbm)):
            chunk = w_stage[k].shape[0] // W_SPLIT
            for c in range(W_SPLIT):
                rows = pl.ds(c * chunk, chunk)
                copies.append(pltpu.make_async_copy(w_hbm[k].at[ee, rows, wcols], w_stage[k].at[rows],
                                                    wsem.at[k]))
        return copies

    def start_weights(jj, ee):
        for c, cp in enumerate(w_copies(jj, ee)):
            cp.start(priority=c % 2)

    class in_copy:
        def __init__(self, t, slot):
            chunk = MOE_TM // ROW_SPLIT
            self.copies = [
                pltpu.make_async_copy(src_hbm.at[pl.ds((g0 + t) * MOE_TM + c * chunk, chunk)],
                                      ibuf.at[slot, pl.ds(c * chunk, chunk)], isem.at[slot])
                for c in range(ROW_SPLIT)]

        def start(self, priority):
            for cp in self.copies:
                cp.start(priority=priority)

        def wait(self):
            for cp in self.copies:
                cp.wait()

    def out_copy(t, slot):
        return pltpu.make_async_copy(obuf.at[slot], dst_hbm.at[pl.ds((g0 + t) * MOE_TM, MOE_TM), cols],
                                     osem.at[slot])

    @pl.when(jnp.logical_and(j == 0, e == 0))
    def _():
        start_weights(j, e)

    @pl.when(n > 0)
    def _():
        in_copy(0, 0).start(priority=ROW_DMA_PRIORITY)

    for cp in w_copies(j, e):
        cp.wait()
    cast_weights()
    last_e = e == ne - 1

    @pl.when(jnp.logical_not(jnp.logical_and(j == nj - 1, last_e)))
    def _():
        start_weights(jnp.where(last_e, j + 1, j), jnp.where(last_e, 0, e + 1))

    def body(t, carry):
        slot = lax.rem(t, 2)
        in_copy(t, slot).wait()

        @pl.when(t + 1 < n)
        def _():
            in_copy(t + 1, 1 - slot).start(priority=ROW_DMA_PRIORITY)

        out = compute(ibuf[slot])

        @pl.when(t >= 2)
        def _():
            out_copy(t - 2, slot).wait()

        obuf[slot] = out
        out_copy(t, slot).start(priority=ROW_DMA_PRIORITY)
        return carry

    lax.fori_loop(0, n, body, 0)
    for back in (2, 1):
        @pl.when(n >= back)
        def _():
            out_copy(n - back, lax.rem(n - back, 2)).wait()

    @pl.when(last_e)
    def _():
        obuf[0] = jnp.zeros(obuf.shape[1:], obuf.dtype)

        def fill(t, carry):
            cp = pltpu.make_async_copy(obuf.at[0], dst_hbm.at[pl.ds(t * MOE_TM, MOE_TM), cols],
                                       osem.at[0])
            cp.start()
            cp.wait()
            return carry

        lax.fori_loop(g0 + n, MOE_TILES, fill, 0)


def _moe_up_kernel(gs_ref, gn_ref, wg_hbm, wu_hbm, xs_hbm, h1_hbm, wg_stage, wu_stage, wgb_ref, wub_ref,
                   xbuf, obuf, wsem, xsem, osem):
    def cast_weights():
        wgb_ref[...] = wg_stage[...].astype(BF16)
        wub_ref[...] = wu_stage[...].astype(BF16)

    def compute(x):
        xb = x.astype(BF16)
        g = jnp.dot(xb, wgb_ref[...], preferred_element_type=F32)
        u = jnp.dot(xb, wub_ref[...], preferred_element_type=F32)
        return (_silu(g) * u).astype(BF16)

    _stream_expert_rows(gs_ref, gn_ref, (wg_hbm, wu_hbm), (wg_stage, wu_stage), wsem, xs_hbm, h1_hbm,
                        MOE_TN_A, xbuf, obuf, xsem, osem, cast_weights, compute)


def _moe_up(gs, gn, xs, w_gate, w_up):
    any_spec = pl.BlockSpec(memory_space=pl.ANY)
    grid_spec = pltpu.PrefetchScalarGridSpec(
        num_scalar_prefetch=2,
        grid=(D_EXPERT // MOE_TN_A, N_EXPERTS),
        in_specs=[any_spec, any_spec, any_spec],
        out_specs=any_spec,
        scratch_shapes=[
            pltpu.VMEM((D_MODEL, MOE_TN_A), F32),
            pltpu.VMEM((D_MODEL, MOE_TN_A), F32),
            pltpu.VMEM((D_MODEL, MOE_TN_A), BF16),
            pltpu.VMEM((D_MODEL, MOE_TN_A), BF16),
            pltpu.VMEM((2, MOE_TM, D_MODEL), F32),
            pltpu.VMEM((2, MOE_TM, MOE_TN_A), BF16),
            pltpu.SemaphoreType.DMA((2,)),
            pltpu.SemaphoreType.DMA((2,)),
            pltpu.SemaphoreType.DMA((2,)),
        ],
    )
    return pl.pallas_call(
        _moe_up_kernel,
        grid_spec=grid_spec,
        out_shape=jax.ShapeDtypeStruct((MOE_CAP, D_EXPERT), BF16),
        compiler_params=_cparams(("arbitrary", "arbitrary")),
        name="moe_up",
    )(gs, gn, w_gate, w_up, xs)


def _moe_down_kernel(gs_ref, gn_ref, wd_hbm, h1_hbm, y_hbm, wd_stage, wdb_ref, hbuf, obuf,
                     wsem, hsem, osem):
    def cast_weights():
        wdb_ref[...] = wd_stage[...].astype(BF16)

    def compute(h):
        return jnp.dot(h, wdb_ref[...], preferred_element_type=F32)

    _stream_expert_rows(gs_ref, gn_ref, (wd_hbm,), (wd_stage,), wsem, h1_hbm, y_hbm, MOE_TN_B,
                        hbuf, obuf, hsem, osem, cast_weights, compute)


def _moe_down(gs, gn, h1, w_down):
    any_spec = pl.BlockSpec(memory_space=pl.ANY)
    grid_spec = pltpu.PrefetchScalarGridSpec(
        num_scalar_prefetch=2,
        grid=(D_MODEL // MOE_TN_B, N_EXPERTS),
        in_specs=[any_spec, any_spec],
        out_specs=any_spec,
        scratch_shapes=[
            pltpu.VMEM((D_EXPERT, MOE_TN_B), F32),
            pltpu.VMEM((D_EXPERT, MOE_TN_B), BF16),
            pltpu.VMEM((2, MOE_TM, D_EXPERT), BF16),
            pltpu.VMEM((2, MOE_TM, MOE_TN_B), F32),
            pltpu.SemaphoreType.DMA((1,)),
            pltpu.SemaphoreType.DMA((2,)),
            pltpu.SemaphoreType.DMA((2,)),
        ],
    )
    return pl.pallas_call(
        _moe_down_kernel,
        grid_spec=grid_spec,
        out_shape=jax.ShapeDtypeStruct((MOE_CAP, D_MODEL), F32),
        compiler_params=_cparams(("arbitrary", "arbitrary")),
        name="moe_down",
    )(gs, gn, w_down, h1)


def _combine_kernel(pos_ref, info_ref, g_ref, h_hbm, y_hbm, o_ref, hbuf, ybuf, sem_h, sem_y):
    i = pl.program_id(0)
    slot = lax.rem(i, 2)

    def h_copy(t, s):
        return pltpu.make_async_copy(h_hbm.at[pl.ds(BLK + t * OUT_T, OUT_T)], hbuf.at[s], sem_h.at[s])

    def fetch(t, s):
        h_copy(t, s).start()

        def issue(r, carry):
            base = (t * OUT_T + r) * 2
            for k in range(2):
                pltpu.make_async_copy(y_hbm.at[pl.ds(pos_ref[base + k], 1)],
                                      ybuf.at[s, k, pl.ds(r, 1)], sem_y.at[s]).start()
            return carry

        lax.fori_loop(0, OUT_T, issue, 0, unroll=4)

    @pl.when(i == 0)
    def _():
        fetch(0, 0)

    @pl.when(i + 1 < pl.num_programs(0))
    def _():
        fetch(i + 1, 1 - slot)

    h_copy(i, slot).wait()
    for k in range(2):
        pltpu.make_async_copy(y_hbm.at[pl.ds(0, OUT_T)], ybuf.at[slot, k], sem_y.at[slot]).wait()
    g1 = info_ref[:, INFO_G1:INFO_G1 + 1]
    g2 = info_ref[:, INFO_G2:INFO_G2 + 1]
    h_new = hbuf[slot] + g1 * ybuf[slot, 0] + g2 * ybuf[slot, 1]
    o_ref[...] = _rms(h_new, g_ref[...])


def _combine(pos_x, info_x, g_final, h, y):
    grid_spec = pltpu.PrefetchScalarGridSpec(
        num_scalar_prefetch=1,
        grid=(SEQ // OUT_T,),
        in_specs=[
            pl.BlockSpec((OUT_T, LANE), lambda i, pos: (i, 0)),
            pl.BlockSpec((1, D_MODEL), lambda i, pos: (0, 0)),
            pl.BlockSpec(memory_space=pl.ANY),
            pl.BlockSpec(memory_space=pl.ANY),
        ],
        out_specs=pl.BlockSpec((OUT_T, D_MODEL), lambda i, pos: (i, 0)),
        scratch_shapes=[
            pltpu.VMEM((2, OUT_T, D_MODEL), F32),
            pltpu.VMEM((2, 2, OUT_T, D_MODEL), F32),
            pltpu.SemaphoreType.DMA((2,)),
            pltpu.SemaphoreType.DMA((2,)),
        ],
    )
    return pl.pallas_call(
        _combine_kernel,
        grid_spec=grid_spec,
        out_shape=jax.ShapeDtypeStruct((SEQ, D_MODEL), F32),
        compiler_params=_cparams(("arbitrary",)),
        name="combine",
    )(pos_x, info_x, g_final, h, y)


def _mixer(h, hn, layer, p, g_next, hn_dtype):
    w_in = p["w_in"][layer]
    a0 = 3 * ATTN_WIDTH
    u0 = a0 + N_HEADS
    w_qkv = w_in[:, :a0].astype(BF16)
    w_f = jnp.pad(w_in[:, a0:u0], ((0, 0), (0, LANE - N_HEADS))).astype(BF16)
    w_a = w_in[:, u0:u0 + CONV_WIDTH].astype(BF16)
    w_g = w_in[:, u0 + CONV_WIDTH:].astype(BF16)
    b_f = jnp.pad(p["b_forget"][layer], (0, LANE - N_HEADS)).reshape(1, LANE)

    qkv = _qkv(hn, w_qkv)
    key_bias = _decay(hn, w_f, b_f)
    attn = _attention(qkv, key_bias, p["attn_out_g"][layer].reshape(N_HEADS, 1, HEAD_DIM))

    hc = _glu(hn, w_a, w_g)
    w_dw = jnp.pad(p["w_dw"][layer], ((0, HALO - CONV_KERNEL), (0, 0)))
    row = lambda v: v.reshape(1, -1)
    conv = _conv(hc, w_dw, row(p["b_dw"][layer]), row(p["conv_ln_g"][layer]),
                 row(p["conv_ln_b"][layer]), p["w_conv_out"][layer].astype(BF16),
                 row(p["conv_out_g"][layer]))
    return _outproj(attn, conv, p["w_out"][layer].astype(BF16), h, g_next, hn_dtype)


def _moe(h, hn, p, j, g_final):
    w_r = jnp.pad(p["moe_w_router"][j], ((0, 0), (0, LANE - N_EXPERTS)))
    info, cnt = _router(hn, w_r)
    e1 = info[:, INFO_E1].astype(jnp.int32)
    e2 = info[:, INFO_E2].astype(jnp.int32)
    r1 = info[:, INFO_R1].astype(jnp.int32)
    r2 = info[:, INFO_R2].astype(jnp.int32)
    counts = cnt[0, :N_EXPERTS].astype(jnp.int32)
    padded = (counts + MOE_TM - 1) // MOE_TM * MOE_TM
    gend = jnp.cumsum(padded)
    gstart = gend - padded
    pos = jnp.stack([gstart[e1] + r1, gstart[e2] + r2], axis=-1).reshape(-1)
    gs = (gstart // MOE_TM).astype(jnp.int32)
    gn = (padded // MOE_TM).astype(jnp.int32)

    xs = _dispatch(pos, hn)
    h1 = _moe_up(gs, gn, xs, p["moe_w_gate"][j], p["moe_w_up"][j])
    y = _moe_down(gs, gn, h1, p["moe_w_down"][j])
    return _combine(pos[2 * BLK:], info[BLK:], g_final, h, y)


def kernel(x, meta_tokens, mix_norm_g, ffn_norm_g, w_in, b_forget, w_dw, b_dw, conv_ln_g, conv_ln_b, w_conv_out, attn_out_g, conv_out_g, w_out, dense_w_gate, dense_w_up, dense_w_down, moe_w_router, moe_w_gate, moe_w_up, moe_w_down, final_norm_g):
    assert x.shape == (1, SEQ, D_MODEL) and meta_tokens.shape == (N_META, D_MODEL)
    p = dict(w_in=w_in, b_forget=b_forget, w_dw=w_dw, b_dw=b_dw, conv_ln_g=conv_ln_g,
             conv_ln_b=conv_ln_b, w_conv_out=w_conv_out, attn_out_g=attn_out_g,
             conv_out_g=conv_out_g, w_out=w_out, moe_w_router=moe_w_router,
             moe_w_gate=moe_w_gate, moe_w_up=moe_w_up, moe_w_down=moe_w_down)
    row = lambda v: v.reshape(1, D_MODEL)
    meta_blk = jnp.pad(meta_tokens.astype(F32), ((PAD, 0), (0, 0)))
    h, hn = _prep(meta_blk, x.reshape(SEQ, D_MODEL), row(mix_norm_g[0]))

    h, hn = _mixer(h, hn, 0, p, row(ffn_norm_g[0]), BF16)
    h, hn = _ffn(hn, dense_w_gate[0].astype(BF16), dense_w_up[0].astype(BF16),
                 dense_w_down[0].astype(BF16), h, row(mix_norm_g[1]))
    h, hn = _mixer(h, hn, 1, p, row(ffn_norm_g[1]), F32)
    out = _moe(h, hn, p, 0, row(final_norm_g))
    return out.reshape(1, SEQ, D_MODEL)
```

```python
import functools

import jax
import jax.numpy as jnp
from jax import lax
from jax.experimental import pallas as pl
from jax.experimental.pallas import tpu as pltpu

F32 = jnp.float32
BF16 = jnp.bfloat16

D_MODEL = 2048
SEQ = 8192
N_META = 16
N_HEADS = 8
HEAD_DIM = 128
ATTN_WIDTH = N_HEADS * HEAD_DIM
CONV_WIDTH = 1024
CONV_GROUPS = 8
CONV_KERNEL = 31
D_FF = 5632
N_EXPERTS = 8
D_EXPERT = 7168
RMS_EPS = 1e-6
LN_EPS = 1e-5
MASK_VALUE = -1e30
LOG2E = 1.4426950408889634

LANE = 128
SUBLANES = 8
BLK = 128
PAD = BLK - N_META
LP = PAD + N_META + SEQ
TM = 640
NT = LP // TM
NB = LP // BLK
HALO = 32
MOE_TM = 256
MOE_TILES = -(-(2 * LP + N_EXPERTS * (MOE_TM - 1)) // MOE_TM)
MOE_CAP = MOE_TILES * MOE_TM
MOE_TN_A = 1024
MOE_TN_B = 512
FFN_TF = 512
OUT_T = 512
VMEM_LIMIT = 56 * 1024 * 1024
HEADS_PER_STEP = 2
ATTN_HW = HEADS_PER_STEP * HEAD_DIM
ATTN_GROUPS = N_HEADS // HEADS_PER_STEP
ROW_DMA_PRIORITY = 1
W_SPLIT = 8
ROW_SPLIT = 4


def _cparams(sem):
    return pltpu.CompilerParams(dimension_semantics=sem, vmem_limit_bytes=VMEM_LIMIT)


def _rms(x, g):
    return x * lax.rsqrt(jnp.mean(x * x, axis=-1, keepdims=True) + RMS_EPS) * g


def _silu(x):
    return x * jax.nn.sigmoid(x)


def _prep_kernel(meta_ref, x_ref, g_ref, h_ref, hn_ref):
    i = pl.program_id(0)
    v = jnp.where(i == 0, meta_ref[...], x_ref[...])
    h_ref[...] = v
    hn_ref[...] = _rms(v, g_ref[...]).astype(BF16)


def _prep(meta_blk, x2d, g):
    return pl.pallas_call(
        _prep_kernel,
        grid=(NB,),
        in_specs=[
            pl.BlockSpec((BLK, D_MODEL), lambda i: (0, 0)),
            pl.BlockSpec((BLK, D_MODEL), lambda i: (jnp.maximum(i - 1, 0), 0)),
            pl.BlockSpec((1, D_MODEL), lambda i: (0, 0)),
        ],
        out_specs=[
            pl.BlockSpec((BLK, D_MODEL), lambda i: (i, 0)),
            pl.BlockSpec((BLK, D_MODEL), lambda i: (i, 0)),
        ],
        out_shape=[
            jax.ShapeDtypeStruct((LP, D_MODEL), F32),
            jax.ShapeDtypeStruct((LP, D_MODEL), BF16),
        ],
        compiler_params=_cparams(("arbitrary",)),
        name="prep",
    )(meta_blk, x2d, g)


def _qkv_kernel(x_ref, w_ref, o_ref):
    j = pl.program_id(0)
    acc = jnp.dot(x_ref[...], w_ref[...], preferred_element_type=F32)
    scale = jnp.where(j == 0, HEAD_DIM ** -0.5 * LOG2E, 1.0).astype(F32)
    res = (acc * scale).astype(BF16)
    for s in range(o_ref.shape[0]):
        o_ref[s] = res[:, s * ATTN_HW:(s + 1) * ATTN_HW]


def _qkv(hn, w_qkv):
    return pl.pallas_call(
        _qkv_kernel,
        grid=(3, NT),
        in_specs=[
            pl.BlockSpec((TM, D_MODEL), lambda j, i: (i, 0)),
            pl.BlockSpec((D_MODEL, ATTN_WIDTH), lambda j, i: (0, j)),
        ],
        out_specs=pl.BlockSpec((ATTN_GROUPS, TM, ATTN_HW), lambda j, i: (j, i, 0)),
        out_shape=jax.ShapeDtypeStruct((3 * ATTN_GROUPS, LP, ATTN_HW), BF16),
        compiler_params=_cparams(("arbitrary", "arbitrary")),
        name="qkv",
    )(hn, w_qkv)


def _glu_kernel(x_ref, wa_ref, wg_ref, o_ref):
    i = pl.program_id(0)
    x = x_ref[...]
    a = jnp.dot(x, wa_ref[...], preferred_element_type=F32)
    g = jnp.dot(x, wg_ref[...], preferred_element_type=F32)
    row = lax.broadcasted_iota(jnp.int32, a.shape, 0) + i * TM
    o_ref[...] = jnp.where(row >= PAD, a * jax.nn.sigmoid(g), 0.0)


def _glu(hn, w_a, w_g):
    return pl.pallas_call(
        _glu_kernel,
        grid=(NT,),
        in_specs=[
            pl.BlockSpec((TM, D_MODEL), lambda i: (i, 0)),
            pl.BlockSpec((D_MODEL, CONV_WIDTH), lambda i: (0, 0)),
            pl.BlockSpec((D_MODEL, CONV_WIDTH), lambda i: (0, 0)),
        ],
        out_specs=pl.BlockSpec((TM, CONV_WIDTH), lambda i: (i, 0)),
        out_shape=jax.ShapeDtypeStruct((LP, CONV_WIDTH), F32),
        compiler_params=_cparams(("arbitrary",)),
        name="glu",
    )(hn, w_a, w_g)


def _split3(x):
    hi = x.astype(BF16)
    r1 = x - hi.astype(F32)
    mid = r1.astype(BF16)
    lo = (r1 - mid.astype(F32)).astype(BF16)
    return hi, mid, lo


def _decay_kernel(x_ref, w_ref, b_ref, kb_ref, carry_ref):
    i = pl.program_id(0)

    @pl.when(i == 0)
    def _():
        carry_ref[...] = jnp.zeros_like(carry_ref)

    f = jnp.dot(x_ref[...], w_ref[...], preferred_element_type=F32) + b_ref[...]
    log_f = jnp.minimum(f, 0.0) - jnp.log1p(jnp.exp(-jnp.abs(f)))
    row = lax.broadcasted_iota(jnp.int32, log_f.shape, 0) + i * TM
    log_f = jnp.where(row >= PAD, log_f, 0.0)
    r = lax.broadcasted_iota(jnp.int32, (TM, TM), 0)
    c = lax.broadcasted_iota(jnp.int32, (TM, TM), 1)
    tri = jnp.where(c <= r, 1.0, 0.0).astype(BF16)
    hi, mid, lo = _split3(log_f)
    cs = (jnp.dot(tri, hi, preferred_element_type=F32)
          + jnp.dot(tri, mid, preferred_element_type=F32)
          + jnp.dot(tri, lo, preferred_element_type=F32)) + carry_ref[0:1, :]
    carry_ref[...] = jnp.broadcast_to(cs[TM - 1:TM, :], carry_ref.shape)
    lane = lax.broadcasted_iota(jnp.int32, (TM, LANE), 1)
    for h in range(N_HEADS):
        bias = jnp.where(row[:, h:h + 1] >= PAD, -LOG2E * cs[:, h:h + 1], MASK_VALUE)
        hi, mid, lo = (v.astype(F32) for v in _split3(jnp.broadcast_to(bias, (TM, LANE))))
        pieces = jnp.where(lane == 0, hi, jnp.where(lane == 1, mid, jnp.where(lane == 2, lo, 0.0)))
        kb_ref[h] = pieces.astype(BF16)


def _decay(hn, w_f, b_f):
    return pl.pallas_call(
        _decay_kernel,
        grid=(NT,),
        in_specs=[
            pl.BlockSpec((TM, D_MODEL), lambda i: (i, 0)),
            pl.BlockSpec((D_MODEL, LANE), lambda i: (0, 0)),
            pl.BlockSpec((1, LANE), lambda i: (0, 0)),
        ],
        out_specs=pl.BlockSpec((N_HEADS, TM, LANE), lambda i: (0, i, 0)),
        out_shape=jax.ShapeDtypeStruct((N_HEADS, LP, LANE), BF16),
        scratch_shapes=[pltpu.VMEM((8, LANE), F32)],
        compiler_params=_cparams(("arbitrary",)),
        name="decay",
    )(hn, w_f, b_f)


ATTN_STRIP = 16
ATTN_WIDE = 2 * TM
ATTN_PART = (512, ATTN_WIDE - 512)
assert ATTN_PART[1] >= TM and all(w % 256 == 0 for w in ATTN_PART)


def _attn_kernel(q_ref, k_ref, v_ref, kb_ref, g_ref, o_ref, *state_refs):
    i = pl.program_id(1)
    n_state = len(state_refs) // HEADS_PER_STEP
    heads = []
    for h in range(HEADS_PER_STEP):
        m_ref, l_ref, acc_ref, s0, s1, p0, p1 = state_refs[n_state * h:n_state * (h + 1)]
        heads.append((m_ref, l_ref, acc_ref, (s0, s1), (p0, p1)))
    for m_ref, l_ref, acc_ref, _, _ in heads:
        m_ref[...] = jnp.full_like(m_ref, -1e38)
        l_ref[...] = jnp.zeros_like(l_ref)
        acc_ref[...] = jnp.zeros_like(acc_ref)

    lane = lax.broadcasted_iota(jnp.int32, (TM, HEAD_DIM), 1)
    q_ones = jnp.where(lane < 3, 1.0, 0.0).astype(BF16)

    def lanes(h):
        return slice(h * HEAD_DIM, (h + 1) * HEAD_DIM)

    def key_rows(key0, width):
        return pl.ds(pl.multiple_of(key0, LANE), width)

    def qk(h, key0, width, part):
        s_ref = heads[h][3][part]
        keys = key_rows(key0, width)
        q_aug = jnp.concatenate([q_ref[0, :, lanes(h)], q_ones], axis=1)
        k_aug = jnp.concatenate([k_ref[0, keys, lanes(h)], kb_ref[h, keys, :]], axis=1)
        s_ref[:, :width] = lax.dot_general(q_aug, k_aug, (((1,), (1,)), ((), ())),
                                           preferred_element_type=F32)

    def softmax_pv(h, key0, width, part, diagonal):
        m_ref, l_ref, acc_ref, s_refs, p_refs = heads[h]
        s_ref, p_ref = s_refs[part], p_refs[part]
        for r in range(TM // ATTN_STRIP):
            rs = slice(r * ATTN_STRIP, (r + 1) * ATTN_STRIP)
            nc = min(width, -(-((r + 1) * ATTN_STRIP) // LANE) * LANE) if diagonal else width

            def logits():
                s = s_ref[rs, :nc]
                if diagonal:
                    row = lax.broadcasted_iota(jnp.int32, s.shape, 0) + r * ATTN_STRIP
                    col = lax.broadcasted_iota(jnp.int32, s.shape, 1)
                    s = jnp.where(col <= row, s, MASK_VALUE)
                return s

            m_prev = m_ref[rs, :]
            m_new = jnp.maximum(m_prev, jnp.max(logits(), axis=-1, keepdims=True))
            p = jnp.exp2(logits() - jnp.concatenate([m_new] * (nc // LANE), axis=1))
            p_ref[rs, :nc] = p.astype(BF16)
            if nc < width:
                p_ref[rs, nc:width] = jnp.zeros((ATTN_STRIP, width - nc), BF16)
            alpha = jnp.exp2(m_prev - m_new)
            l_ref[rs, :] = alpha * l_ref[rs, :] + jnp.sum(p, axis=-1, keepdims=True)
            m_ref[rs, :] = m_new
            acc_ref[rs, :] = alpha * acc_ref[rs, :]
        acc_ref[...] += jnp.dot(p_ref[:, :width], v_ref[0, key_rows(key0, width), lanes(h)],
                                preferred_element_type=F32)

    def block(parts, diagonal=False):
        for key0, width, part in parts:
            for h in range(HEADS_PER_STEP):
                qk(h, key0, width, part)
        for key0, width, part in parts:
            for h in range(HEADS_PER_STEP):
                softmax_pv(h, key0, width, part, diagonal)

    def wide_block(j, carry):
        k0 = j * ATTN_WIDE
        block(((k0, ATTN_PART[0], 0), (k0 + ATTN_PART[0], ATTN_PART[1], 1)))
        return carry

    lax.fori_loop(0, i // 2, wide_block, 0)

    @pl.when(i % 2 == 1)
    def _():
        block((((i - 1) * TM, TM, 1),))

    block(((i * TM, TM, 1),), diagonal=True)
    for h, (_, l_ref, acc_ref, _, _) in enumerate(heads):
        hs = slice(h * HEAD_DIM, (h + 1) * HEAD_DIM)
        o = acc_ref[...] / l_ref[...]
        o_ref[:, hs] = _rms(o, g_ref[h]).astype(BF16)


def _attention(qkv, key_bias, g_attn):
    hw = ATTN_HW
    nhp = ATTN_GROUPS
    return pl.pallas_call(
        _attn_kernel,
        grid=(nhp, NT),
        in_specs=[
            pl.BlockSpec((1, TM, hw), lambda hp, i: (hp, i, 0)),
            pl.BlockSpec((1, LP, hw), lambda hp, i: (nhp + hp, 0, 0)),
            pl.BlockSpec((1, LP, hw), lambda hp, i: (2 * nhp + hp, 0, 0)),
            pl.BlockSpec((HEADS_PER_STEP, LP, LANE), lambda hp, i: (hp, 0, 0)),
            pl.BlockSpec((HEADS_PER_STEP, 1, HEAD_DIM), lambda hp, i: (hp, 0, 0)),
        ],
        out_specs=pl.BlockSpec((TM, hw), lambda hp, i: (i, hp)),
        out_shape=jax.ShapeDtypeStruct((LP, ATTN_WIDTH), BF16),
        scratch_shapes=[
            pltpu.VMEM((TM, HEAD_DIM), F32),
            pltpu.VMEM((TM, HEAD_DIM), F32),
            pltpu.VMEM((TM, HEAD_DIM), F32),
            pltpu.VMEM((TM, ATTN_PART[0]), F32),
            pltpu.VMEM((TM, ATTN_PART[1]), F32),
            pltpu.VMEM((TM, ATTN_PART[0]), BF16),
            pltpu.VMEM((TM, ATTN_PART[1]), BF16),
        ] * HEADS_PER_STEP,
        compiler_params=_cparams(("arbitrary", "arbitrary")),
        name="attention",
    )(qkv, qkv, qkv, key_bias, g_attn)


CONV_RC = 64
CONV_CC = 512
CONV_ROWS = HALO + TM - SUBLANES


def _conv_kernel(x_ref, halo_ref, wdw_ref, bdw_ref, lng_ref, lnb_ref, wpw_ref, og_ref, o_ref,
                 xs_ref, cv_ref):
    i = pl.program_id(0)
    xs_ref[0, 0:HALO, :] = jnp.where(i == 0, 0.0, halo_ref[...])
    xs_ref[0, HALO:HALO + TM, :] = x_ref[...]
    for s in range(1, SUBLANES):
        xs_ref[s, 0:CONV_ROWS, :] = xs_ref[0, s:s + CONV_ROWS, :]
    off = HALO - (CONV_KERNEL - 1)
    for rc in range(TM // CONV_RC):
        for cc in range(CONV_WIDTH // CONV_CC):
            cs = slice(cc * CONV_CC, (cc + 1) * CONV_CC)
            acc = jnp.broadcast_to(bdw_ref[:, cs], (CONV_RC, CONV_CC))
            for j in range(CONV_KERNEL):
                s, a = (off + j) % SUBLANES, (off + j) // SUBLANES * SUBLANES
                r0 = rc * CONV_RC + a
                acc = acc + wdw_ref[j:j + 1, cs] * xs_ref[s, r0:r0 + CONV_RC, cs]
            cv_ref[rc * CONV_RC:(rc + 1) * CONV_RC, cs] = acc
    y = cv_ref[...]
    mu = jnp.mean(y, axis=-1, keepdims=True)
    yc = y - mu
    var = jnp.mean(yc * yc, axis=-1, keepdims=True)
    a = _silu(yc * lax.rsqrt(var + LN_EPS) * lng_ref[...] + lnb_ref[...])
    z = jnp.dot(a.astype(BF16), wpw_ref[...], preferred_element_type=F32)
    gd = CONV_WIDTH // CONV_GROUPS
    for g in range(CONV_GROUPS):
        gs = slice(g * gd, (g + 1) * gd)
        o_ref[:, gs] = _rms(z[:, gs], og_ref[:, gs]).astype(BF16)


def _conv(hc, w_dw, b_dw, ln_g, ln_b, w_pw, out_g):
    row = lambda i: (0, 0)
    return pl.pallas_call(
        _conv_kernel,
        grid=(NT,),
        in_specs=[
            pl.BlockSpec((TM, CONV_WIDTH), lambda i: (i, 0)),
            pl.BlockSpec((HALO, CONV_WIDTH), lambda i: (jnp.maximum(i * (TM // HALO) - 1, 0), 0)),
            pl.BlockSpec((HALO, CONV_WIDTH), row),
            pl.BlockSpec((1, CONV_WIDTH), row),
            pl.BlockSpec((1, CONV_WIDTH), row),
            pl.BlockSpec((1, CONV_WIDTH), row),
            pl.BlockSpec((CONV_WIDTH, CONV_WIDTH), row),
            pl.BlockSpec((1, CONV_WIDTH), row),
        ],
        out_specs=pl.BlockSpec((TM, CONV_WIDTH), lambda i: (i, 0)),
        out_shape=jax.ShapeDtypeStruct((LP, CONV_WIDTH), BF16),
        scratch_shapes=[
            pltpu.VMEM((SUBLANES, HALO + TM, CONV_WIDTH), F32),
            pltpu.VMEM((TM, CONV_WIDTH), F32),
        ],
        compiler_params=_cparams(("arbitrary",)),
        name="conv",
    )(hc, hc, w_dw, b_dw, ln_g, ln_b, w_pw, out_g)


def _outproj_kernel(a_ref, c_ref, w_ref, h_ref, g_ref, ho_ref, hn_ref):
    acc = jnp.dot(a_ref[...], w_ref[0:ATTN_WIDTH, :], preferred_element_type=F32)
    acc = acc + jnp.dot(c_ref[...], w_ref[ATTN_WIDTH:, :], preferred_element_type=F32)
    h_new = h_ref[...] + acc
    ho_ref[...] = h_new
    hn_ref[...] = _rms(h_new, g_ref[...]).astype(hn_ref.dtype)


def _outproj(attn, conv, w_out, h, g_next, hn_dtype):
    return pl.pallas_call(
        _outproj_kernel,
        grid=(NT,),
        in_specs=[
            pl.BlockSpec((TM, ATTN_WIDTH), lambda i: (i, 0)),
            pl.BlockSpec((TM, CONV_WIDTH), lambda i: (i, 0)),
            pl.BlockSpec((D_MODEL, D_MODEL), lambda i: (0, 0)),
            pl.BlockSpec((TM, D_MODEL), lambda i: (i, 0)),
            pl.BlockSpec((1, D_MODEL), lambda i: (0, 0)),
        ],
        out_specs=[
            pl.BlockSpec((TM, D_MODEL), lambda i: (i, 0)),
            pl.BlockSpec((TM, D_MODEL), lambda i: (i, 0)),
        ],
        out_shape=[
            jax.ShapeDtypeStruct((LP, D_MODEL), F32),
            jax.ShapeDtypeStruct((LP, D_MODEL), hn_dtype),
        ],
        compiler_params=_cparams(("arbitrary",)),
        name="outproj",
    )(attn, conv, w_out, h, g_next)


def _ffn_kernel(x_ref, wg_ref, wu_ref, wd_ref, h_ref, g_ref, ho_ref, hn_ref, acc_ref):
    j = pl.program_id(1)

    @pl.when(j == 0)
    def _():
        acc_ref[...] = jnp.zeros_like(acc_ref)

    x = x_ref[...]
    g = jnp.dot(x, wg_ref[...], preferred_element_type=F32)
    u = jnp.dot(x, wu_ref[...], preferred_element_type=F32)
    a = (_silu(g) * u).astype(BF16)
    acc_ref[...] += jnp.dot(a, wd_ref[...], preferred_element_type=F32)

    @pl.when(j == pl.num_programs(1) - 1)
    def _():
        h_new = h_ref[...] + acc_ref[...]
        ho_ref[...] = h_new
        hn_ref[...] = _rms(h_new, g_ref[...]).astype(BF16)


def _ffn(hn, w_gate, w_up, w_down, h, g_next):
    return pl.pallas_call(
        _ffn_kernel,
        grid=(NT, D_FF // FFN_TF),
        in_specs=[
            pl.BlockSpec((TM, D_MODEL), lambda i, j: (i, 0)),
            pl.BlockSpec((D_MODEL, FFN_TF), lambda i, j: (0, j)),
            pl.BlockSpec((D_MODEL, FFN_TF), lambda i, j: (0, j)),
            pl.BlockSpec((FFN_TF, D_MODEL), lambda i, j: (j, 0)),
            pl.BlockSpec((TM, D_MODEL), lambda i, j: (i, 0)),
            pl.BlockSpec((1, D_MODEL), lambda i, j: (0, 0)),
        ],
        out_specs=[
            pl.BlockSpec((TM, D_MODEL), lambda i, j: (i, 0)),
            pl.BlockSpec((TM, D_MODEL), lambda i, j: (i, 0)),
        ],
        out_shape=[
            jax.ShapeDtypeStruct((LP, D_MODEL), F32),
            jax.ShapeDtypeStruct((LP, D_MODEL), BF16),
        ],
        scratch_shapes=[pltpu.VMEM((TM, D_MODEL), F32)],
        compiler_params=_cparams(("arbitrary", "arbitrary")),
        name="ffn",
    )(hn, w_gate, w_up, w_down, h, g_next)


INFO_E1, INFO_E2, INFO_G1, INFO_G2, INFO_R1, INFO_R2 = range(6)


def _router_kernel(x_ref, w_ref, info_ref, cnt_ref, carry_ref):
    i = pl.program_id(0)

    @pl.when(i == 0)
    def _():
        carry_ref[...] = jnp.zeros_like(carry_ref)

    logits = jnp.dot(x_ref[...], w_ref[...], preferred_element_type=F32,
                     precision=lax.Precision.HIGHEST)
    lane = lax.broadcasted_iota(jnp.int32, logits.shape, 1)
    neg = jnp.float32(-3e38)
    l1 = jnp.where(lane < N_EXPERTS, logits, neg)
    m1 = jnp.max(l1, axis=-1, keepdims=True)
    i1 = jnp.min(jnp.where(l1 == m1, lane, LANE), axis=-1, keepdims=True)
    l2 = jnp.where(lane == i1, neg, l1)
    m2 = jnp.max(l2, axis=-1, keepdims=True)
    i2 = jnp.min(jnp.where(l2 == m2, lane, LANE), axis=-1, keepdims=True)
    ex = jnp.exp(m2 - m1)
    g1 = 1.0 / (1.0 + ex)
    g2 = ex / (1.0 + ex)
    sel = jnp.where(lane == i1, 1.0, jnp.where(lane == i2, 1.0, 0.0))
    r = lax.broadcasted_iota(jnp.int32, (TM, TM), 0)
    c = lax.broadcasted_iota(jnp.int32, (TM, TM), 1)
    tri = jnp.where(c < r, 1.0, 0.0).astype(BF16)
    carry = carry_ref[0:1, :]
    rank = jnp.dot(tri, sel.astype(BF16), preferred_element_type=F32) + carry
    r1 = jnp.sum(jnp.where(lane == i1, rank, 0.0), axis=-1, keepdims=True)
    r2 = jnp.sum(jnp.where(lane == i2, rank, 0.0), axis=-1, keepdims=True)
    info = jnp.zeros_like(logits)
    for k, v in ((INFO_E1, i1.astype(F32)), (INFO_E2, i2.astype(F32)), (INFO_G1, g1),
                 (INFO_G2, g2), (INFO_R1, r1), (INFO_R2, r2)):
        info = jnp.where(lane == k, v, info)
    info_ref[...] = info
    total = carry + jnp.sum(sel, axis=0, keepdims=True)
    carry_ref[...] = jnp.broadcast_to(total, carry_ref.shape)
    cnt_ref[...] = jnp.broadcast_to(total, cnt_ref.shape)


def _router(hn, w_r):
    return pl.pallas_call(
        _router_kernel,
        grid=(NT,),
        in_specs=[
            pl.BlockSpec((TM, D_MODEL), lambda i: (i, 0)),
            pl.BlockSpec((D_MODEL, LANE), lambda i: (0, 0)),
        ],
        out_specs=[
            pl.BlockSpec((TM, LANE), lambda i: (i, 0)),
            pl.BlockSpec((8, LANE), lambda i: (0, 0)),
        ],
        out_shape=[
            jax.ShapeDtypeStruct((LP, LANE), F32),
            jax.ShapeDtypeStruct((8, LANE), F32),
        ],
        scratch_shapes=[pltpu.VMEM((8, LANE), F32)],
        compiler_params=_cparams(("arbitrary",)),
        name="router",
    )(hn, w_r)


def _dispatch_kernel(pos_ref, x_ref, xs_in_ref, xs_ref, sem):
    del xs_in_ref
    i = pl.program_id(0)

    def issue(r, carry):
        base = (i * BLK + r) * 2
        for k in range(2):
            pltpu.make_async_copy(x_ref.at[pl.ds(r, 1)],
                                  xs_ref.at[pl.ds(pos_ref[base + k], 1)], sem).start()
        return carry

    lax.fori_loop(0, BLK, issue, 0, unroll=4)
    for _ in range(2):
        pltpu.make_async_copy(x_ref, xs_ref.at[pl.ds(0, BLK)], sem).wait()


def _dispatch(pos, hn):
    grid_spec = pltpu.PrefetchScalarGridSpec(
        num_scalar_prefetch=1,
        grid=(NB,),
        in_specs=[
            pl.BlockSpec((BLK, D_MODEL), lambda i, pos: (i, 0)),
            pl.BlockSpec(memory_space=pl.ANY),
        ],
        out_specs=pl.BlockSpec(memory_space=pl.ANY),
        scratch_shapes=[pltpu.SemaphoreType.DMA(())],
    )
    return pl.pallas_call(
        _dispatch_kernel,
        grid_spec=grid_spec,
        out_shape=jax.ShapeDtypeStruct((MOE_CAP, D_MODEL), F32),
        input_output_aliases={2: 0},
        compiler_params=_cparams(("arbitrary",)),
        name="dispatch",
    )(pos, hn, jnp.zeros((MOE_CAP, D_MODEL), F32))


def _stream_expert_rows(gs_ref, gn_ref, w_hbm, w_stage, wsem, src_hbm, dst_hbm, tn, ibuf, obuf, isem, osem,
                        sbuf, sobuf, ssem, cast_weights, compute):
    j, e = pl.program_id(0), pl.program_id(1)
    nj, ne = pl.num_programs(0), pl.num_programs(1)
    g0 = gs_ref[e]
    n = gn_ref[e]
    cols = pl.ds(pl.multiple_of(j * tn, tn), tn)

    def w_copies(jj, ee):
        wcols = pl.ds(pl.multiple_of(jj * tn, tn), tn)
        copies = []
        for k in range(len(w_hbm)):
            chunk = w_stage[k].shape[0] // W_SPLIT
            for c in range(W_SPLIT):
                rows = pl.ds(c * chunk, chunk)
                copies.append(pltpu.make_async_copy(w_hbm[k].at[ee, rows, wcols], w_stage[k].at[rows],
                                                    wsem.at[k]))
        return copies

    def start_weights(jj, ee):
        for c, cp in enumerate(w_copies(jj, ee)):
            cp.start(priority=c % 2)

    class rows_in:
        def __init__(self, row0, nrows, buf, sem):
            chunk = nrows // ROW_SPLIT
            self.copies = [
                pltpu.make_async_copy(src_hbm.at[pl.ds(row0 + c * chunk, chunk)],
                                      buf.at[pl.ds(c * chunk, chunk)], sem)
                for c in range(ROW_SPLIT)]

        def start(self, priority):
            for cp in self.copies:
                cp.start(priority=priority)

        def wait(self):
            for cp in self.copies:
                cp.wait()

    pair = 2 * MOE_TM
    n2 = n // 2
    has_single = n - 2 * n2 == 1
    row0 = g0 * MOE_TM
    single_row = row0 + n2 * pair

    def in_copy(u, slot):
        return rows_in(row0 + u * pair, pair, ibuf.at[slot], isem.at[slot])

    def out_copy(u, slot):
        return pltpu.make_async_copy(obuf.at[slot], dst_hbm.at[pl.ds(row0 + u * pair, pair), cols],
                                     osem.at[slot])

    def single_in():
        return rows_in(single_row, MOE_TM, sbuf, ssem.at[0])

    def single_out(row):
        return pltpu.make_async_copy(sobuf, dst_hbm.at[pl.ds(row, MOE_TM), cols], ssem.at[1])

    @pl.when(jnp.logical_and(j == 0, e == 0))
    def _():
        start_weights(j, e)

    @pl.when(n2 > 0)
    def _():
        in_copy(0, 0).start(priority=ROW_DMA_PRIORITY)

    @pl.when(has_single)
    def _():
        single_in().start(priority=ROW_DMA_PRIORITY)

    for cp in w_copies(j, e):
        cp.wait()
    cast_weights()
    last_e = e == ne - 1

    @pl.when(jnp.logical_not(jnp.logical_and(j == nj - 1, last_e)))
    def _():
        start_weights(jnp.where(last_e, j + 1, j), jnp.where(last_e, 0, e + 1))

    def body(t, carry):
        slot = lax.rem(t, 2)
        in_copy(t, slot).wait()

        @pl.when(t + 1 < n2)
        def _():
            in_copy(t + 1, 1 - slot).start(priority=ROW_DMA_PRIORITY)

        out = compute(ibuf[slot])

        @pl.when(t >= 2)
        def _():
            out_copy(t - 2, slot).wait()

        obuf[slot] = out
        out_copy(t, slot).start(priority=ROW_DMA_PRIORITY)
        return carry

    lax.fori_loop(0, n2, body, 0)
    for back in (2, 1):
        @pl.when(n2 >= back)
        def _():
            out_copy(n2 - back, lax.rem(n2 - back, 2)).wait()

    @pl.when(has_single)
    def _():
        single_in().wait()
        sobuf[...] = compute(sbuf[...])
        cp = single_out(single_row)
        cp.start(priority=ROW_DMA_PRIORITY)
        cp.wait()

    @pl.when(last_e)
    def _():
        sobuf[...] = jnp.zeros(sobuf.shape, sobuf.dtype)

        def fill(t, carry):
            cp = single_out(t * MOE_TM)
            cp.start()
            cp.wait()
            return carry

        lax.fori_loop(g0 + n, MOE_TILES, fill, 0)


def _moe_up_kernel(gs_ref, gn_ref, wg_hbm, wu_hbm, xs_hbm, h1_hbm, wg_stage, wu_stage, wgb_ref, wub_ref,
                   xbuf, obuf, sbuf, sobuf, wsem, xsem, osem, ssem):
    def cast_weights():
        wgb_ref[...] = wg_stage[...].astype(BF16)
        wub_ref[...] = wu_stage[...].astype(BF16)

    def compute(x):
        xb = x.astype(BF16)
        g = jnp.dot(xb, wgb_ref[...], preferred_element_type=F32)
        u = jnp.dot(xb, wub_ref[...], preferred_element_type=F32)
        return (_silu(g) * u).astype(BF16)

    _stream_expert_rows(gs_ref, gn_ref, (wg_hbm, wu_hbm), (wg_stage, wu_stage), wsem, xs_hbm, h1_hbm,
                        MOE_TN_A, xbuf, obuf, xsem, osem, sbuf, sobuf, ssem, cast_weights, compute)


def _moe_up(gs, gn, xs, w_gate, w_up):
    any_spec = pl.BlockSpec(memory_space=pl.ANY)
    grid_spec = pltpu.PrefetchScalarGridSpec(
        num_scalar_prefetch=2,
        grid=(D_EXPERT // MOE_TN_A, N_EXPERTS),
        in_specs=[any_spec, any_spec, any_spec],
        out_specs=any_spec,
        scratch_shapes=[
            pltpu.VMEM((D_MODEL, MOE_TN_A), F32),
            pltpu.VMEM((D_MODEL, MOE_TN_A), F32),
            pltpu.VMEM((D_MODEL, MOE_TN_A), BF16),
            pltpu.VMEM((D_MODEL, MOE_TN_A), BF16),
            pltpu.VMEM((2, 2 * MOE_TM, D_MODEL), F32),
            pltpu.VMEM((2, 2 * MOE_TM, MOE_TN_A), BF16),
            pltpu.VMEM((MOE_TM, D_MODEL), F32),
            pltpu.VMEM((MOE_TM, MOE_TN_A), BF16),
            pltpu.SemaphoreType.DMA((2,)),
            pltpu.SemaphoreType.DMA((2,)),
            pltpu.SemaphoreType.DMA((2,)),
            pltpu.SemaphoreType.DMA((2,)),
        ],
    )
    return pl.pallas_call(
        _moe_up_kernel,
        grid_spec=grid_spec,
        out_shape=jax.ShapeDtypeStruct((MOE_CAP, D_EXPERT), BF16),
        compiler_params=_cparams(("arbitrary", "arbitrary")),
        name="moe_up",
    )(gs, gn, w_gate, w_up, xs)


def _moe_down_kernel(gs_ref, gn_ref, wd_hbm, h1_hbm, y_hbm, wd_stage, wdb_ref, hbuf, obuf, sbuf, sobuf,
                     wsem, hsem, osem, ssem):
    def cast_weights():
        wdb_ref[...] = wd_stage[...].astype(BF16)

    def compute(h):
        return jnp.dot(h, wdb_ref[...], preferred_element_type=F32)

    _stream_expert_rows(gs_ref, gn_ref, (wd_hbm,), (wd_stage,), wsem, h1_hbm, y_hbm, MOE_TN_B,
                        hbuf, obuf, hsem, osem, sbuf, sobuf, ssem, cast_weights, compute)


def _moe_down(gs, gn, h1, w_down):
    any_spec = pl.BlockSpec(memory_space=pl.ANY)
    grid_spec = pltpu.PrefetchScalarGridSpec(
        num_scalar_prefetch=2,
        grid=(D_MODEL // MOE_TN_B, N_EXPERTS),
        in_specs=[any_spec, any_spec],
        out_specs=any_spec,
        scratch_shapes=[
            pltpu.VMEM((D_EXPERT, MOE_TN_B), F32),
            pltpu.VMEM((D_EXPERT, MOE_TN_B), BF16),
            pltpu.VMEM((2, 2 * MOE_TM, D_EXPERT), BF16),
            pltpu.VMEM((2, 2 * MOE_TM, MOE_TN_B), F32),
            pltpu.VMEM((MOE_TM, D_EXPERT), BF16),
            pltpu.VMEM((MOE_TM, MOE_TN_B), F32),
            pltpu.SemaphoreType.DMA((1,)),
            pltpu.SemaphoreType.DMA((2,)),
            pltpu.SemaphoreType.DMA((2,)),
            pltpu.SemaphoreType.DMA((2,)),
        ],
    )
    return pl.pallas_call(
        _moe_down_kernel,
        grid_spec=grid_spec,
        out_shape=jax.ShapeDtypeStruct((MOE_CAP, D_MODEL), F32),
        compiler_params=_cparams(("arbitrary", "arbitrary")),
        name="moe_down",
    )(gs, gn, w_down, h1)


def _combine_kernel(pos_ref, info_ref, g_ref, h_hbm, y_hbm, o_ref, hbuf, ybuf, sem_h, sem_y):
    i = pl.program_id(0)
    slot = lax.rem(i, 2)

    def h_copy(t, s):
        return pltpu.make_async_copy(h_hbm.at[pl.ds(BLK + t * OUT_T, OUT_T)], hbuf.at[s], sem_h.at[s])

    def fetch(t, s):
        h_copy(t, s).start()

        def issue(r, carry):
            base = (t * OUT_T + r) * 2
            for k in range(2):
                pltpu.make_async_copy(y_hbm.at[pl.ds(pos_ref[base + k], 1)],
                                      ybuf.at[s, k, pl.ds(r, 1)], sem_y.at[s]).start()
            return carry

        lax.fori_loop(0, OUT_T, issue, 0, unroll=4)

    @pl.when(i == 0)
    def _():
        fetch(0, 0)

    @pl.when(i + 1 < pl.num_programs(0))
    def _():
        fetch(i + 1, 1 - slot)

    h_copy(i, slot).wait()
    for k in range(2):
        pltpu.make_async_copy(y_hbm.at[pl.ds(0, OUT_T)], ybuf.at[slot, k], sem_y.at[slot]).wait()
    g1 = info_ref[:, INFO_G1:INFO_G1 + 1]
    g2 = info_ref[:, INFO_G2:INFO_G2 + 1]
    h_new = hbuf[slot] + g1 * ybuf[slot, 0] + g2 * ybuf[slot, 1]
    o_ref[...] = _rms(h_new, g_ref[...])


def _combine(pos_x, info_x, g_final, h, y):
    grid_spec = pltpu.PrefetchScalarGridSpec(
        num_scalar_prefetch=1,
        grid=(SEQ // OUT_T,),
        in_specs=[
            pl.BlockSpec((OUT_T, LANE), lambda i, pos: (i, 0)),
            pl.BlockSpec((1, D_MODEL), lambda i, pos: (0, 0)),
            pl.BlockSpec(memory_space=pl.ANY),
            pl.BlockSpec(memory_space=pl.ANY),
        ],
        out_specs=pl.BlockSpec((OUT_T, D_MODEL), lambda i, pos: (i, 0)),
        scratch_shapes=[
            pltpu.VMEM((2, OUT_T, D_MODEL), F32),
            pltpu.VMEM((2, 2, OUT_T, D_MODEL), F32),
            pltpu.SemaphoreType.DMA((2,)),
            pltpu.SemaphoreType.DMA((2,)),
        ],
    )
    return pl.pallas_call(
        _combine_kernel,
        grid_spec=grid_spec,
        out_shape=jax.ShapeDtypeStruct((SEQ, D_MODEL), F32),
        compiler_params=_cparams(("arbitrary",)),
        name="combine",
    )(pos_x, info_x, g_final, h, y)


def _mixer(h, hn, layer, p, g_next, hn_dtype):
    w_in = p["w_in"][layer]
    a0 = 3 * ATTN_WIDTH
    u0 = a0 + N_HEADS
    w_qkv = w_in[:, :a0].astype(BF16)
    w_f = jnp.pad(w_in[:, a0:u0], ((0, 0), (0, LANE - N_HEADS))).astype(BF16)
    w_a = w_in[:, u0:u0 + CONV_WIDTH].astype(BF16)
    w_g = w_in[:, u0 + CONV_WIDTH:].astype(BF16)
    b_f = jnp.pad(p["b_forget"][layer], (0, LANE - N_HEADS)).reshape(1, LANE)

    qkv = _qkv(hn, w_qkv)
    key_bias = _decay(hn, w_f, b_f)
    attn = _attention(qkv, key_bias, p["attn_out_g"][layer].reshape(N_HEADS, 1, HEAD_DIM))

    hc = _glu(hn, w_a, w_g)
    w_dw = jnp.pad(p["w_dw"][layer], ((0, HALO - CONV_KERNEL), (0, 0)))
    row = lambda v: v.reshape(1, -1)
    conv = _conv(hc, w_dw, row(p["b_dw"][layer]), row(p["conv_ln_g"][layer]),
                 row(p["conv_ln_b"][layer]), p["w_conv_out"][layer].astype(BF16),
                 row(p["conv_out_g"][layer]))
    return _outproj(attn, conv, p["w_out"][layer].astype(BF16), h, g_next, hn_dtype)


def _moe(h, hn, p, j, g_final):
    w_r = jnp.pad(p["moe_w_router"][j], ((0, 0), (0, LANE - N_EXPERTS)))
    info, cnt = _router(hn, w_r)
    e1 = info[:, INFO_E1].astype(jnp.int32)
    e2 = info[:, INFO_E2].astype(jnp.int32)
    r1 = info[:, INFO_R1].astype(jnp.int32)
    r2 = info[:, INFO_R2].astype(jnp.int32)
    counts = cnt[0, :N_EXPERTS].astype(jnp.int32)
    padded = (counts + MOE_TM - 1) // MOE_TM * MOE_TM
    gend = jnp.cumsum(padded)
    gstart = gend - padded
    pos = jnp.stack([gstart[e1] + r1, gstart[e2] + r2], axis=-1).reshape(-1)
    gs = (gstart // MOE_TM).astype(jnp.int32)
    gn = (padded // MOE_TM).astype(jnp.int32)

    xs = _dispatch(pos, hn)
    h1 = _moe_up(gs, gn, xs, p["moe_w_gate"][j], p["moe_w_up"][j])
    y = _moe_down(gs, gn, h1, p["moe_w_down"][j])
    return _combine(pos[2 * BLK:], info[BLK:], g_final, h, y)


def kernel(x, meta_tokens, mix_norm_g, ffn_norm_g, w_in, b_forget, w_dw, b_dw, conv_ln_g, conv_ln_b, w_conv_out, attn_out_g, conv_out_g, w_out, dense_w_gate, dense_w_up, dense_w_down, moe_w_router, moe_w_gate, moe_w_up, moe_w_down, final_norm_g):
    assert x.shape == (1, SEQ, D_MODEL) and meta_tokens.shape == (N_META, D_MODEL)
    p = dict(w_in=w_in, b_forget=b_forget, w_dw=w_dw, b_dw=b_dw, conv_ln_g=conv_ln_g,
             conv_ln_b=conv_ln_b, w_conv_out=w_conv_out, attn_out_g=attn_out_g,
             conv_out_g=conv_out_g, w_out=w_out, moe_w_router=moe_w_router,
             moe_w_gate=moe_w_gate, moe_w_up=moe_w_up, moe_w_down=moe_w_down)
    row = lambda v: v.reshape(1, D_MODEL)
    meta_blk = jnp.pad(meta_tokens.astype(F32), ((PAD, 0), (0, 0)))
    h, hn = _prep(meta_blk, x.reshape(SEQ, D_MODEL), row(mix_norm_g[0]))

    h, hn = _mixer(h, hn, 0, p, row(ffn_norm_g[0]), BF16)
    h, hn = _ffn(hn, dense_w_gate[0].astype(BF16), dense_w_up[0].astype(BF16),
                 dense_w_down[0].astype(BF16), h, row(mix_norm_g[1]))
    h, hn = _mixer(h, hn, 1, p, row(ffn_norm_g[1]), F32)
    out = _moe(h, hn, p, 0, row(final_norm_g))
    return out.reshape(1, SEQ, D_MODEL)
```

```python
import functools

import jax
import jax.numpy as jnp
from jax import lax
from jax.experimental import pallas as pl
from jax.experimental.pallas import tpu as pltpu

F32 = jnp.float32
BF16 = jnp.bfloat16

D_MODEL = 2048
SEQ = 8192
N_META = 16
N_HEADS = 8
HEAD_DIM = 128
ATTN_WIDTH = N_HEADS * HEAD_DIM
CONV_WIDTH = 1024
CONV_GROUPS = 8
CONV_KERNEL = 31
D_FF = 5632
N_EXPERTS = 8
D_EXPERT = 7168
RMS_EPS = 1e-6
LN_EPS = 1e-5
MASK_VALUE = -1e30
LOG2E = 1.4426950408889634

LANE = 128
SUBLANES = 8
BLK = 128
PAD = BLK - N_META
LP = PAD + N_META + SEQ
TM = 640
NT = LP // TM
NB = LP // BLK
HALO = 32
MOE_TM = 256
MOE_TILES = -(-(2 * LP + N_EXPERTS * (MOE_TM - 1)) // MOE_TM)
MOE_CAP = MOE_TILES * MOE_TM
MOE_TN_A = 1024
MOE_TN_B = 512
FFN_TF = 512
OUT_T = 512
VMEM_LIMIT = 56 * 1024 * 1024
HEADS_PER_STEP = 2
ATTN_HW = HEADS_PER_STEP * HEAD_DIM
ATTN_GROUPS = N_HEADS // HEADS_PER_STEP
ROW_DMA_PRIORITY = 1
W_SPLIT = 16
ROW_SPLIT = 8


def _cparams(sem):
    return pltpu.CompilerParams(dimension_semantics=sem, vmem_limit_bytes=VMEM_LIMIT)


def _rms(x, g):
    return x * lax.rsqrt(jnp.mean(x * x, axis=-1, keepdims=True) + RMS_EPS) * g


def _silu(x):
    return x * jax.nn.sigmoid(x)


def _prep_kernel(meta_ref, x_ref, g_ref, h_ref, hn_ref):
    i = pl.program_id(0)
    v = jnp.where(i == 0, meta_ref[...], x_ref[...])
    h_ref[...] = v
    hn_ref[...] = _rms(v, g_ref[...]).astype(BF16)


def _prep(meta_blk, x2d, g):
    return pl.pallas_call(
        _prep_kernel,
        grid=(NB,),
        in_specs=[
            pl.BlockSpec((BLK, D_MODEL), lambda i: (0, 0)),
            pl.BlockSpec((BLK, D_MODEL), lambda i: (jnp.maximum(i - 1, 0), 0)),
            pl.BlockSpec((1, D_MODEL), lambda i: (0, 0)),
        ],
        out_specs=[
            pl.BlockSpec((BLK, D_MODEL), lambda i: (i, 0)),
            pl.BlockSpec((BLK, D_MODEL), lambda i: (i, 0)),
        ],
        out_shape=[
            jax.ShapeDtypeStruct((LP, D_MODEL), F32),
            jax.ShapeDtypeStruct((LP, D_MODEL), BF16),
        ],
        compiler_params=_cparams(("arbitrary",)),
        name="prep",
    )(meta_blk, x2d, g)


def _qkv_kernel(x_ref, w_ref, o_ref):
    j = pl.program_id(0)
    acc = jnp.dot(x_ref[...], w_ref[...], preferred_element_type=F32)
    scale = jnp.where(j == 0, HEAD_DIM ** -0.5 * LOG2E, 1.0).astype(F32)
    res = (acc * scale).astype(BF16)
    for s in range(o_ref.shape[0]):
        o_ref[s] = res[:, s * ATTN_HW:(s + 1) * ATTN_HW]


def _qkv(hn, w_qkv):
    return pl.pallas_call(
        _qkv_kernel,
        grid=(3, NT),
        in_specs=[
            pl.BlockSpec((TM, D_MODEL), lambda j, i: (i, 0)),
            pl.BlockSpec((D_MODEL, ATTN_WIDTH), lambda j, i: (0, j)),
        ],
        out_specs=pl.BlockSpec((ATTN_GROUPS, TM, ATTN_HW), lambda j, i: (j, i, 0)),
        out_shape=jax.ShapeDtypeStruct((3 * ATTN_GROUPS, LP, ATTN_HW), BF16),
        compiler_params=_cparams(("arbitrary", "arbitrary")),
        name="qkv",
    )(hn, w_qkv)


def _glu_kernel(x_ref, wa_ref, wg_ref, o_ref):
    i = pl.program_id(0)
    x = x_ref[...]
    a = jnp.dot(x, wa_ref[...], preferred_element_type=F32)
    g = jnp.dot(x, wg_ref[...], preferred_element_type=F32)
    row = lax.broadcasted_iota(jnp.int32, a.shape, 0) + i * TM
    o_ref[...] = jnp.where(row >= PAD, a * jax.nn.sigmoid(g), 0.0)


def _glu(hn, w_a, w_g):
    return pl.pallas_call(
        _glu_kernel,
        grid=(NT,),
        in_specs=[
            pl.BlockSpec((TM, D_MODEL), lambda i: (i, 0)),
            pl.BlockSpec((D_MODEL, CONV_WIDTH), lambda i: (0, 0)),
            pl.BlockSpec((D_MODEL, CONV_WIDTH), lambda i: (0, 0)),
        ],
        out_specs=pl.BlockSpec((TM, CONV_WIDTH), lambda i: (i, 0)),
        out_shape=jax.ShapeDtypeStruct((LP, CONV_WIDTH), F32),
        compiler_params=_cparams(("arbitrary",)),
        name="glu",
    )(hn, w_a, w_g)


def _split3(x):
    hi = x.astype(BF16)
    r1 = x - hi.astype(F32)
    mid = r1.astype(BF16)
    lo = (r1 - mid.astype(F32)).astype(BF16)
    return hi, mid, lo


def _decay_kernel(x_ref, w_ref, b_ref, kb_ref, carry_ref):
    i = pl.program_id(0)

    @pl.when(i == 0)
    def _():
        carry_ref[...] = jnp.zeros_like(carry_ref)

    f = jnp.dot(x_ref[...], w_ref[...], preferred_element_type=F32) + b_ref[...]
    log_f = jnp.minimum(f, 0.0) - jnp.log1p(jnp.exp(-jnp.abs(f)))
    row = lax.broadcasted_iota(jnp.int32, log_f.shape, 0) + i * TM
    log_f = jnp.where(row >= PAD, log_f, 0.0)
    r = lax.broadcasted_iota(jnp.int32, (TM, TM), 0)
    c = lax.broadcasted_iota(jnp.int32, (TM, TM), 1)
    tri = jnp.where(c <= r, 1.0, 0.0).astype(BF16)
    hi, mid, lo = _split3(log_f)
    cs = (jnp.dot(tri, hi, preferred_element_type=F32)
          + jnp.dot(tri, mid, preferred_element_type=F32)
          + jnp.dot(tri, lo, preferred_element_type=F32)) + carry_ref[0:1, :]
    carry_ref[...] = jnp.broadcast_to(cs[TM - 1:TM, :], carry_ref.shape)
    lane = lax.broadcasted_iota(jnp.int32, (TM, LANE), 1)
    for h in range(N_HEADS):
        bias = jnp.where(row[:, h:h + 1] >= PAD, -LOG2E * cs[:, h:h + 1], MASK_VALUE)
        hi, mid, lo = (v.astype(F32) for v in _split3(jnp.broadcast_to(bias, (TM, LANE))))
        pieces = jnp.where(lane == 0, hi, jnp.where(lane == 1, mid, jnp.where(lane == 2, lo, 0.0)))
        kb_ref[h] = pieces.astype(BF16)


def _decay(hn, w_f, b_f):
    return pl.pallas_call(
        _decay_kernel,
        grid=(NT,),
        in_specs=[
            pl.BlockSpec((TM, D_MODEL), lambda i: (i, 0)),
            pl.BlockSpec((D_MODEL, LANE), lambda i: (0, 0)),
            pl.BlockSpec((1, LANE), lambda i: (0, 0)),
        ],
        out_specs=pl.BlockSpec((N_HEADS, TM, LANE), lambda i: (0, i, 0)),
        out_shape=jax.ShapeDtypeStruct((N_HEADS, LP, LANE), BF16),
        scratch_shapes=[pltpu.VMEM((8, LANE), F32)],
        compiler_params=_cparams(("arbitrary",)),
        name="decay",
    )(hn, w_f, b_f)


ATTN_STRIP = 16
ATTN_WIDE = 2 * TM
ATTN_PART = (512, ATTN_WIDE - 512)
assert sum(ATTN_PART) == ATTN_WIDE and ATTN_PART[-1] >= TM and all(w % 256 == 0 for w in ATTN_PART)


def _attn_kernel(q_ref, k_ref, v_ref, kb_ref, g_ref, o_ref, *state_refs):
    i = pl.program_id(1)
    n_state = len(state_refs) // HEADS_PER_STEP
    heads = []
    n_part = len(ATTN_PART)
    for h in range(HEADS_PER_STEP):
        refs = state_refs[n_state * h:n_state * (h + 1)]
        heads.append((refs[0], refs[1], refs[2], refs[3:3 + n_part], refs[3 + n_part:]))
    for m_ref, l_ref, acc_ref, _, _ in heads:
        m_ref[...] = jnp.full_like(m_ref, -1e38)
        l_ref[...] = jnp.zeros_like(l_ref)
        acc_ref[...] = jnp.zeros_like(acc_ref)

    lane = lax.broadcasted_iota(jnp.int32, (TM, HEAD_DIM), 1)
    q_ones = jnp.where(lane < 3, 1.0, 0.0).astype(BF16)

    def lanes(h):
        return slice(h * HEAD_DIM, (h + 1) * HEAD_DIM)

    def key_rows(key0, width):
        return pl.ds(pl.multiple_of(key0, LANE), width)

    def qk(h, key0, width, part):
        s_ref = heads[h][3][part]
        keys = key_rows(key0, width)
        q_aug = jnp.concatenate([q_ref[0, :, lanes(h)], q_ones], axis=1)
        k_aug = jnp.concatenate([k_ref[0, keys, lanes(h)], kb_ref[h, keys, :]], axis=1)
        s_ref[:, :width] = lax.dot_general(q_aug, k_aug, (((1,), (1,)), ((), ())),
                                           preferred_element_type=F32)

    def softmax_pv(h, key0, width, part, diagonal):
        m_ref, l_ref, acc_ref, s_refs, p_refs = heads[h]
        s_ref, p_ref = s_refs[part], p_refs[part]
        for r in range(TM // ATTN_STRIP):
            rs = slice(r * ATTN_STRIP, (r + 1) * ATTN_STRIP)
            nc = min(width, -(-((r + 1) * ATTN_STRIP) // LANE) * LANE) if diagonal else width

            def logits():
                s = s_ref[rs, :nc]
                if diagonal:
                    row = lax.broadcasted_iota(jnp.int32, s.shape, 0) + r * ATTN_STRIP
                    col = lax.broadcasted_iota(jnp.int32, s.shape, 1)
                    s = jnp.where(col <= row, s, MASK_VALUE)
                return s

            m_prev = m_ref[rs, :]
            m_new = jnp.maximum(m_prev, jnp.max(logits(), axis=-1, keepdims=True))
            p = jnp.exp2(logits() - jnp.concatenate([m_new] * (nc // LANE), axis=1))
            p_ref[rs, :nc] = p.astype(BF16)
            if nc < width:
                p_ref[rs, nc:width] = jnp.zeros((ATTN_STRIP, width - nc), BF16)
            alpha = jnp.exp2(m_prev - m_new)
            l_ref[rs, :] = alpha * l_ref[rs, :] + jnp.sum(p, axis=-1, keepdims=True)
            m_ref[rs, :] = m_new
            acc_ref[rs, :] = alpha * acc_ref[rs, :]
        acc_ref[...] += jnp.dot(p_ref[:, :width], v_ref[0, key_rows(key0, width), lanes(h)],
                                preferred_element_type=F32)

    def block(parts, diagonal=False):
        for key0, width, part in parts:
            for h in range(HEADS_PER_STEP):
                qk(h, key0, width, part)
        for key0, width, part in parts:
            for h in range(HEADS_PER_STEP):
                softmax_pv(h, key0, width, part, diagonal)

    def wide_block(j, carry):
        k0 = j * ATTN_WIDE
        block(tuple((k0 + sum(ATTN_PART[:part]), width, part) for part, width in enumerate(ATTN_PART)))
        return carry

    lax.fori_loop(0, i // 2, wide_block, 0)
    last = len(ATTN_PART) - 1

    @pl.when(i % 2 == 1)
    def _():
        block((((i - 1) * TM, TM, last),))

    block(((i * TM, TM, last),), diagonal=True)
    for h, (_, l_ref, acc_ref, _, _) in enumerate(heads):
        hs = slice(h * HEAD_DIM, (h + 1) * HEAD_DIM)
        o = acc_ref[...] / l_ref[...]
        o_ref[:, hs] = _rms(o, g_ref[h]).astype(BF16)


def _attention(qkv, key_bias, g_attn):
    hw = ATTN_HW
    nhp = ATTN_GROUPS
    return pl.pallas_call(
        _attn_kernel,
        grid=(nhp, NT),
        in_specs=[
            pl.BlockSpec((1, TM, hw), lambda hp, i: (hp, i, 0)),
            pl.BlockSpec((1, LP, hw), lambda hp, i: (nhp + hp, 0, 0)),
            pl.BlockSpec((1, LP, hw), lambda hp, i: (2 * nhp + hp, 0, 0)),
            pl.BlockSpec((HEADS_PER_STEP, LP, LANE), lambda hp, i: (hp, 0, 0)),
            pl.BlockSpec((HEADS_PER_STEP, 1, HEAD_DIM), lambda hp, i: (hp, 0, 0)),
        ],
        out_specs=pl.BlockSpec((TM, hw), lambda hp, i: (i, hp)),
        out_shape=jax.ShapeDtypeStruct((LP, ATTN_WIDTH), BF16),
        scratch_shapes=[
            pltpu.VMEM((TM, HEAD_DIM), F32),
            pltpu.VMEM((TM, HEAD_DIM), F32),
            pltpu.VMEM((TM, HEAD_DIM), F32),
            *[pltpu.VMEM((TM, w), F32) for w in ATTN_PART],
            *[pltpu.VMEM((TM, w), BF16) for w in ATTN_PART],
        ] * HEADS_PER_STEP,
        compiler_params=_cparams(("arbitrary", "arbitrary")),
        name="attention",
    )(qkv, qkv, qkv, key_bias, g_attn)


CONV_RC = 64
CONV_CC = 512
CONV_ROWS = HALO + TM - SUBLANES


def _conv_kernel(x_ref, halo_ref, wdw_ref, bdw_ref, lng_ref, lnb_ref, wpw_ref, og_ref, o_ref,
                 xs_ref, cv_ref):
    i = pl.program_id(0)
    xs_ref[0, 0:HALO, :] = jnp.where(i == 0, 0.0, halo_ref[...])
    xs_ref[0, HALO:HALO + TM, :] = x_ref[...]
    for s in range(1, SUBLANES):
        xs_ref[s, 0:CONV_ROWS, :] = xs_ref[0, s:s + CONV_ROWS, :]
    off = HALO - (CONV_KERNEL - 1)
    for rc in range(TM // CONV_RC):
        for cc in range(CONV_WIDTH // CONV_CC):
            cs = slice(cc * CONV_CC, (cc + 1) * CONV_CC)
            acc = jnp.broadcast_to(bdw_ref[:, cs], (CONV_RC, CONV_CC))
            for j in range(CONV_KERNEL):
                s, a = (off + j) % SUBLANES, (off + j) // SUBLANES * SUBLANES
                r0 = rc * CONV_RC + a
                acc = acc + wdw_ref[j:j + 1, cs] * xs_ref[s, r0:r0 + CONV_RC, cs]
            cv_ref[rc * CONV_RC:(rc + 1) * CONV_RC, cs] = acc
    y = cv_ref[...]
    mu = jnp.mean(y, axis=-1, keepdims=True)
    yc = y - mu
    var = jnp.mean(yc * yc, axis=-1, keepdims=True)
    a = _silu(yc * lax.rsqrt(var + LN_EPS) * lng_ref[...] + lnb_ref[...])
    z = jnp.dot(a.astype(BF16), wpw_ref[...], preferred_element_type=F32)
    gd = CONV_WIDTH // CONV_GROUPS
    for g in range(CONV_GROUPS):
        gs = slice(g * gd, (g + 1) * gd)
        o_ref[:, gs] = _rms(z[:, gs], og_ref[:, gs]).astype(BF16)


def _conv(hc, w_dw, b_dw, ln_g, ln_b, w_pw, out_g):
    row = lambda i: (0, 0)
    return pl.pallas_call(
        _conv_kernel,
        grid=(NT,),
        in_specs=[
            pl.BlockSpec((TM, CONV_WIDTH), lambda i: (i, 0)),
            pl.BlockSpec((HALO, CONV_WIDTH), lambda i: (jnp.maximum(i * (TM // HALO) - 1, 0), 0)),
            pl.BlockSpec((HALO, CONV_WIDTH), row),
            pl.BlockSpec((1, CONV_WIDTH), row),
            pl.BlockSpec((1, CONV_WIDTH), row),
            pl.BlockSpec((1, CONV_WIDTH), row),
            pl.BlockSpec((CONV_WIDTH, CONV_WIDTH), row),
            pl.BlockSpec((1, CONV_WIDTH), row),
        ],
        out_specs=pl.BlockSpec((TM, CONV_WIDTH), lambda i: (i, 0)),
        out_shape=jax.ShapeDtypeStruct((LP, CONV_WIDTH), BF16),
        scratch_shapes=[
            pltpu.VMEM((SUBLANES, HALO + TM, CONV_WIDTH), F32),
            pltpu.VMEM((TM, CONV_WIDTH), F32),
        ],
        compiler_params=_cparams(("arbitrary",)),
        name="conv",
    )(hc, hc, w_dw, b_dw, ln_g, ln_b, w_pw, out_g)


def _outproj_kernel(a_ref, c_ref, w_ref, h_ref, g_ref, ho_ref, hn_ref):
    acc = jnp.dot(a_ref[...], w_ref[0:ATTN_WIDTH, :], preferred_element_type=F32)
    acc = acc + jnp.dot(c_ref[...], w_ref[ATTN_WIDTH:, :], preferred_element_type=F32)
    h_new = h_ref[...] + acc
    ho_ref[...] = h_new
    hn_ref[...] = _rms(h_new, g_ref[...]).astype(hn_ref.dtype)


def _outproj(attn, conv, w_out, h, g_next, hn_dtype):
    return pl.pallas_call(
        _outproj_kernel,
        grid=(NT,),
        in_specs=[
            pl.BlockSpec((TM, ATTN_WIDTH), lambda i: (i, 0)),
            pl.BlockSpec((TM, CONV_WIDTH), lambda i: (i, 0)),
            pl.BlockSpec((D_MODEL, D_MODEL), lambda i: (0, 0)),
            pl.BlockSpec((TM, D_MODEL), lambda i: (i, 0)),
            pl.BlockSpec((1, D_MODEL), lambda i: (0, 0)),
        ],
        out_specs=[
            pl.BlockSpec((TM, D_MODEL), lambda i: (i, 0)),
            pl.BlockSpec((TM, D_MODEL), lambda i: (i, 0)),
        ],
        out_shape=[
            jax.ShapeDtypeStruct((LP, D_MODEL), F32),
            jax.ShapeDtypeStruct((LP, D_MODEL), hn_dtype),
        ],
        compiler_params=_cparams(("arbitrary",)),
        name="outproj",
    )(attn, conv, w_out, h, g_next)


def _ffn_kernel(x_ref, wg_ref, wu_ref, wd_ref, h_ref, g_ref, ho_ref, hn_ref, acc_ref):
    j = pl.program_id(1)

    @pl.when(j == 0)
    def _():
        acc_ref[...] = jnp.zeros_like(acc_ref)

    x = x_ref[...]
    g = jnp.dot(x, wg_ref[...], preferred_element_type=F32)
    u = jnp.dot(x, wu_ref[...], preferred_element_type=F32)
    a = (_silu(g) * u).astype(BF16)
    acc_ref[...] += jnp.dot(a, wd_ref[...], preferred_element_type=F32)

    @pl.when(j == pl.num_programs(1) - 1)
    def _():
        h_new = h_ref[...] + acc_ref[...]
        ho_ref[...] = h_new
        hn_ref[...] = _rms(h_new, g_ref[...]).astype(BF16)


def _ffn(hn, w_gate, w_up, w_down, h, g_next):
    return pl.pallas_call(
        _ffn_kernel,
        grid=(NT, D_FF // FFN_TF),
        in_specs=[
            pl.BlockSpec((TM, D_MODEL), lambda i, j: (i, 0)),
            pl.BlockSpec((D_MODEL, FFN_TF), lambda i, j: (0, j)),
            pl.BlockSpec((D_MODEL, FFN_TF), lambda i, j: (0, j)),
            pl.BlockSpec((FFN_TF, D_MODEL), lambda i, j: (j, 0)),
            pl.BlockSpec((TM, D_MODEL), lambda i, j: (i, 0)),
            pl.BlockSpec((1, D_MODEL), lambda i, j: (0, 0)),
        ],
        out_specs=[
            pl.BlockSpec((TM, D_MODEL), lambda i, j: (i, 0)),
            pl.BlockSpec((TM, D_MODEL), lambda i, j: (i, 0)),
        ],
        out_shape=[
            jax.ShapeDtypeStruct((LP, D_MODEL), F32),
            jax.ShapeDtypeStruct((LP, D_MODEL), BF16),
        ],
        scratch_shapes=[pltpu.VMEM((TM, D_MODEL), F32)],
        compiler_params=_cparams(("arbitrary", "arbitrary")),
        name="ffn",
    )(hn, w_gate, w_up, w_down, h, g_next)


INFO_E1, INFO_E2, INFO_G1, INFO_G2, INFO_R1, INFO_R2 = range(6)


def _router_kernel(x_ref, w_ref, info_ref, cnt_ref, carry_ref):
    i = pl.program_id(0)

    @pl.when(i == 0)
    def _():
        carry_ref[...] = jnp.zeros_like(carry_ref)

    logits = jnp.dot(x_ref[...], w_ref[...], preferred_element_type=F32,
                     precision=lax.Precision.HIGHEST)
    lane = lax.broadcasted_iota(jnp.int32, logits.shape, 1)
    neg = jnp.float32(-3e38)
    l1 = jnp.where(lane < N_EXPERTS, logits, neg)
    m1 = jnp.max(l1, axis=-1, keepdims=True)
    i1 = jnp.min(jnp.where(l1 == m1, lane, LANE), axis=-1, keepdims=True)
    l2 = jnp.where(lane == i1, neg, l1)
    m2 = jnp.max(l2, axis=-1, keepdims=True)
    i2 = jnp.min(jnp.where(l2 == m2, lane, LANE), axis=-1, keepdims=True)
    ex = jnp.exp(m2 - m1)
    g1 = 1.0 / (1.0 + ex)
    g2 = ex / (1.0 + ex)
    sel = jnp.where(lane == i1, 1.0, jnp.where(lane == i2, 1.0, 0.0))
    r = lax.broadcasted_iota(jnp.int32, (TM, TM), 0)
    c = lax.broadcasted_iota(jnp.int32, (TM, TM), 1)
    tri = jnp.where(c < r, 1.0, 0.0).astype(BF16)
    carry = carry_ref[0:1, :]
    rank = jnp.dot(tri, sel.astype(BF16), preferred_element_type=F32) + carry
    r1 = jnp.sum(jnp.where(lane == i1, rank, 0.0), axis=-1, keepdims=True)
    r2 = jnp.sum(jnp.where(lane == i2, rank, 0.0), axis=-1, keepdims=True)
    info = jnp.zeros_like(logits)
    for k, v in ((INFO_E1, i1.astype(F32)), (INFO_E2, i2.astype(F32)), (INFO_G1, g1),
                 (INFO_G2, g2), (INFO_R1, r1), (INFO_R2, r2)):
        info = jnp.where(lane == k, v, info)
    info_ref[...] = info
    total = carry + jnp.sum(sel, axis=0, keepdims=True)
    carry_ref[...] = jnp.broadcast_to(total, carry_ref.shape)
    cnt_ref[...] = jnp.broadcast_to(total, cnt_ref.shape)


def _router(hn, w_r):
    return pl.pallas_call(
        _router_kernel,
        grid=(NT,),
        in_specs=[
            pl.BlockSpec((TM, D_MODEL), lambda i: (i, 0)),
            pl.BlockSpec((D_MODEL, LANE), lambda i: (0, 0)),
        ],
        out_specs=[
            pl.BlockSpec((TM, LANE), lambda i: (i, 0)),
            pl.BlockSpec((8, LANE), lambda i: (0, 0)),
        ],
        out_shape=[
            jax.ShapeDtypeStruct((LP, LANE), F32),
            jax.ShapeDtypeStruct((8, LANE), F32),
        ],
        scratch_shapes=[pltpu.VMEM((8, LANE), F32)],
        compiler_params=_cparams(("arbitrary",)),
        name="router",
    )(hn, w_r)


def _dispatch_kernel(pos_ref, x_ref, xs_in_ref, xs_ref, sem):
    del xs_in_ref
    i = pl.program_id(0)

    def issue(r, carry):
        base = (i * BLK + r) * 2
        for k in range(2):
            pltpu.make_async_copy(x_ref.at[pl.ds(r, 1)],
                                  xs_ref.at[pl.ds(pos_ref[base + k], 1)], sem).start()
        return carry

    lax.fori_loop(0, BLK, issue, 0, unroll=4)
    for _ in range(2):
        pltpu.make_async_copy(x_ref, xs_ref.at[pl.ds(0, BLK)], sem).wait()


def _dispatch(pos, hn):
    grid_spec = pltpu.PrefetchScalarGridSpec(
        num_scalar_prefetch=1,
        grid=(NB,),
        in_specs=[
            pl.BlockSpec((BLK, D_MODEL), lambda i, pos: (i, 0)),
            pl.BlockSpec(memory_space=pl.ANY),
        ],
        out_specs=pl.BlockSpec(memory_space=pl.ANY),
        scratch_shapes=[pltpu.SemaphoreType.DMA(())],
    )
    return pl.pallas_call(
        _dispatch_kernel,
        grid_spec=grid_spec,
        out_shape=jax.ShapeDtypeStruct((MOE_CAP, D_MODEL), F32),
        input_output_aliases={2: 0},
        compiler_params=_cparams(("arbitrary",)),
        name="dispatch",
    )(pos, hn, jnp.zeros((MOE_CAP, D_MODEL), F32))


def _stream_expert_rows(gs_ref, gn_ref, w_hbm, w_stage, wsem, src_hbm, dst_hbm, tn, ibuf, obuf, isem, osem,
                        sbuf, sobuf, ssem, cast_weights, compute):
    j, e = pl.program_id(0), pl.program_id(1)
    nj, ne = pl.num_programs(0), pl.num_programs(1)
    g0 = gs_ref[e]
    n = gn_ref[e]
    cols = pl.ds(pl.multiple_of(j * tn, tn), tn)

    def w_copies(jj, ee):
        wcols = pl.ds(pl.multiple_of(jj * tn, tn), tn)
        copies = []
        for k in range(len(w_hbm)):
            chunk = w_stage[k].shape[0] // W_SPLIT
            for c in range(W_SPLIT):
                rows = pl.ds(c * chunk, chunk)
                copies.append(pltpu.make_async_copy(w_hbm[k].at[ee, rows, wcols], w_stage[k].at[rows],
                                                    wsem.at[k]))
        return copies

    def start_weights(jj, ee):
        for c, cp in enumerate(w_copies(jj, ee)):
            cp.start(priority=c % 2)

    class rows_in:
        def __init__(self, row0, nrows, buf, sem):
            chunk = nrows // ROW_SPLIT
            self.copies = [
                pltpu.make_async_copy(src_hbm.at[pl.ds(row0 + c * chunk, chunk)],
                                      buf.at[pl.ds(c * chunk, chunk)], sem)
                for c in range(ROW_SPLIT)]

        def start(self, priority):
            for cp in self.copies:
                cp.start(priority=priority)

        def wait(self):
            for cp in self.copies:
                cp.wait()

    pair = 2 * MOE_TM
    n2 = n // 2
    has_single = n - 2 * n2 == 1
    row0 = g0 * MOE_TM
    single_row = row0 + n2 * pair

    def in_copy(u, slot):
        return rows_in(row0 + u * pair, pair, ibuf.at[slot], isem.at[slot])

    def out_copy(u, slot):
        return pltpu.make_async_copy(obuf.at[slot], dst_hbm.at[pl.ds(row0 + u * pair, pair), cols],
                                     osem.at[slot])

    def single_in():
        return rows_in(single_row, MOE_TM, sbuf, ssem.at[0])

    def single_out(row):
        return pltpu.make_async_copy(sobuf, dst_hbm.at[pl.ds(row, MOE_TM), cols], ssem.at[1])

    @pl.when(jnp.logical_and(j == 0, e == 0))
    def _():
        start_weights(j, e)

    @pl.when(n2 > 0)
    def _():
        in_copy(0, 0).start(priority=ROW_DMA_PRIORITY)

    @pl.when(has_single)
    def _():
        single_in().start(priority=ROW_DMA_PRIORITY)

    for cp in w_copies(j, e):
        cp.wait()
    cast_weights()
    last_e = e == ne - 1

    @pl.when(jnp.logical_not(jnp.logical_and(j == nj - 1, last_e)))
    def _():
        start_weights(jnp.where(last_e, j + 1, j), jnp.where(last_e, 0, e + 1))

    def body(t, carry):
        slot = lax.rem(t, 2)
        in_copy(t, slot).wait()

        @pl.when(t + 1 < n2)
        def _():
            in_copy(t + 1, 1 - slot).start(priority=ROW_DMA_PRIORITY)

        out = compute(ibuf[slot])

        @pl.when(t >= 2)
        def _():
            out_copy(t - 2, slot).wait()

        obuf[slot] = out
        out_copy(t, slot).start(priority=ROW_DMA_PRIORITY)
        return carry

    lax.fori_loop(0, n2, body, 0)
    for back in (2, 1):
        @pl.when(n2 >= back)
        def _():
            out_copy(n2 - back, lax.rem(n2 - back, 2)).wait()

    @pl.when(has_single)
    def _():
        single_in().wait()
        sobuf[...] = compute(sbuf[...])
        cp = single_out(single_row)
        cp.start(priority=ROW_DMA_PRIORITY)
        cp.wait()

    @pl.when(last_e)
    def _():
        sobuf[...] = jnp.zeros(sobuf.shape, sobuf.dtype)

        def fill(t, carry):
            cp = single_out(t * MOE_TM)
            cp.start()
            cp.wait()
            return carry

        lax.fori_loop(g0 + n, MOE_TILES, fill, 0)


def _moe_up_kernel(gs_ref, gn_ref, wg_hbm, wu_hbm, xs_hbm, h1_hbm, wg_stage, wu_stage, wgb_ref, wub_ref,
                   xbuf, obuf, sbuf, sobuf, wsem, xsem, osem, ssem):
    def cast_weights():
        wgb_ref[...] = wg_stage[...].astype(BF16)
        wub_ref[...] = wu_stage[...].astype(BF16)

    def compute(x):
        xb = x.astype(BF16)
        g = jnp.dot(xb, wgb_ref[...], preferred_element_type=F32)
        u = jnp.dot(xb, wub_ref[...], preferred_element_type=F32)
        return (_silu(g) * u).astype(BF16)

    _stream_expert_rows(gs_ref, gn_ref, (wg_hbm, wu_hbm), (wg_stage, wu_stage), wsem, xs_hbm, h1_hbm,
                        MOE_TN_A, xbuf, obuf, xsem, osem, sbuf, sobuf, ssem, cast_weights, compute)


def _moe_up(gs, gn, xs, w_gate, w_up):
    any_spec = pl.BlockSpec(memory_space=pl.ANY)
    grid_spec = pltpu.PrefetchScalarGridSpec(
        num_scalar_prefetch=2,
        grid=(D_EXPERT // MOE_TN_A, N_EXPERTS),
        in_specs=[any_spec, any_spec, any_spec],
        out_specs=any_spec,
        scratch_shapes=[
            pltpu.VMEM((D_MODEL, MOE_TN_A), F32),
            pltpu.VMEM((D_MODEL, MOE_TN_A), F32),
            pltpu.VMEM((D_MODEL, MOE_TN_A), BF16),
            pltpu.VMEM((D_MODEL, MOE_TN_A), BF16),
            pltpu.VMEM((2, 2 * MOE_TM, D_MODEL), F32),
            pltpu.VMEM((2, 2 * MOE_TM, MOE_TN_A), BF16),
            pltpu.VMEM((MOE_TM, D_MODEL), F32),
            pltpu.VMEM((MOE_TM, MOE_TN_A), BF16),
            pltpu.SemaphoreType.DMA((2,)),
            pltpu.SemaphoreType.DMA((2,)),
            pltpu.SemaphoreType.DMA((2,)),
            pltpu.SemaphoreType.DMA((2,)),
        ],
    )
    return pl.pallas_call(
        _moe_up_kernel,
        grid_spec=grid_spec,
        out_shape=jax.ShapeDtypeStruct((MOE_CAP, D_EXPERT), BF16),
        compiler_params=_cparams(("arbitrary", "arbitrary")),
        name="moe_up",
    )(gs, gn, w_gate, w_up, xs)


def _moe_down_kernel(gs_ref, gn_ref, wd_hbm, h1_hbm, y_hbm, wd_stage, wdb_ref, hbuf, obuf, sbuf, sobuf,
                     wsem, hsem, osem, ssem):
    def cast_weights():
        wdb_ref[...] = wd_stage[...].astype(BF16)

    def compute(h):
        return jnp.dot(h, wdb_ref[...], preferred_element_type=F32)

    _stream_expert_rows(gs_ref, gn_ref, (wd_hbm,), (wd_stage,), wsem, h1_hbm, y_hbm, MOE_TN_B,
                        hbuf, obuf, hsem, osem, sbuf, sobuf, ssem, cast_weights, compute)


def _moe_down(gs, gn, h1, w_down):
    any_spec = pl.BlockSpec(memory_space=pl.ANY)
    grid_spec = pltpu.PrefetchScalarGridSpec(
        num_scalar_prefetch=2,
        grid=(D_MODEL // MOE_TN_B, N_EXPERTS),
        in_specs=[any_spec, any_spec],
        out_specs=any_spec,
        scratch_shapes=[
            pltpu.VMEM((D_EXPERT, MOE_TN_B), F32),
            pltpu.VMEM((D_EXPERT, MOE_TN_B), BF16),
            pltpu.VMEM((2, 2 * MOE_TM, D_EXPERT), BF16),
            pltpu.VMEM((2, 2 * MOE_TM, MOE_TN_B), F32),
            pltpu.VMEM((MOE_TM, D_EXPERT), BF16),
            pltpu.VMEM((MOE_TM, MOE_TN_B), F32),
            pltpu.SemaphoreType.DMA((1,)),
            pltpu.SemaphoreType.DMA((2,)),
            pltpu.SemaphoreType.DMA((2,)),
            pltpu.SemaphoreType.DMA((2,)),
        ],
    )
    return pl.pallas_call(
        _moe_down_kernel,
        grid_spec=grid_spec,
        out_shape=jax.ShapeDtypeStruct((MOE_CAP, D_MODEL), F32),
        compiler_params=_cparams(("arbitrary", "arbitrary")),
        name="moe_down",
    )(gs, gn, w_down, h1)


def _combine_kernel(pos_ref, info_ref, g_ref, h_hbm, y_hbm, o_ref, hbuf, ybuf, sem_h, sem_y):
    i = pl.program_id(0)
    slot = lax.rem(i, 2)

    def h_copy(t, s):
        return pltpu.make_async_copy(h_hbm.at[pl.ds(BLK + t * OUT_T, OUT_T)], hbuf.at[s], sem_h.at[s])

    def fetch(t, s):
        h_copy(t, s).start()

        def issue(r, carry):
            base = (t * OUT_T + r) * 2
            for k in range(2):
                pltpu.make_async_copy(y_hbm.at[pl.ds(pos_ref[base + k], 1)],
                                      ybuf.at[s, k, pl.ds(r, 1)], sem_y.at[s]).start()
            return carry

        lax.fori_loop(0, OUT_T, issue, 0, unroll=4)

    @pl.when(i == 0)
    def _():
        fetch(0, 0)

    @pl.when(i + 1 < pl.num_programs(0))
    def _():
        fetch(i + 1, 1 - slot)

    h_copy(i, slot).wait()
    for k in range(2):
        pltpu.make_async_copy(y_hbm.at[pl.ds(0, OUT_T)], ybuf.at[slot, k], sem_y.at[slot]).wait()
    g1 = info_ref[:, INFO_G1:INFO_G1 + 1]
    g2 = info_ref[:, INFO_G2:INFO_G2 + 1]
    h_new = hbuf[slot] + g1 * ybuf[slot, 0] + g2 * ybuf[slot, 1]
    o_ref[...] = _rms(h_new, g_ref[...])


def _combine(pos_x, info_x, g_final, h, y):
    grid_spec = pltpu.PrefetchScalarGridSpec(
        num_scalar_prefetch=1,
        grid=(SEQ // OUT_T,),
        in_specs=[
            pl.BlockSpec((OUT_T, LANE), lambda i, pos: (i, 0)),
            pl.BlockSpec((1, D_MODEL), lambda i, pos: (0, 0)),
            pl.BlockSpec(memory_space=pl.ANY),
            pl.BlockSpec(memory_space=pl.ANY),
        ],
        out_specs=pl.BlockSpec((OUT_T, D_MODEL), lambda i, pos: (i, 0)),
        scratch_shapes=[
            pltpu.VMEM((2, OUT_T, D_MODEL), F32),
            pltpu.VMEM((2, 2, OUT_T, D_MODEL), F32),
            pltpu.SemaphoreType.DMA((2,)),
            pltpu.SemaphoreType.DMA((2,)),
        ],
    )
    return pl.pallas_call(
        _combine_kernel,
        grid_spec=grid_spec,
        out_shape=jax.ShapeDtypeStruct((SEQ, D_MODEL), F32),
        compiler_params=_cparams(("arbitrary",)),
        name="combine",
    )(pos_x, info_x, g_final, h, y)


def _mixer(h, hn, layer, p, g_next, hn_dtype):
    w_in = p["w_in"][layer]
    a0 = 3 * ATTN_WIDTH
    u0 = a0 + N_HEADS
    w_qkv = w_in[:, :a0].astype(BF16)
    w_f = jnp.pad(w_in[:, a0:u0], ((0, 0), (0, LANE - N_HEADS))).astype(BF16)
    w_a = w_in[:, u0:u0 + CONV_WIDTH].astype(BF16)
    w_g = w_in[:, u0 + CONV_WIDTH:].astype(BF16)
    b_f = jnp.pad(p["b_forget"][layer], (0, LANE - N_HEADS)).reshape(1, LANE)

    qkv = _qkv(hn, w_qkv)
    key_bias = _decay(hn, w_f, b_f)
    attn = _attention(qkv, key_bias, p["attn_out_g"][layer].reshape(N_HEADS, 1, HEAD_DIM))

    hc = _glu(hn, w_a, w_g)
    w_dw = jnp.pad(p["w_dw"][layer], ((0, HALO - CONV_KERNEL), (0, 0)))
    row = lambda v: v.reshape(1, -1)
    conv = _conv(hc, w_dw, row(p["b_dw"][layer]), row(p["conv_ln_g"][layer]),
                 row(p["conv_ln_b"][layer]), p["w_conv_out"][layer].astype(BF16),
                 row(p["conv_out_g"][layer]))
    return _outproj(attn, conv, p["w_out"][layer].astype(BF16), h, g_next, hn_dtype)


def _moe(h, hn, p, j, g_final):
    w_r = jnp.pad(p["moe_w_router"][j], ((0, 0), (0, LANE - N_EXPERTS)))
    info, cnt = _router(hn, w_r)
    e1 = info[:, INFO_E1].astype(jnp.int32)
    e2 = info[:, INFO_E2].astype(jnp.int32)
    r1 = info[:, INFO_R1].astype(jnp.int32)
    r2 = info[:, INFO_R2].astype(jnp.int32)
    counts = cnt[0, :N_EXPERTS].astype(jnp.int32)
    padded = (counts + MOE_TM - 1) // MOE_TM * MOE_TM
    gend = jnp.cumsum(padded)
    gstart = gend - padded
    pos = jnp.stack([gstart[e1] + r1, gstart[e2] + r2], axis=-1).reshape(-1)
    gs = (gstart // MOE_TM).astype(jnp.int32)
    gn = (padded // MOE_TM).astype(jnp.int32)

    xs = _dispatch(pos, hn)
    h1 = _moe_up(gs, gn, xs, p["moe_w_gate"][j], p["moe_w_up"][j])
    y = _moe_down(gs, gn, h1, p["moe_w_down"][j])
    return _combine(pos[2 * BLK:], info[BLK:], g_final, h, y)


def kernel(x, meta_tokens, mix_norm_g, ffn_norm_g, w_in, b_forget, w_dw, b_dw, conv_ln_g, conv_ln_b, w_conv_out, attn_out_g, conv_out_g, w_out, dense_w_gate, dense_w_up, dense_w_down, moe_w_router, moe_w_gate, moe_w_up, moe_w_down, final_norm_g):
    assert x.shape == (1, SEQ, D_MODEL) and meta_tokens.shape == (N_META, D_MODEL)
    p = dict(w_in=w_in, b_forget=b_forget, w_dw=w_dw, b_dw=b_dw, conv_ln_g=conv_ln_g,
             conv_ln_b=conv_ln_b, w_conv_out=w_conv_out, attn_out_g=attn_out_g,
             conv_out_g=conv_out_g, w_out=w_out, moe_w_router=moe_w_router,
             moe_w_gate=moe_w_gate, moe_w_up=moe_w_up, moe_w_down=moe_w_down)
    row = lambda v: v.reshape(1, D_MODEL)
    meta_blk = jnp.pad(meta_tokens.astype(F32), ((PAD, 0), (0, 0)))
    h, hn = _prep(meta_blk, x.reshape(SEQ, D_MODEL), row(mix_norm_g[0]))

    h, hn = _mixer(h, hn, 0, p, row(ffn_norm_g[0]), BF16)
    h, hn = _ffn(hn, dense_w_gate[0].astype(BF16), dense_w_up[0].astype(BF16),
                 dense_w_down[0].astype(BF16), h, row(mix_norm_g[1]))
    h, hn = _mixer(h, hn, 1, p, row(ffn_norm_g[1]), F32)
    out = _moe(h, hn, p, 0, row(final_norm_g))
    return out.reshape(1, SEQ, D_MODEL)
```

```python
import functools

import jax
import jax.numpy as jnp
from jax import lax
from jax.experimental import pallas as pl
from jax.experimental.pallas import tpu as pltpu

F32 = jnp.float32
BF16 = jnp.bfloat16

D_MODEL = 2048
SEQ = 8192
N_META = 16
N_HEADS = 8
HEAD_DIM = 128
ATTN_WIDTH = N_HEADS * HEAD_DIM
CONV_WIDTH = 1024
CONV_GROUPS = 8
CONV_KERNEL = 31
D_FF = 5632
N_EXPERTS = 8
D_EXPERT = 7168
RMS_EPS = 1e-6
LN_EPS = 1e-5
MASK_VALUE = -1e30
LOG2E = 1.4426950408889634

LANE = 128
SUBLANES = 8
BLK = 128
PAD = BLK - N_META
LP = PAD + N_META + SEQ
TM = 640
NT = LP // TM
NB = LP // BLK
HALO = 32
MOE_TM = 256
MOE_TILES = -(-(2 * LP + N_EXPERTS * (MOE_TM - 1)) // MOE_TM)
MOE_CAP = MOE_TILES * MOE_TM
MOE_TN_A = 1024
MOE_TN_B = 512
FFN_TF = 512
OUT_T = 512
VMEM_LIMIT = 56 * 1024 * 1024
HEADS_PER_STEP = 2
ATTN_HW = HEADS_PER_STEP * HEAD_DIM
ATTN_GROUPS = N_HEADS // HEADS_PER_STEP
ROW_DMA_PRIORITY = 1
W_SPLIT = 32
ROW_SPLIT = 8


def _cparams(sem):
    return pltpu.CompilerParams(dimension_semantics=sem, vmem_limit_bytes=VMEM_LIMIT)


def _rms(x, g):
    return x * lax.rsqrt(jnp.mean(x * x, axis=-1, keepdims=True) + RMS_EPS) * g


def _silu(x):
    return x * jax.nn.sigmoid(x)


def _prep_kernel(meta_ref, x_ref, g_ref, h_ref, hn_ref):
    i = pl.program_id(0)
    v = jnp.where(i == 0, meta_ref[...], x_ref[...])
    h_ref[...] = v
    hn_ref[...] = _rms(v, g_ref[...]).astype(BF16)


def _prep(meta_blk, x2d, g):
    return pl.pallas_call(
        _prep_kernel,
        grid=(NB,),
        in_specs=[
            pl.BlockSpec((BLK, D_MODEL), lambda i: (0, 0)),
            pl.BlockSpec((BLK, D_MODEL), lambda i: (jnp.maximum(i - 1, 0), 0)),
            pl.BlockSpec((1, D_MODEL), lambda i: (0, 0)),
        ],
        out_specs=[
            pl.BlockSpec((BLK, D_MODEL), lambda i: (i, 0)),
            pl.BlockSpec((BLK, D_MODEL), lambda i: (i, 0)),
        ],
        out_shape=[
            jax.ShapeDtypeStruct((LP, D_MODEL), F32),
            jax.ShapeDtypeStruct((LP, D_MODEL), BF16),
        ],
        compiler_params=_cparams(("arbitrary",)),
        name="prep",
    )(meta_blk, x2d, g)


def _qkv_kernel(x_ref, w_ref, o_ref):
    j = pl.program_id(0)
    acc = jnp.dot(x_ref[...], w_ref[...], preferred_element_type=F32)
    scale = jnp.where(j == 0, HEAD_DIM ** -0.5 * LOG2E, 1.0).astype(F32)
    res = (acc * scale).astype(BF16)
    for s in range(o_ref.shape[0]):
        o_ref[s] = res[:, s * ATTN_HW:(s + 1) * ATTN_HW]


def _qkv(hn, w_qkv):
    return pl.pallas_call(
        _qkv_kernel,
        grid=(3, NT),
        in_specs=[
            pl.BlockSpec((TM, D_MODEL), lambda j, i: (i, 0)),
            pl.BlockSpec((D_MODEL, ATTN_WIDTH), lambda j, i: (0, j)),
        ],
        out_specs=pl.BlockSpec((ATTN_GROUPS, TM, ATTN_HW), lambda j, i: (j, i, 0)),
        out_shape=jax.ShapeDtypeStruct((3 * ATTN_GROUPS, LP, ATTN_HW), BF16),
        compiler_params=_cparams(("arbitrary", "arbitrary")),
        name="qkv",
    )(hn, w_qkv)


def _glu_kernel(x_ref, wa_ref, wg_ref, o_ref):
    i = pl.program_id(0)
    x = x_ref[...]
    a = jnp.dot(x, wa_ref[...], preferred_element_type=F32)
    g = jnp.dot(x, wg_ref[...], preferred_element_type=F32)
    row = lax.broadcasted_iota(jnp.int32, a.shape, 0) + i * TM
    o_ref[...] = jnp.where(row >= PAD, a * jax.nn.sigmoid(g), 0.0)


def _glu(hn, w_a, w_g):
    return pl.pallas_call(
        _glu_kernel,
        grid=(NT,),
        in_specs=[
            pl.BlockSpec((TM, D_MODEL), lambda i: (i, 0)),
            pl.BlockSpec((D_MODEL, CONV_WIDTH), lambda i: (0, 0)),
            pl.BlockSpec((D_MODEL, CONV_WIDTH), lambda i: (0, 0)),
        ],
        out_specs=pl.BlockSpec((TM, CONV_WIDTH), lambda i: (i, 0)),
        out_shape=jax.ShapeDtypeStruct((LP, CONV_WIDTH), F32),
        compiler_params=_cparams(("arbitrary",)),
        name="glu",
    )(hn, w_a, w_g)


def _split3(x):
    hi = x.astype(BF16)
    r1 = x - hi.astype(F32)
    mid = r1.astype(BF16)
    lo = (r1 - mid.astype(F32)).astype(BF16)
    return hi, mid, lo


def _decay_kernel(x_ref, w_ref, b_ref, kb_ref, carry_ref):
    i = pl.program_id(0)

    @pl.when(i == 0)
    def _():
        carry_ref[...] = jnp.zeros_like(carry_ref)

    f = jnp.dot(x_ref[...], w_ref[...], preferred_element_type=F32) + b_ref[...]
    log_f = jnp.minimum(f, 0.0) - jnp.log1p(jnp.exp(-jnp.abs(f)))
    row = lax.broadcasted_iota(jnp.int32, log_f.shape, 0) + i * TM
    log_f = jnp.where(row >= PAD, log_f, 0.0)
    r = lax.broadcasted_iota(jnp.int32, (TM, TM), 0)
    c = lax.broadcasted_iota(jnp.int32, (TM, TM), 1)
    tri = jnp.where(c <= r, 1.0, 0.0).astype(BF16)
    hi, mid, lo = _split3(log_f)
    cs = (jnp.dot(tri, hi, preferred_element_type=F32)
          + jnp.dot(tri, mid, preferred_element_type=F32)
          + jnp.dot(tri, lo, preferred_element_type=F32)) + carry_ref[0:1, :]
    carry_ref[...] = jnp.broadcast_to(cs[TM - 1:TM, :], carry_ref.shape)
    lane = lax.broadcasted_iota(jnp.int32, (TM, LANE), 1)
    for h in range(N_HEADS):
        bias = jnp.where(row[:, h:h + 1] >= PAD, -LOG2E * cs[:, h:h + 1], MASK_VALUE)
        hi, mid, lo = (v.astype(F32) for v in _split3(jnp.broadcast_to(bias, (TM, LANE))))
        pieces = jnp.where(lane == 0, hi, jnp.where(lane == 1, mid, jnp.where(lane == 2, lo, 0.0)))
        kb_ref[h] = pieces.astype(BF16)


def _decay(hn, w_f, b_f):
    return pl.pallas_call(
        _decay_kernel,
        grid=(NT,),
        in_specs=[
            pl.BlockSpec((TM, D_MODEL), lambda i: (i, 0)),
            pl.BlockSpec((D_MODEL, LANE), lambda i: (0, 0)),
            pl.BlockSpec((1, LANE), lambda i: (0, 0)),
        ],
        out_specs=pl.BlockSpec((N_HEADS, TM, LANE), lambda i: (0, i, 0)),
        out_shape=jax.ShapeDtypeStruct((N_HEADS, LP, LANE), BF16),
        scratch_shapes=[pltpu.VMEM((8, LANE), F32)],
        compiler_params=_cparams(("arbitrary",)),
        name="decay",
    )(hn, w_f, b_f)


ATTN_STRIP = 16
ATTN_WIDE = 2 * TM
ATTN_PART = (512, ATTN_WIDE - 512)
assert sum(ATTN_PART) == ATTN_WIDE and ATTN_PART[-1] >= TM and all(w % 256 == 0 for w in ATTN_PART)


def _attn_kernel(q_ref, k_ref, v_ref, kb_ref, g_ref, o_ref, *state_refs):
    i = pl.program_id(1)
    n_state = len(state_refs) // HEADS_PER_STEP
    heads = []
    n_part = len(ATTN_PART)
    for h in range(HEADS_PER_STEP):
        refs = state_refs[n_state * h:n_state * (h + 1)]
        heads.append((refs[0], refs[1], refs[2], refs[3:3 + n_part], refs[3 + n_part:]))
    for m_ref, l_ref, acc_ref, _, _ in heads:
        m_ref[...] = jnp.full_like(m_ref, -1e38)
        l_ref[...] = jnp.zeros_like(l_ref)
        acc_ref[...] = jnp.zeros_like(acc_ref)

    lane = lax.broadcasted_iota(jnp.int32, (TM, HEAD_DIM), 1)
    q_ones = jnp.where(lane < 3, 1.0, 0.0).astype(BF16)

    def lanes(h):
        return slice(h * HEAD_DIM, (h + 1) * HEAD_DIM)

    def key_rows(key0, width):
        return pl.ds(pl.multiple_of(key0, LANE), width)

    def qk(h, key0, width, part):
        s_ref = heads[h][3][part]
        keys = key_rows(key0, width)
        q_aug = jnp.concatenate([q_ref[0, :, lanes(h)], q_ones], axis=1)
        k_aug = jnp.concatenate([k_ref[0, keys, lanes(h)], kb_ref[h, keys, :]], axis=1)
        s_ref[:, :width] = lax.dot_general(q_aug, k_aug, (((1,), (1,)), ((), ())),
                                           preferred_element_type=F32)

    def softmax_pv(h, key0, width, part, diagonal):
        m_ref, l_ref, acc_ref, s_refs, p_refs = heads[h]
        s_ref, p_ref = s_refs[part], p_refs[part]
        for r in range(TM // ATTN_STRIP):
            rs = slice(r * ATTN_STRIP, (r + 1) * ATTN_STRIP)
            nc = min(width, -(-((r + 1) * ATTN_STRIP) // LANE) * LANE) if diagonal else width

            def logits():
                s = s_ref[rs, :nc]
                if diagonal:
                    row = lax.broadcasted_iota(jnp.int32, s.shape, 0) + r * ATTN_STRIP
                    col = lax.broadcasted_iota(jnp.int32, s.shape, 1)
                    s = jnp.where(col <= row, s, MASK_VALUE)
                return s

            m_prev = m_ref[rs, :]
            m_new = jnp.maximum(m_prev, jnp.max(logits(), axis=-1, keepdims=True))
            p = jnp.exp2(logits() - jnp.concatenate([m_new] * (nc // LANE), axis=1))
            p_ref[rs, :nc] = p.astype(BF16)
            if nc < width:
                p_ref[rs, nc:width] = jnp.zeros((ATTN_STRIP, width - nc), BF16)
            alpha = jnp.exp2(m_prev - m_new)
            l_ref[rs, :] = alpha * l_ref[rs, :] + jnp.sum(p, axis=-1, keepdims=True)
            m_ref[rs, :] = m_new
            acc_ref[rs, :] = alpha * acc_ref[rs, :]
        acc_ref[...] += jnp.dot(p_ref[:, :width], v_ref[0, key_rows(key0, width), lanes(h)],
                                preferred_element_type=F32)

    def block(parts, diagonal=False):
        for key0, width, part in parts:
            for h in range(HEADS_PER_STEP):
                qk(h, key0, width, part)
        for key0, width, part in parts:
            for h in range(HEADS_PER_STEP):
                softmax_pv(h, key0, width, part, diagonal)

    def wide_block(j, carry):
        k0 = j * ATTN_WIDE
        block(tuple((k0 + sum(ATTN_PART[:part]), width, part) for part, width in enumerate(ATTN_PART)))
        return carry

    lax.fori_loop(0, i // 2, wide_block, 0)
    last = len(ATTN_PART) - 1

    @pl.when(i % 2 == 1)
    def _():
        block((((i - 1) * TM, TM, last),))

    block(((i * TM, TM, last),), diagonal=True)
    for h, (_, l_ref, acc_ref, _, _) in enumerate(heads):
        hs = slice(h * HEAD_DIM, (h + 1) * HEAD_DIM)
        o = acc_ref[...] / l_ref[...]
        o_ref[:, hs] = _rms(o, g_ref[h]).astype(BF16)


def _attention(qkv, key_bias, g_attn):
    hw = ATTN_HW
    nhp = ATTN_GROUPS
    return pl.pallas_call(
        _attn_kernel,
        grid=(nhp, NT),
        in_specs=[
            pl.BlockSpec((1, TM, hw), lambda hp, i: (hp, i, 0)),
            pl.BlockSpec((1, LP, hw), lambda hp, i: (nhp + hp, 0, 0)),
            pl.BlockSpec((1, LP, hw), lambda hp, i: (2 * nhp + hp, 0, 0)),
            pl.BlockSpec((HEADS_PER_STEP, LP, LANE), lambda hp, i: (hp, 0, 0)),
            pl.BlockSpec((HEADS_PER_STEP, 1, HEAD_DIM), lambda hp, i: (hp, 0, 0)),
        ],
        out_specs=pl.BlockSpec((TM, hw), lambda hp, i: (i, hp)),
        out_shape=jax.ShapeDtypeStruct((LP, ATTN_WIDTH), BF16),
        scratch_shapes=[
            pltpu.VMEM((TM, HEAD_DIM), F32),
            pltpu.VMEM((TM, HEAD_DIM), F32),
            pltpu.VMEM((TM, HEAD_DIM), F32),
            *[pltpu.VMEM((TM, w), F32) for w in ATTN_PART],
            *[pltpu.VMEM((TM, w), BF16) for w in ATTN_PART],
        ] * HEADS_PER_STEP,
        compiler_params=_cparams(("arbitrary", "arbitrary")),
        name="attention",
    )(qkv, qkv, qkv, key_bias, g_attn)


CONV_RC = 64
CONV_CC = 512
CONV_ROWS = HALO + TM - SUBLANES


def _conv_kernel(x_ref, halo_ref, wdw_ref, bdw_ref, lng_ref, lnb_ref, wpw_ref, og_ref, o_ref,
                 xs_ref, cv_ref):
    i = pl.program_id(0)
    xs_ref[0, 0:HALO, :] = jnp.where(i == 0, 0.0, halo_ref[...])
    xs_ref[0, HALO:HALO + TM, :] = x_ref[...]
    for s in range(1, SUBLANES):
        xs_ref[s, 0:CONV_ROWS, :] = xs_ref[0, s:s + CONV_ROWS, :]
    off = HALO - (CONV_KERNEL - 1)
    for rc in range(TM // CONV_RC):
        for cc in range(CONV_WIDTH // CONV_CC):
            cs = slice(cc * CONV_CC, (cc + 1) * CONV_CC)
            acc = jnp.broadcast_to(bdw_ref[:, cs], (CONV_RC, CONV_CC))
            for j in range(CONV_KERNEL):
                s, a = (off + j) % SUBLANES, (off + j) // SUBLANES * SUBLANES
                r0 = rc * CONV_RC + a
                acc = acc + wdw_ref[j:j + 1, cs] * xs_ref[s, r0:r0 + CONV_RC, cs]
            cv_ref[rc * CONV_RC:(rc + 1) * CONV_RC, cs] = acc
    y = cv_ref[...]
    mu = jnp.mean(y, axis=-1, keepdims=True)
    yc = y - mu
    var = jnp.mean(yc * yc, axis=-1, keepdims=True)
    a = _silu(yc * lax.rsqrt(var + LN_EPS) * lng_ref[...] + lnb_ref[...])
    z = jnp.dot(a.astype(BF16), wpw_ref[...], preferred_element_type=F32)
    gd = CONV_WIDTH // CONV_GROUPS
    for g in range(CONV_GROUPS):
        gs = slice(g * gd, (g + 1) * gd)
        o_ref[:, gs] = _rms(z[:, gs], og_ref[:, gs]).astype(BF16)


def _conv(hc, w_dw, b_dw, ln_g, ln_b, w_pw, out_g):
    row = lambda i: (0, 0)
    return pl.pallas_call(
        _conv_kernel,
        grid=(NT,),
        in_specs=[
            pl.BlockSpec((TM, CONV_WIDTH), lambda i: (i, 0)),
            pl.BlockSpec((HALO, CONV_WIDTH), lambda i: (jnp.maximum(i * (TM // HALO) - 1, 0), 0)),
            pl.BlockSpec((HALO, CONV_WIDTH), row),
            pl.BlockSpec((1, CONV_WIDTH), row),
            pl.BlockSpec((1, CONV_WIDTH), row),
            pl.BlockSpec((1, CONV_WIDTH), row),
            pl.BlockSpec((CONV_WIDTH, CONV_WIDTH), row),
            pl.BlockSpec((1, CONV_WIDTH), row),
        ],
        out_specs=pl.BlockSpec((TM, CONV_WIDTH), lambda i: (i, 0)),
        out_shape=jax.ShapeDtypeStruct((LP, CONV_WIDTH), BF16),
        scratch_shapes=[
            pltpu.VMEM((SUBLANES, HALO + TM, CONV_WIDTH), F32),
            pltpu.VMEM((TM, CONV_WIDTH), F32),
        ],
        compiler_params=_cparams(("arbitrary",)),
        name="conv",
    )(hc, hc, w_dw, b_dw, ln_g, ln_b, w_pw, out_g)


def _outproj_kernel(a_ref, c_ref, w_ref, h_ref, g_ref, ho_ref, hn_ref):
    acc = jnp.dot(a_ref[...], w_ref[0:ATTN_WIDTH, :], preferred_element_type=F32)
    acc = acc + jnp.dot(c_ref[...], w_ref[ATTN_WIDTH:, :], preferred_element_type=F32)
    h_new = h_ref[...] + acc
    ho_ref[...] = h_new
    hn_ref[...] = _rms(h_new, g_ref[...]).astype(hn_ref.dtype)


def _outproj(attn, conv, w_out, h, g_next, hn_dtype):
    return pl.pallas_call(
        _outproj_kernel,
        grid=(NT,),
        in_specs=[
            pl.BlockSpec((TM, ATTN_WIDTH), lambda i: (i, 0)),
            pl.BlockSpec((TM, CONV_WIDTH), lambda i: (i, 0)),
            pl.BlockSpec((D_MODEL, D_MODEL), lambda i: (0, 0)),
            pl.BlockSpec((TM, D_MODEL), lambda i: (i, 0)),
            pl.BlockSpec((1, D_MODEL), lambda i: (0, 0)),
        ],
        out_specs=[
            pl.BlockSpec((TM, D_MODEL), lambda i: (i, 0)),
            pl.BlockSpec((TM, D_MODEL), lambda i: (i, 0)),
        ],
        out_shape=[
            jax.ShapeDtypeStruct((LP, D_MODEL), F32),
            jax.ShapeDtypeStruct((LP, D_MODEL), hn_dtype),
        ],
        compiler_params=_cparams(("arbitrary",)),
        name="outproj",
    )(attn, conv, w_out, h, g_next)


def _ffn_kernel(x_ref, wg_ref, wu_ref, wd_ref, h_ref, g_ref, ho_ref, hn_ref, acc_ref):
    j = pl.program_id(1)

    @pl.when(j == 0)
    def _():
        acc_ref[...] = jnp.zeros_like(acc_ref)

    x = x_ref[...]
    g = jnp.dot(x, wg_ref[...], preferred_element_type=F32)
    u = jnp.dot(x, wu_ref[...], preferred_element_type=F32)
    a = (_silu(g) * u).astype(BF16)
    acc_ref[...] += jnp.dot(a, wd_ref[...], preferred_element_type=F32)

    @pl.when(j == pl.num_programs(1) - 1)
    def _():
        h_new = h_ref[...] + acc_ref[...]
        ho_ref[...] = h_new
        hn_ref[...] = _rms(h_new, g_ref[...]).astype(BF16)


def _ffn(hn, w_gate, w_up, w_down, h, g_next):
    return pl.pallas_call(
        _ffn_kernel,
        grid=(NT, D_FF // FFN_TF),
        in_specs=[
            pl.BlockSpec((TM, D_MODEL), lambda i, j: (i, 0)),
            pl.BlockSpec((D_MODEL, FFN_TF), lambda i, j: (0, j)),
            pl.BlockSpec((D_MODEL, FFN_TF), lambda i, j: (0, j)),
            pl.BlockSpec((FFN_TF, D_MODEL), lambda i, j: (j, 0)),
            pl.BlockSpec((TM, D_MODEL), lambda i, j: (i, 0)),
            pl.BlockSpec((1, D_MODEL), lambda i, j: (0, 0)),
        ],
        out_specs=[
            pl.BlockSpec((TM, D_MODEL), lambda i, j: (i, 0)),
            pl.BlockSpec((TM, D_MODEL), lambda i, j: (i, 0)),
        ],
        out_shape=[
            jax.ShapeDtypeStruct((LP, D_MODEL), F32),
            jax.ShapeDtypeStruct((LP, D_MODEL), BF16),
        ],
        scratch_shapes=[pltpu.VMEM((TM, D_MODEL), F32)],
        compiler_params=_cparams(("arbitrary", "arbitrary")),
        name="ffn",
    )(hn, w_gate, w_up, w_down, h, g_next)


INFO_E1, INFO_E2, INFO_G1, INFO_G2, INFO_R1, INFO_R2 = range(6)


def _router_kernel(x_ref, w_ref, info_ref, cnt_ref, carry_ref):
    i = pl.program_id(0)

    @pl.when(i == 0)
    def _():
        carry_ref[...] = jnp.zeros_like(carry_ref)

    logits = jnp.dot(x_ref[...], w_ref[...], preferred_element_type=F32,
                     precision=lax.Precision.HIGHEST)
    lane = lax.broadcasted_iota(jnp.int32, logits.shape, 1)
    neg = jnp.float32(-3e38)
    l1 = jnp.where(lane < N_EXPERTS, logits, neg)
    m1 = jnp.max(l1, axis=-1, keepdims=True)
    i1 = jnp.min(jnp.where(l1 == m1, lane, LANE), axis=-1, keepdims=True)
    l2 = jnp.where(lane == i1, neg, l1)
    m2 = jnp.max(l2, axis=-1, keepdims=True)
    i2 = jnp.min(jnp.where(l2 == m2, lane, LANE), axis=-1, keepdims=True)
    ex = jnp.exp(m2 - m1)
    g1 = 1.0 / (1.0 + ex)
    g2 = ex / (1.0 + ex)
    sel = jnp.where(lane == i1, 1.0, jnp.where(lane == i2, 1.0, 0.0))
    r = lax.broadcasted_iota(jnp.int32, (TM, TM), 0)
    c = lax.broadcasted_iota(jnp.int32, (TM, TM), 1)
    tri = jnp.where(c < r, 1.0, 0.0).astype(BF16)
    carry = carry_ref[0:1, :]
    rank = jnp.dot(tri, sel.astype(BF16), preferred_element_type=F32) + carry
    r1 = jnp.sum(jnp.where(lane == i1, rank, 0.0), axis=-1, keepdims=True)
    r2 = jnp.sum(jnp.where(lane == i2, rank, 0.0), axis=-1, keepdims=True)
    info = jnp.zeros_like(logits)
    for k, v in ((INFO_E1, i1.astype(F32)), (INFO_E2, i2.astype(F32)), (INFO_G1, g1),
                 (INFO_G2, g2), (INFO_R1, r1), (INFO_R2, r2)):
        info = jnp.where(lane == k, v, info)
    info_ref[...] = info
    total = carry + jnp.sum(sel, axis=0, keepdims=True)
    carry_ref[...] = jnp.broadcast_to(total, carry_ref.shape)
    cnt_ref[...] = jnp.broadcast_to(total, cnt_ref.shape)


def _router(hn, w_r):
    return pl.pallas_call(
        _router_kernel,
        grid=(NT,),
        in_specs=[
            pl.BlockSpec((TM, D_MODEL), lambda i: (i, 0)),
            pl.BlockSpec((D_MODEL, LANE), lambda i: (0, 0)),
        ],
        out_specs=[
            pl.BlockSpec((TM, LANE), lambda i: (i, 0)),
            pl.BlockSpec((8, LANE), lambda i: (0, 0)),
        ],
        out_shape=[
            jax.ShapeDtypeStruct((LP, LANE), F32),
            jax.ShapeDtypeStruct((8, LANE), F32),
        ],
        scratch_shapes=[pltpu.VMEM((8, LANE), F32)],
        compiler_params=_cparams(("arbitrary",)),
        name="router",
    )(hn, w_r)


def _dispatch_kernel(pos_ref, x_ref, xs_in_ref, xs_ref, sem):
    del xs_in_ref
    i = pl.program_id(0)

    def issue(r, carry):
        base = (i * BLK + r) * 2
        for k in range(2):
            pltpu.make_async_copy(x_ref.at[pl.ds(r, 1)],
                                  xs_ref.at[pl.ds(pos_ref[base + k], 1)], sem).start()
        return carry

    lax.fori_loop(0, BLK, issue, 0, unroll=4)
    for _ in range(2):
        pltpu.make_async_copy(x_ref, xs_ref.at[pl.ds(0, BLK)], sem).wait()


def _dispatch(pos, hn):
    grid_spec = pltpu.PrefetchScalarGridSpec(
        num_scalar_prefetch=1,
        grid=(NB,),
        in_specs=[
            pl.BlockSpec((BLK, D_MODEL), lambda i, pos: (i, 0)),
            pl.BlockSpec(memory_space=pl.ANY),
        ],
        out_specs=pl.BlockSpec(memory_space=pl.ANY),
        scratch_shapes=[pltpu.SemaphoreType.DMA(())],
    )
    return pl.pallas_call(
        _dispatch_kernel,
        grid_spec=grid_spec,
        out_shape=jax.ShapeDtypeStruct((MOE_CAP, D_MODEL), F32),
        input_output_aliases={2: 0},
        compiler_params=_cparams(("arbitrary",)),
        name="dispatch",
    )(pos, hn, jnp.zeros((MOE_CAP, D_MODEL), F32))


def _stream_expert_rows(gs_ref, gn_ref, w_hbm, w_stage, wsem, src_hbm, dst_hbm, tn, ibuf, obuf, isem, osem,
                        sbuf, sobuf, ssem, cast_weights, compute):
    j, e = pl.program_id(0), pl.program_id(1)
    nj, ne = pl.num_programs(0), pl.num_programs(1)
    g0 = gs_ref[e]
    n = gn_ref[e]
    cols = pl.ds(pl.multiple_of(j * tn, tn), tn)

    def w_copies(jj, ee):
        wcols = pl.ds(pl.multiple_of(jj * tn, tn), tn)
        copies = []
        for k in range(len(w_hbm)):
            chunk = w_stage[k].shape[0] // W_SPLIT
            for c in range(W_SPLIT):
                rows = pl.ds(c * chunk, chunk)
                copies.append(pltpu.make_async_copy(w_hbm[k].at[ee, rows, wcols], w_stage[k].at[rows],
                                                    wsem.at[k]))
        return copies

    def start_weights(jj, ee):
        for c, cp in enumerate(w_copies(jj, ee)):
            cp.start(priority=c % 2)

    class rows_in:
        def __init__(self, row0, nrows, buf, sem):
            chunk = nrows // ROW_SPLIT
            self.copies = [
                pltpu.make_async_copy(src_hbm.at[pl.ds(row0 + c * chunk, chunk)],
                                      buf.at[pl.ds(c * chunk, chunk)], sem)
                for c in range(ROW_SPLIT)]

        def start(self, priority):
            for cp in self.copies:
                cp.start(priority=priority)

        def wait(self):
            for cp in self.copies:
                cp.wait()

    pair = 2 * MOE_TM
    n2 = n // 2
    has_single = n - 2 * n2 == 1
    row0 = g0 * MOE_TM
    single_row = row0 + n2 * pair

    def in_copy(u, slot):
        return rows_in(row0 + u * pair, pair, ibuf.at[slot], isem.at[slot])

    def out_copy(u, slot):
        return pltpu.make_async_copy(obuf.at[slot], dst_hbm.at[pl.ds(row0 + u * pair, pair), cols],
                                     osem.at[slot])

    def single_in():
        return rows_in(single_row, MOE_TM, sbuf, ssem.at[0])

    def single_out(row):
        return pltpu.make_async_copy(sobuf, dst_hbm.at[pl.ds(row, MOE_TM), cols], ssem.at[1])

    @pl.when(jnp.logical_and(j == 0, e == 0))
    def _():
        start_weights(j, e)

    @pl.when(n2 > 0)
    def _():
        in_copy(0, 0).start(priority=ROW_DMA_PRIORITY)

    @pl.when(has_single)
    def _():
        single_in().start(priority=ROW_DMA_PRIORITY)

    for cp in w_copies(j, e):
        cp.wait()
    cast_weights()
    last_e = e == ne - 1

    @pl.when(jnp.logical_not(jnp.logical_and(j == nj - 1, last_e)))
    def _():
        start_weights(jnp.where(last_e, j + 1, j), jnp.where(last_e, 0, e + 1))

    def body(t, carry):
        slot = lax.rem(t, 2)
        in_copy(t, slot).wait()

        @pl.when(t + 1 < n2)
        def _():
            in_copy(t + 1, 1 - slot).start(priority=ROW_DMA_PRIORITY)

        out = compute(ibuf[slot])

        @pl.when(t >= 2)
        def _():
            out_copy(t - 2, slot).wait()

        obuf[slot] = out
        out_copy(t, slot).start(priority=ROW_DMA_PRIORITY)
        return carry

    lax.fori_loop(0, n2, body, 0)
    for back in (2, 1):
        @pl.when(n2 >= back)
        def _():
            out_copy(n2 - back, lax.rem(n2 - back, 2)).wait()

    @pl.when(has_single)
    def _():
        single_in().wait()
        sobuf[...] = compute(sbuf[...])
        cp = single_out(single_row)
        cp.start(priority=ROW_DMA_PRIORITY)
        cp.wait()

    @pl.when(last_e)
    def _():
        sobuf[...] = jnp.zeros(sobuf.shape, sobuf.dtype)

        def fill(t, carry):
            cp = single_out(t * MOE_TM)
            cp.start()
            cp.wait()
            return carry

        lax.fori_loop(g0 + n, MOE_TILES, fill, 0)


def _moe_up_kernel(gs_ref, gn_ref, wg_hbm, wu_hbm, xs_hbm, h1_hbm, wg_stage, wu_stage, wgb_ref, wub_ref,
                   xbuf, obuf, sbuf, sobuf, wsem, xsem, osem, ssem):
    def cast_weights():
        wgb_ref[...] = wg_stage[...].astype(BF16)
        wub_ref[...] = wu_stage[...].astype(BF16)

    def compute(x):
        xb = x.astype(BF16)
        g = jnp.dot(xb, wgb_ref[...], preferred_element_type=F32)
        u = jnp.dot(xb, wub_ref[...], preferred_element_type=F32)
        return (_silu(g) * u).astype(BF16)

    _stream_expert_rows(gs_ref, gn_ref, (wg_hbm, wu_hbm), (wg_stage, wu_stage), wsem, xs_hbm, h1_hbm,
                        MOE_TN_A, xbuf, obuf, xsem, osem, sbuf, sobuf, ssem, cast_weights, compute)


def _moe_up(gs, gn, xs, w_gate, w_up):
    any_spec = pl.BlockSpec(memory_space=pl.ANY)
    grid_spec = pltpu.PrefetchScalarGridSpec(
        num_scalar_prefetch=2,
        grid=(D_EXPERT // MOE_TN_A, N_EXPERTS),
        in_specs=[any_spec, any_spec, any_spec],
        out_specs=any_spec,
        scratch_shapes=[
            pltpu.VMEM((D_MODEL, MOE_TN_A), F32),
            pltpu.VMEM((D_MODEL, MOE_TN_A), F32),
            pltpu.VMEM((D_MODEL, MOE_TN_A), BF16),
            pltpu.VMEM((D_MODEL, MOE_TN_A), BF16),
            pltpu.VMEM((2, 2 * MOE_TM, D_MODEL), F32),
            pltpu.VMEM((2, 2 * MOE_TM, MOE_TN_A), BF16),
            pltpu.VMEM((MOE_TM, D_MODEL), F32),
            pltpu.VMEM((MOE_TM, MOE_TN_A), BF16),
            pltpu.SemaphoreType.DMA((2,)),
            pltpu.SemaphoreType.DMA((2,)),
            pltpu.SemaphoreType.DMA((2,)),
            pltpu.SemaphoreType.DMA((2,)),
        ],
    )
    return pl.pallas_call(
        _moe_up_kernel,
        grid_spec=grid_spec,
        out_shape=jax.ShapeDtypeStruct((MOE_CAP, D_EXPERT), BF16),
        compiler_params=_cparams(("arbitrary", "arbitrary")),
        name="moe_up",
    )(gs, gn, w_gate, w_up, xs)


def _moe_down_kernel(gs_ref, gn_ref, wd_hbm, h1_hbm, y_hbm, wd_stage, wdb_ref, hbuf, obuf, sbuf, sobuf,
                     wsem, hsem, osem, ssem):
    def cast_weights():
        wdb_ref[...] = wd_stage[...].astype(BF16)

    def compute(h):
        return jnp.dot(h, wdb_ref[...], preferred_element_type=F32)

    _stream_expert_rows(gs_ref, gn_ref, (wd_hbm,), (wd_stage,), wsem, h1_hbm, y_hbm, MOE_TN_B,
                        hbuf, obuf, hsem, osem, sbuf, sobuf, ssem, cast_weights, compute)


def _moe_down(gs, gn, h1, w_down):
    any_spec = pl.BlockSpec(memory_space=pl.ANY)
    grid_spec = pltpu.PrefetchScalarGridSpec(
        num_scalar_prefetch=2,
        grid=(D_MODEL // MOE_TN_B, N_EXPERTS),
        in_specs=[any_spec, any_spec],
        out_specs=any_spec,
        scratch_shapes=[
            pltpu.VMEM((D_EXPERT, MOE_TN_B), F32),
            pltpu.VMEM((D_EXPERT, MOE_TN_B), BF16),
            pltpu.VMEM((2, 2 * MOE_TM, D_EXPERT), BF16),
            pltpu.VMEM((2, 2 * MOE_TM, MOE_TN_B), F32),
            pltpu.VMEM((MOE_TM, D_EXPERT), BF16),
            pltpu.VMEM((MOE_TM, MOE_TN_B), F32),
            pltpu.SemaphoreType.DMA((1,)),
            pltpu.SemaphoreType.DMA((2,)),
            pltpu.SemaphoreType.DMA((2,)),
            pltpu.SemaphoreType.DMA((2,)),
        ],
    )
    return pl.pallas_call(
        _moe_down_kernel,
        grid_spec=grid_spec,
        out_shape=jax.ShapeDtypeStruct((MOE_CAP, D_MODEL), F32),
        compiler_params=_cparams(("arbitrary", "arbitrary")),
        name="moe_down",
    )(gs, gn, w_down, h1)


def _combine_kernel(pos_ref, info_ref, g_ref, h_hbm, y_hbm, o_ref, hbuf, ybuf, sem_h, sem_y):
    i = pl.program_id(0)
    slot = lax.rem(i, 2)

    def h_copy(t, s):
        return pltpu.make_async_copy(h_hbm.at[pl.ds(BLK + t * OUT_T, OUT_T)], hbuf.at[s], sem_h.at[s])

    def fetch(t, s):
        h_copy(t, s).start()

        def issue(r, carry):
            base = (t * OUT_T + r) * 2
            for k in range(2):
                pltpu.make_async_copy(y_hbm.at[pl.ds(pos_ref[base + k], 1)],
                                      ybuf.at[s, k, pl.ds(r, 1)], sem_y.at[s]).start()
            return carry

        lax.fori_loop(0, OUT_T, issue, 0, unroll=4)

    @pl.when(i == 0)
    def _():
        fetch(0, 0)

    @pl.when(i + 1 < pl.num_programs(0))
    def _():
        fetch(i + 1, 1 - slot)

    h_copy(i, slot).wait()
    for k in range(2):
        pltpu.make_async_copy(y_hbm.at[pl.ds(0, OUT_T)], ybuf.at[slot, k], sem_y.at[slot]).wait()
    g1 = info_ref[:, INFO_G1:INFO_G1 + 1]
    g2 = info_ref[:, INFO_G2:INFO_G2 + 1]
    h_new = hbuf[slot] + g1 * ybuf[slot, 0] + g2 * ybuf[slot, 1]
    o_ref[...] = _rms(h_new, g_ref[...])


def _combine(pos_x, info_x, g_final, h, y):
    grid_spec = pltpu.PrefetchScalarGridSpec(
        num_scalar_prefetch=1,
        grid=(SEQ // OUT_T,),
        in_specs=[
            pl.BlockSpec((OUT_T, LANE), lambda i, pos: (i, 0)),
            pl.BlockSpec((1, D_MODEL), lambda i, pos: (0, 0)),
            pl.BlockSpec(memory_space=pl.ANY),
            pl.BlockSpec(memory_space=pl.ANY),
        ],
        out_specs=pl.BlockSpec((OUT_T, D_MODEL), lambda i, pos: (i, 0)),
        scratch_shapes=[
            pltpu.VMEM((2, OUT_T, D_MODEL), F32),
            pltpu.VMEM((2, 2, OUT_T, D_MODEL), F32),
            pltpu.SemaphoreType.DMA((2,)),
            pltpu.SemaphoreType.DMA((2,)),
        ],
    )
    return pl.pallas_call(
        _combine_kernel,
        grid_spec=grid_spec,
        out_shape=jax.ShapeDtypeStruct((SEQ, D_MODEL), F32),
        compiler_params=_cparams(("arbitrary",)),
        name="combine",
    )(pos_x, info_x, g_final, h, y)


def _mixer(h, hn, layer, p, g_next, hn_dtype):
    w_in = p["w_in"][layer]
    a0 = 3 * ATTN_WIDTH
    u0 = a0 + N_HEADS
    w_qkv = w_in[:, :a0].astype(BF16)
    w_f = jnp.pad(w_in[:, a0:u0], ((0, 0), (0, LANE - N_HEADS))).astype(BF16)
    w_a = w_in[:, u0:u0 + CONV_WIDTH].astype(BF16)
    w_g = w_in[:, u0 + CONV_WIDTH:].astype(BF16)
    b_f = jnp.pad(p["b_forget"][layer], (0, LANE - N_HEADS)).reshape(1, LANE)

    qkv = _qkv(hn, w_qkv)
    key_bias = _decay(hn, w_f, b_f)
    attn = _attention(qkv, key_bias, p["attn_out_g"][layer].reshape(N_HEADS, 1, HEAD_DIM))

    hc = _glu(hn, w_a, w_g)
    w_dw = jnp.pad(p["w_dw"][layer], ((0, HALO - CONV_KERNEL), (0, 0)))
    row = lambda v: v.reshape(1, -1)
    conv = _conv(hc, w_dw, row(p["b_dw"][layer]), row(p["conv_ln_g"][layer]),
                 row(p["conv_ln_b"][layer]), p["w_conv_out"][layer].astype(BF16),
                 row(p["conv_out_g"][layer]))
    return _outproj(attn, conv, p["w_out"][layer].astype(BF16), h, g_next, hn_dtype)


def _moe(h, hn, p, j, g_final):
    w_r = jnp.pad(p["moe_w_router"][j], ((0, 0), (0, LANE - N_EXPERTS)))
    info, cnt = _router(hn, w_r)
    e1 = info[:, INFO_E1].astype(jnp.int32)
    e2 = info[:, INFO_E2].astype(jnp.int32)
    r1 = info[:, INFO_R1].astype(jnp.int32)
    r2 = info[:, INFO_R2].astype(jnp.int32)
    counts = cnt[0, :N_EXPERTS].astype(jnp.int32)
    padded = (counts + MOE_TM - 1) // MOE_TM * MOE_TM
    gend = jnp.cumsum(padded)
    gstart = gend - padded
    pos = jnp.stack([gstart[e1] + r1, gstart[e2] + r2], axis=-1).reshape(-1)
    gs = (gstart // MOE_TM).astype(jnp.int32)
    gn = (padded // MOE_TM).astype(jnp.int32)

    xs = _dispatch(pos, hn)
    h1 = _moe_up(gs, gn, xs, p["moe_w_gate"][j], p["moe_w_up"][j])
    y = _moe_down(gs, gn, h1, p["moe_w_down"][j])
    return _combine(pos[2 * BLK:], info[BLK:], g_final, h, y)


def kernel(x, meta_tokens, mix_norm_g, ffn_norm_g, w_in, b_forget, w_dw, b_dw, conv_ln_g, conv_ln_b, w_conv_out, attn_out_g, conv_out_g, w_out, dense_w_gate, dense_w_up, dense_w_down, moe_w_router, moe_w_gate, moe_w_up, moe_w_down, final_norm_g):
    assert x.shape == (1, SEQ, D_MODEL) and meta_tokens.shape == (N_META, D_MODEL)
    p = dict(w_in=w_in, b_forget=b_forget, w_dw=w_dw, b_dw=b_dw, conv_ln_g=conv_ln_g,
             conv_ln_b=conv_ln_b, w_conv_out=w_conv_out, attn_out_g=attn_out_g,
             conv_out_g=conv_out_g, w_out=w_out, moe_w_router=moe_w_router,
             moe_w_gate=moe_w_gate, moe_w_up=moe_w_up, moe_w_down=moe_w_down)
    row = lambda v: v.reshape(1, D_MODEL)
    meta_blk = jnp.pad(meta_tokens.astype(F32), ((PAD, 0), (0, 0)))
    h, hn = _prep(meta_blk, x.reshape(SEQ, D_MODEL), row(mix_norm_g[0]))

    h, hn = _mixer(h, hn, 0, p, row(ffn_norm_g[0]), BF16)
    h, hn = _ffn(hn, dense_w_gate[0].astype(BF16), dense_w_up[0].astype(BF16),
                 dense_w_down[0].astype(BF16), h, row(mix_norm_g[1]))
    h, hn = _mixer(h, hn, 1, p, row(ffn_norm_g[1]), F32)
    out = _moe(h, hn, p, 0, row(final_norm_g))
    return out.reshape(1, SEQ, D_MODEL)
```
